```python
import math
import jax
import jax.numpy as jnp
from jax import lax
import numpy as np

D_MODEL = 4096
BATCH = 2
SEQ = 4096
DEPTH = 1
DEC_BATCH = 32
DEC_SEQ = 8
PAST_LEN = 8192
PAGE_SIZE = 128

HEAD_DIM = 128
MIX_WIDTH = D_MODEL
N_HEADS_TOTAL = MIX_WIDTH // HEAD_DIM
NSA_HEADS = N_HEADS_TOTAL // 2
FOX_HEADS = N_HEADS_TOTAL - NSA_HEADS
NSA_KV_HEADS = 4
FOX_KV_HEADS = 4
NSA_REP = NSA_HEADS // NSA_KV_HEADS
FOX_REP = FOX_HEADS // FOX_KV_HEADS
NSA_WIDTH = NSA_HEADS * HEAD_DIM
FOX_WIDTH = FOX_HEADS * HEAD_DIM
N_BRANCH = 3
CMP_BLOCK = 32
CMP_STRIDE = 16
CMP_RATIO = CMP_BLOCK // CMP_STRIDE
SLC_BLOCK = 64
SLC_TOPK = 16
SLC_GATHER_ROWS = 128
WINDOW = 512
WIN_QBLOCK = 128
FOX_QBLOCK = 128
ROPE_THETA = 10000.0
RMS_EPS = 1e-6
N_GROUPS = 4
EXPERTS_PER_GROUP = 8
N_EXPERTS = N_GROUPS * EXPERTS_PER_GROUP
EXPERT_TOPK = 2
D_EXPERT = D_MODEL // 4
MOE_BLOCK = 128
ATTN_SCALE = HEAD_DIM ** -0.5

COLS_NSA_Q = NSA_WIDTH
COLS_NSA_KV = N_BRANCH * 2 * NSA_KV_HEADS * HEAD_DIM
COLS_NSA_GATE = N_BRANCH * NSA_HEADS
COLS_FOX_Q = FOX_WIDTH
COLS_FOX_KV = 2 * FOX_KV_HEADS * HEAD_DIM
COLS_FOX_F = FOX_HEADS
IN_COLS = COLS_NSA_Q + COLS_NSA_KV + COLS_NSA_GATE + COLS_FOX_Q + COLS_FOX_KV + COLS_FOX_F

kernel_name = 'hymba_nsa_fox_hmoe_decode_step'


def _split(x, c, axis):
    t = x.shape[axis]
    x = x.reshape(x.shape[:axis] + (t // c, c) + x.shape[axis + 1:])
    return jnp.moveaxis(x, axis, 0)


def _merge(y, axis):
    y = jnp.moveaxis(y, 0, axis)
    return y.reshape(y.shape[:axis] + (-1,) + y.shape[axis + 2:])


def rms_norm(x, g):
    xf = x.astype(jnp.float32)
    y = xf * lax.rsqrt(jnp.mean(xf * xf, axis=-1, keepdims=True) + RMS_EPS)
    return (y * g.astype(jnp.float32)).astype(x.dtype)


def rope(x, pos):
    d = x.shape[-1]
    half = d // 2
    inv = jnp.exp(-math.log(ROPE_THETA) * jnp.arange(half, dtype=jnp.float32) * (2.0 / d))
    ang = pos.astype(jnp.float32)[:, None] * inv[None, :]
    shape = (1, pos.shape[0]) + (1,) * (x.ndim - 3) + (half,)
    cos, sin = jnp.cos(ang).reshape(shape), jnp.sin(ang).reshape(shape)
    xf = x.astype(jnp.float32)
    x1, x2 = xf[..., :half], xf[..., half:]
    return jnp.concatenate([x1 * cos - x2 * sin, x2 * cos + x1 * sin], axis=-1).astype(x.dtype)


def masked_softmax(s, mask):
    s = jnp.where(mask, s, -jnp.inf)
    m = jnp.max(s, axis=-1, keepdims=True)
    p = jnp.exp(s - jnp.where(jnp.isfinite(m), m, 0.0))
    return p / jnp.maximum(jnp.sum(p, axis=-1, keepdims=True), jnp.finfo(jnp.float32).tiny)


def gqa_attend(q, k, v, mask, bias=None):
    s = jnp.einsum('...qgrd,...sgd->...grqs', q, k).astype(jnp.float32) * ATTN_SCALE
    if bias is not None:
        s = s + bias
    p = masked_softmax(s, mask)
    return jnp.einsum('...grqs,...sgd->...qgrd', p.astype(v.dtype), v)


def project(xn, pos, lw):
    B, T, _ = xn.shape
    h = jnp.einsum('btd,dc->btc', xn, lw['w_in'])
    o1 = COLS_NSA_Q
    o2 = o1 + COLS_NSA_KV
    o3 = o2 + COLS_NSA_GATE
    o4 = o3 + COLS_FOX_Q
    o5 = o4 + COLS_FOX_KV
    q_nsa = h[..., :o1].reshape(B, T, NSA_HEADS, HEAD_DIM)
    kv_nsa = h[..., o1:o2].reshape(B, T, N_BRANCH, 2, NSA_KV_HEADS, HEAD_DIM)
    gate = jax.nn.sigmoid((h[..., o2:o3].reshape(B, T, NSA_HEADS, N_BRANCH) + lw['b_nsa_gate']).astype(jnp.float32))
    q_fox = h[..., o3:o4].reshape(B, T, FOX_HEADS, HEAD_DIM)
    kv_fox = h[..., o4:o5].reshape(B, T, 2, FOX_KV_HEADS, HEAD_DIM)
    logf = jax.nn.log_sigmoid((h[..., o5:] + lw['b_fox_forget']).astype(jnp.float32))
    q_nsa = rope(rms_norm(q_nsa, lw['g_nsa_q']), pos)
    k_nsa = rope(rms_norm(kv_nsa[:, :, :, 0], lw['g_nsa_k'][:, None, :]), pos)
    kv_nsa = jnp.stack([k_nsa, kv_nsa[:, :, :, 1]], axis=3)
    kv_fox = jnp.stack([rms_norm(kv_fox[:, :, 0], lw['g_fox_k']), kv_fox[:, :, 1]], axis=2)
    q_fox = rms_norm(q_fox, lw['g_fox_q'])
    return (q_nsa.reshape(B, T, NSA_KV_HEADS, NSA_REP, HEAD_DIM), gate,
            kv_nsa[:, :, 0], kv_nsa[:, :, 1], kv_nsa[:, :, 2],
            q_fox.reshape(B, T, FOX_KV_HEADS, FOX_REP, HEAD_DIM), kv_fox, logf)


def compress(kv, lw):
    B, T, _, G, D = kv.shape
    n_chunks = -(-T // CMP_STRIDE)
    kvp = jnp.pad(kv, ((0, 0), (0, n_chunks * CMP_STRIDE - T), (0, 0), (0, 0), (0, 0)))
    ch = kvp.reshape(B, n_chunks, CMP_STRIDE, 2, G, D)
    n_blk = n_chunks - CMP_RATIO + 1
    outs = []
    for i, (w, pe) in enumerate(((lw['w_cmp_k'], lw['pe_cmp_k']), (lw['w_cmp_v'], lw['pe_cmp_v']))):
        w = w.reshape(CMP_RATIO, CMP_STRIDE, D, D)
        pe_term = jnp.einsum('rjd,rjde->re', pe.reshape(CMP_RATIO, CMP_STRIDE, D), w)
        part = jnp.einsum('bcjgd,rjde->bcrge', ch[:, :, :, i], w) + pe_term[None, None, :, None, :]
        outs.append(sum(part[:, r:r + n_blk, r] for r in range(CMP_RATIO)))
    return outs[0], outs[1]


def cmp_attend(q, kc, vc, qpos):
    s = jnp.einsum('bqgrd,bngd->bgrqn', q, kc).astype(jnp.float32) * ATTN_SCALE
    end = jnp.arange(kc.shape[1]) * CMP_STRIDE + CMP_BLOCK - 1
    p = masked_softmax(s, end[None, :] <= qpos[:, None])
    o = jnp.einsum('bgrqn,bngd->bqgrd', p.astype(vc.dtype), vc)
    return o, p


def select_blocks(p_cmp, qpos, n_slc):
    n_slc = max(n_slc, SLC_TOPK)
    cs = jnp.arange(p_cmp.shape[-1]) * CMP_STRIDE
    ss = jnp.arange(n_slc) * SLC_BLOCK
    overlap = ((cs[:, None] < ss[None, :] + SLC_BLOCK) & (cs[:, None] + CMP_BLOCK > ss[None, :])).astype(jnp.float32)
    imp = jnp.einsum('bgrqn,nj->bqgj', p_cmp, overlap)
    cur = (qpos // SLC_BLOCK)[:, None]
    j = jnp.arange(n_slc)[None, :]
    forced = (j == 0) | (j == cur) | (j == cur - 1)
    causal = ss[None, :] <= qpos[:, None]
    imp = jnp.where(forced[None, :, None, :], jnp.inf, imp)
    imp = jnp.where(causal[None, :, None, :], imp, -jnp.inf)
    val, idx = lax.top_k(imp, SLC_TOPK)
    return idx, val > -jnp.inf


def slc_attend(q, top_idx, top_ok, qpos, fetch):
    B, Tq = q.shape[:2]
    c = math.gcd(Tq, max(1, SLC_GATHER_ROWS // B))

    def block(args):
        qb, ib, ok, pb = args
        b_, c_, g_, k_ = ib.shape
        rpos = ib[..., None] * SLC_BLOCK + jnp.arange(SLC_BLOCK)
        m = ok[..., None] & (rpos <= pb[None, :, None, None, None])
        rows = rpos.reshape(b_, c_, g_, k_ * SLC_BLOCK)
        kv = fetch(rows)
        s = jnp.einsum('bqgrd,bqgld->bqgrl', qb, kv[..., 0, :]).astype(jnp.float32) * ATTN_SCALE
        p = masked_softmax(s, m.reshape(b_, c_, g_, 1, k_ * SLC_BLOCK))
        return jnp.einsum('bqgrl,bqgld->bqgrd', p.astype(kv.dtype), kv[..., 1, :])

    out = lax.map(block, (_split(q, c, 1), _split(top_idx, c, 1), _split(top_ok, c, 1), _split(qpos, c, 0)))
    return _merge(out, 1)


def window_banded(q, kv):
    B, T, G, R, D = q.shape
    nb = T // WIN_QBLOCK
    nw = WINDOW // WIN_QBLOCK
    kvp = jnp.pad(kv, ((0, 0), (WINDOW, 0), (0, 0), (0, 0), (0, 0))).reshape(B, nb + nw, WIN_QBLOCK, 2, G, D)
    band = jnp.concatenate([kvp[:, i:i + nb] for i in range(nw + 1)], axis=2)
    qpos = jnp.arange(T).reshape(nb, WIN_QBLOCK)
    kpos = (jnp.arange(nb)[:, None] - nw) * WIN_QBLOCK + jnp.arange((nw + 1) * WIN_QBLOCK)[None, :]
    kp, qp = kpos[:, None, :], qpos[:, :, None]
    mask = (kp >= 0) & (kp <= qp) & (qp - kp < WINDOW)
    o = gqa_attend(q.reshape(B, nb, WIN_QBLOCK, G, R, D), band[:, :, :, 0], band[:, :, :, 1],
                   mask[None, :, None, None])
    return o.reshape(B, T, G, R, D)


def fox_attend(q, cq, qpos, kv, c_ctx):
    B, Tq, G, R, D = q.shape
    S = kv.shape[1]
    k, v = kv[:, :, 0], kv[:, :, 1]
    kpos = jnp.arange(S)
    ck = c_ctx.reshape(B, S, G, R).transpose(0, 2, 3, 1)[:, :, :, None, :]

    def block(args):
        qb, cb, pb = args
        bias = cb.reshape(B, -1, G, R).transpose(0, 2, 3, 1)[..., None] - ck
        return gqa_attend(qb, k, v, kpos[None, :] <= pb[:, None], bias)

    c = math.gcd(Tq, FOX_QBLOCK)
    out = lax.map(block, (_split(q, c, 1), _split(cq, c, 1), _split(qpos, c, 0)))
    return _merge(out, 1)


def merge_heads(o_cmp, o_slc, o_win, gate, o_fox, lw):
    B, T = o_cmp.shape[:2]
    r = lambda o: o.reshape(B, T, NSA_HEADS, HEAD_DIM)
    o_nsa = (gate[..., 0:1] * r(o_cmp) + gate[..., 1:2] * r(o_slc) + gate[..., 2:3] * r(o_win))
    o_nsa = rms_norm(o_nsa.reshape(B, T, NSA_WIDTH).astype(o_cmp.dtype), lw['g_out_nsa'])
    o_fox = rms_norm(o_fox.reshape(B, T, FOX_WIDTH), lw['g_out_fox'])
    return jnp.einsum('btc,cd->btd', jnp.concatenate([o_nsa, o_fox], axis=-1), lw['w_out'])


def mixer_prompt(xn, lw):
    B, T, _ = xn.shape
    pos = jnp.arange(T, dtype=jnp.int32)
    q_nsa, gate, kv_cmp, kv_slc, kv_win, q_fox, kv_fox, logf = project(xn, pos, lw)
    kc, vc = compress(kv_cmp, lw)
    o_cmp, p_cmp = cmp_attend(q_nsa, kc, vc, pos)
    top_idx, top_ok = select_blocks(p_cmp, pos, -(-T // SLC_BLOCK))
    bi = jnp.arange(B)[:, None, None, None]
    gi = jnp.arange(NSA_KV_HEADS)[None, None, :, None]
    fetch = lambda rows: kv_slc[bi, jnp.minimum(rows, T - 1), :, gi]
    o_slc = slc_attend(q_nsa, top_idx, top_ok, pos, fetch)
    o_win = window_banded(q_nsa, kv_win)
    c = jnp.cumsum(logf, axis=1)
    o_fox = fox_attend(q_fox, c, pos, kv_fox, c)
    y = merge_heads(o_cmp, o_slc, o_win, gate, o_fox, lw)
    return y, (kv_cmp, kv_slc, kv_win[:, T - min(WINDOW, T):], kv_fox, logf)


def mixer_sample(xn, l, c_cmp, c_slc, c_win, c_fox, c_logf, page_table, lw):
    B, Tn, _ = xn.shape
    past = page_table.shape[1] * PAGE_SIZE
    T = past + Tn
    pos = past + jnp.arange(Tn, dtype=jnp.int32)
    q_nsa, gate, kv_cmp, kv_slc, kv_win, q_fox, kv_fox, logf = project(xn, pos, lw)

    def gather_past(pool):
        g = pool[l, page_table]
        return g.reshape((B, past) + pool.shape[3:])

    ctx_cmp = jnp.concatenate([gather_past(c_cmp), kv_cmp], axis=1)
    kc, vc = compress(ctx_cmp, lw)
    o_cmp, p_cmp = cmp_attend(q_nsa, kc, vc, pos)
    top_idx, top_ok = select_blocks(p_cmp, pos, -(-T // SLC_BLOCK))
    pool = c_slc.reshape((c_slc.shape[0], -1) + c_slc.shape[3:])
    bi = jnp.arange(B)[:, None, None, None]
    gi = jnp.arange(NSA_KV_HEADS)[None, None, :, None]

    def fetch(rows):
        pr = jnp.clip(rows, 0, past - 1)
        phys = page_table[bi, pr // PAGE_SIZE] * PAGE_SIZE + pr % PAGE_SIZE
        nr = jnp.clip(rows - past, 0, Tn - 1)
        return jnp.where((rows < past)[..., None, None], pool[l, phys, :, gi], kv_slc[bi, nr, :, gi])

    o_slc = slc_attend(q_nsa, top_idx, top_ok, pos, fetch)
    win_len = c_win.shape[2]
    buf = jnp.concatenate([c_win[l], kv_win], axis=1)
    kpos = past - win_len + jnp.arange(win_len + Tn)
    wmask = (kpos[None, :] <= pos[:, None]) & (pos[:, None] - kpos[None, :] < WINDOW)
    o_win = gqa_attend(q_nsa, buf[:, :, 0], buf[:, :, 1], wmask)
    ctx_fox = jnp.concatenate([gather_past(c_fox), kv_fox], axis=1)
    c = jnp.cumsum(jnp.concatenate([gather_past(c_logf).astype(jnp.float32), logf], axis=1), axis=1)
    o_fox = fox_attend(q_fox, c[:, past:], pos, ctx_fox, c)
    y = merge_heads(o_cmp, o_slc, o_win, gate, o_fox, lw)
    return y, (kv_cmp, kv_slc, buf[:, Tn:], kv_fox, logf)


def grouped_experts(x, eid, gate, lw):
    N, K = eid.shape
    A = N * K
    flat = eid.reshape(A)
    order = jnp.argsort(flat)
    se = flat[order]
    tok = order // K
    counts = jnp.bincount(flat, length=N_EXPERTS)
    starts = jnp.cumsum(counts) - counts
    pcounts = (counts + MOE_BLOCK - 1) // MOE_BLOCK * MOE_BLOCK
    pends = jnp.cumsum(pcounts)
    dest = (pends - pcounts)[se] + jnp.arange(A) - starts[se]
    n_blocks = -(-A // MOE_BLOCK) + N_EXPERTS
    xb = jnp.zeros((n_blocks * MOE_BLOCK, x.shape[1]), x.dtype).at[dest].set(x[tok])
    block_e = jnp.minimum(jnp.searchsorted(pends, jnp.arange(n_blocks) * MOE_BLOCK, side='right'), N_EXPERTS - 1)
    w_g, w_u, w_d = lw['w_exp_gate'], lw['w_exp_up'], lw['w_exp_down']

    def run(args):
        xblk, e = args
        return (jax.nn.silu(xblk @ w_g[e]) * (xblk @ w_u[e])) @ w_d[e]

    yb = lax.map(run, (xb.reshape(n_blocks, MOE_BLOCK, -1), block_e)).reshape(n_blocks * MOE_BLOCK, -1)
    y = yb[dest] * gate.reshape(A)[order][:, None].astype(x.dtype)
    return jnp.zeros_like(x).at[tok].add(y)


def hier_moe(x, lw):
    N = x.shape[0]
    grp_logit = jnp.einsum('nd,dg->ng', x, lw['w_router_grp']).astype(jnp.float32) + lw['b_router_grp']
    p_grp = jax.nn.softmax(grp_logit, axis=-1)
    _, g_sel = lax.top_k(grp_logit, 1)
    exp_logit = (jnp.einsum('nd,de->ne', x, lw['w_router_exp']).astype(jnp.float32)
                 + lw['b_router_exp']).reshape(N, N_GROUPS, EXPERTS_PER_GROUP)
    in_grp = jnp.take_along_axis(exp_logit, g_sel[:, :, None], axis=1)[:, 0]
    top_v, top_l = lax.top_k(in_grp, EXPERT_TOPK)
    gate = jnp.take_along_axis(p_grp, g_sel, axis=1) * jax.nn.softmax(top_v, axis=-1)
    eid = g_sel * EXPERTS_PER_GROUP + top_l
    return grouped_experts(x, eid, gate, lw)


def channel_mixer(h, lw):
    B, T, D = h.shape
    return hier_moe(rms_norm(h, lw['g_ffn_norm']).reshape(B * T, D), lw).reshape(B, T, D)


def setup_inputs(seed: int = 0) -> dict:
    key = jax.random.key(seed)
    keys = iter(jax.random.split(key, 40))

    def nrm(shape, scale=1.0):
        return jax.random.normal(next(keys), shape, jnp.float32) * scale

    def gain(shape):
        return 1.0 + nrm(shape, 0.02)

    L = DEPTH
    n_pages = PAST_LEN // PAGE_SIZE
    used = DEC_BATCH * n_pages
    n_phys = used + max(1, used // 4)
    win_len = min(WINDOW, PAST_LEN)
    page_table = jax.random.permutation(next(keys), n_phys)[:used].reshape(DEC_BATCH, n_pages).astype(jnp.int32)
    d_in = D_MODEL ** -0.5
    return {
        'x_prompt': nrm((BATCH, SEQ, D_MODEL)),
        'x_sample': nrm((DEC_BATCH, DEC_SEQ, D_MODEL)),
        'cache_nsa_cmp_kv': nrm((L, n_phys, PAGE_SIZE, 2, NSA_KV_HEADS, HEAD_DIM)),
        'cache_nsa_slc_kv': nrm((L, n_phys, PAGE_SIZE, 2, NSA_KV_HEADS, HEAD_DIM)),
        'cache_nsa_win_kv': nrm((L, DEC_BATCH, win_len, 2, NSA_KV_HEADS, HEAD_DIM)),
        'cache_fox_kv': nrm((L, n_phys, PAGE_SIZE, 2, FOX_KV_HEADS, HEAD_DIM)),
        'cache_fox_logf': jax.nn.log_sigmoid(2.0 + nrm((L, n_phys, PAGE_SIZE, FOX_HEADS), 0.5)),
        'page_table': page_table,
        'g_attn_norm': gain((L, D_MODEL)),
        'w_in': nrm((L, D_MODEL, IN_COLS), d_in),
        'b_nsa_gate': nrm((L, NSA_HEADS, N_BRANCH), 0.1),
        'b_fox_forget': 2.0 + nrm((L, FOX_HEADS), 0.5),
        'g_nsa_q': gain((L, HEAD_DIM)),
        'g_nsa_k': gain((L, N_BRANCH, HEAD_DIM)),
        'g_fox_q': gain((L, HEAD_DIM)),
        'g_fox_k': gain((L, HEAD_DIM)),
        'w_cmp_k': nrm((L, CMP_BLOCK, HEAD_DIM, HEAD_DIM), (CMP_BLOCK * HEAD_DIM) ** -0.5),
        'w_cmp_v': nrm((L, CMP_BLOCK, HEAD_DIM, HEAD_DIM), (CMP_BLOCK * HEAD_DIM) ** -0.5),
        'pe_cmp_k': nrm((L, CMP_BLOCK, HEAD_DIM), 0.1),
        'pe_cmp_v': nrm((L, CMP_BLOCK, HEAD_DIM), 0.1),
        'g_out_nsa': gain((L, NSA_WIDTH)),
        'g_out_fox': gain((L, FOX_WIDTH)),
        'w_out': nrm((L, MIX_WIDTH, D_MODEL), MIX_WIDTH ** -0.5),
        'g_ffn_norm': gain((L, D_MODEL)),
        'w_router_grp': nrm((L, D_MODEL, N_GROUPS), d_in),
        'b_router_grp': nrm((L, N_GROUPS), 0.01),
        'w_router_exp': nrm((L, D_MODEL, N_EXPERTS), d_in),
        'b_router_exp': nrm((L, N_EXPERTS), 0.01),
        'w_exp_gate': nrm((L, N_EXPERTS, D_MODEL, D_EXPERT), d_in),
        'w_exp_up': nrm((L, N_EXPERTS, D_MODEL, D_EXPERT), d_in),
        'w_exp_down': nrm((L, N_EXPERTS, D_EXPERT, D_MODEL), D_EXPERT ** -0.5),
    }


def reference(x_prompt, x_sample, cache_nsa_cmp_kv, cache_nsa_slc_kv, cache_nsa_win_kv, cache_fox_kv,
              cache_fox_logf, page_table, g_attn_norm, w_in, b_nsa_gate, b_fox_forget, g_nsa_q, g_nsa_k,
              g_fox_q, g_fox_k, w_cmp_k, w_cmp_v, pe_cmp_k, pe_cmp_v, g_out_nsa, g_out_fox, w_out,
              g_ffn_norm, w_router_grp, b_router_grp, w_router_exp, b_router_exp, w_exp_gate, w_exp_up,
              w_exp_down):
    hp, hs = x_prompt, x_sample
    outs_p, outs_s = [], []
    for l in range(DEPTH):
        lw = dict(g_attn_norm=g_attn_norm[l], w_in=w_in[l], b_nsa_gate=b_nsa_gate[l],
                  b_fox_forget=b_fox_forget[l], g_nsa_q=g_nsa_q[l], g_nsa_k=g_nsa_k[l],
                  g_fox_q=g_fox_q[l], g_fox_k=g_fox_k[l], w_cmp_k=w_cmp_k[l], w_cmp_v=w_cmp_v[l],
                  pe_cmp_k=pe_cmp_k[l], pe_cmp_v=pe_cmp_v[l], g_out_nsa=g_out_nsa[l],
                  g_out_fox=g_out_fox[l], w_out=w_out[l], g_ffn_norm=g_ffn_norm[l],
                  w_router_grp=w_router_grp[l], b_router_grp=b_router_grp[l],
                  w_router_exp=w_router_exp[l], b_router_exp=b_router_exp[l],
                  w_exp_gate=w_exp_gate[l], w_exp_up=w_exp_up[l], w_exp_down=w_exp_down[l])
        yp, st_p = mixer_prompt(rms_norm(hp, lw['g_attn_norm']), lw)
        hp = hp + yp
        hp = hp + channel_mixer(hp, lw)
        ys, st_s = mixer_sample(rms_norm(hs, lw['g_attn_norm']), l, cache_nsa_cmp_kv, cache_nsa_slc_kv,
                                cache_nsa_win_kv, cache_fox_kv, cache_fox_logf, page_table, lw)
        hs = hs + ys
        hs = hs + channel_mixer(hs, lw)
        outs_p.append(st_p)
        outs_s.append(st_s)
    stk = lambda outs, i: jnp.stack([o[i] for o in outs], axis=0)
    return (hp, hs, stk(outs_p, 0), stk(outs_s, 0), stk(outs_p, 1), stk(outs_s, 1), stk(outs_p, 2),
            stk(outs_s, 2), stk(outs_p, 3), stk(outs_s, 3), stk(outs_p, 4), stk(outs_s, 4))
```

```python
import functools
import math

import jax
import jax.numpy as jnp
from jax import lax
from jax.experimental import pallas as pl
from jax.experimental.pallas import tpu as pltpu

HEAD_DIM = 128
NSA_HEADS = 16
FOX_HEADS = 16
KV_HEADS = 4
REP = NSA_HEADS // KV_HEADS
N_BRANCH = 3
CMP_BLOCK = 32
CMP_STRIDE = 16
CMP_RATIO = CMP_BLOCK // CMP_STRIDE
SLC_BLOCK = 64
SLC_TOPK = 16
WINDOW = 512
ROPE_THETA = 10000.0
RMS_EPS = 1e-6
N_GROUPS = 4
EXPERTS_PER_GROUP = 8
N_EXPERTS = N_GROUPS * EXPERTS_PER_GROUP
EXPERT_TOPK = 2
ATTN_SCALE = HEAD_DIM ** -0.5

LANES = 128
GROUP_COLS = REP * HEAD_DIM
KV_COLS = 2 * KV_HEADS * HEAD_DIM
NEG = -1e30
VMEM_LIMIT = 48 * 1024 * 1024
PAGES_PER_STEP = 8

COL_Q_NSA = 0
COL_KV_NSA = NSA_HEADS * HEAD_DIM
COL_Q_FOX = COL_KV_NSA + N_BRANCH * KV_COLS
COL_KV_FOX = COL_Q_FOX + FOX_HEADS * HEAD_DIM
MAIN_COLS = COL_KV_FOX + KV_COLS
N_GATE = N_BRANCH * NSA_HEADS
COL_LOGF = N_GATE


def _pick(n, cands):
    for c in cands:
        if n % c == 0:
            return c
    raise ValueError(f"no tile in {cands} divides {n}")


def _cparams(sem, vmem=VMEM_LIMIT):
    return pltpu.CompilerParams(dimension_semantics=sem, vmem_limit_bytes=vmem)


def _split3(x):
    hi = x.astype(jnp.bfloat16)
    r1 = x - hi.astype(jnp.float32)
    mid = r1.astype(jnp.bfloat16)
    lo = (r1 - mid.astype(jnp.float32)).astype(jnp.bfloat16)
    return hi, mid, lo


def _dot(a, b):
    return jnp.dot(a, b, preferred_element_type=jnp.float32)


def _dot_nt(a, b):
    return lax.dot_general(a, b, (((1,), (1,)), ((), ())), preferred_element_type=jnp.float32)


def _rms_kernel(x_ref, g_ref, o_ref):
    x = x_ref[...]
    ms = jnp.mean(x * x, axis=-1, keepdims=True)
    o_ref[...] = (x * lax.rsqrt(ms + RMS_EPS) * g_ref[...]).astype(o_ref.dtype)


def _rmsnorm(x, g, out_dtype):
    n, d = x.shape
    tm = _pick(n, (256, 128, 64, 32, 16, 8))
    return pl.pallas_call(
        _rms_kernel,
        grid=(n // tm,),
        in_specs=[pl.BlockSpec((tm, d), lambda i: (i, 0)), pl.BlockSpec((1, d), lambda i: (0, 0))],
        out_specs=pl.BlockSpec((tm, d), lambda i: (i, 0)),
        out_shape=jax.ShapeDtypeStruct((n, d), out_dtype),
        compiler_params=_cparams(("parallel",)),
        name="rmsnorm",
    )(x, g.reshape(1, d))


N_COLBLK = MAIN_COLS // GROUP_COLS


def _colblock_kinds():
    kinds = []
    kinds += ["rope"] * (NSA_HEADS // REP)
    for _ in range(N_BRANCH):
        kinds += ["rope", "id"]
    kinds += ["norm"] * (FOX_HEADS // REP)
    kinds += ["norm", "id"]
    assert len(kinds) == N_COLBLK
    return kinds


def _any_eq(j, vals):
    return functools.reduce(jnp.logical_or, [j == v for v in vals])


def _inproj_kernel(x_ref, w_ref, gain_ref, cos_ref, sin_ref, of_ref, ob_ref):
    j = pl.program_id(1)
    kinds = _colblock_kinds()
    id_blocks = [i for i, k in enumerate(kinds) if k == "id"]
    rope_blocks = [i for i, k in enumerate(kinds) if k == "rope"]
    norm_blocks = [i for i, k in enumerate(kinds) if k == "norm"]
    acc = _dot(x_ref[...], w_ref[...])

    def write(y):
        of_ref[...] = y
        ob_ref[...] = y.astype(ob_ref.dtype)

    def normed(rope):
        outs = []
        for s in range(GROUP_COLS // HEAD_DIM):
            h = acc[:, s * HEAD_DIM:(s + 1) * HEAD_DIM]
            g = gain_ref[:, s * HEAD_DIM:(s + 1) * HEAD_DIM]
            y = h * lax.rsqrt(jnp.mean(h * h, axis=-1, keepdims=True) + RMS_EPS) * g
            if rope:
                y = y * cos_ref[...] + pltpu.roll(y, HEAD_DIM // 2, 1) * sin_ref[...]
            outs.append(y)
        return jnp.concatenate(outs, axis=1)

    @pl.when(_any_eq(j, id_blocks))
    def _():
        write(acc)

    @pl.when(_any_eq(j, norm_blocks))
    def _():
        write(normed(False))

    @pl.when(_any_eq(j, rope_blocks))
    def _():
        write(normed(True))


def _inproj(xn, w_main, gain_cols, cos_f, sin_s):
    n, d = xn.shape
    tm = _pick(n, (768, 512, 384, 256, 128, 64, 32, 16, 8))
    tn = GROUP_COLS
    return pl.pallas_call(
        _inproj_kernel,
        grid=(n // tm, MAIN_COLS // tn),
        in_specs=[
            pl.BlockSpec((tm, d), lambda i, j: (i, 0)),
            pl.BlockSpec((d, tn), lambda i, j: (0, j)),
            pl.BlockSpec((1, tn), lambda i, j: (0, j)),
            pl.BlockSpec((tm, HEAD_DIM), lambda i, j: (i, 0)),
            pl.BlockSpec((tm, HEAD_DIM), lambda i, j: (i, 0)),
        ],
        out_specs=[pl.BlockSpec((tm, tn), lambda i, j: (i, j)), pl.BlockSpec((tm, tn), lambda i, j: (i, j))],
        out_shape=[jax.ShapeDtypeStruct((n, MAIN_COLS), jnp.float32),
                   jax.ShapeDtypeStruct((n, MAIN_COLS), jnp.bfloat16)],
        compiler_params=_cparams(("parallel", "arbitrary")),
        name="inproj",
    )(xn, w_main, gain_cols, cos_f, sin_s)


def _small_kernel(x_ref, w_ref, b_ref, o_ref):
    z = _dot(x_ref[...], w_ref[...]) + b_ref[...]
    lane = lax.broadcasted_iota(jnp.int32, z.shape, 1)
    sig = 1.0 / (1.0 + jnp.exp(-z))
    logsig = jnp.minimum(z, 0.0) - jnp.log1p(jnp.exp(-jnp.abs(z)))
    o_ref[...] = jnp.where(lane < N_GATE, sig, logsig)


def _small_proj(xn, w_small, b_small):
    n, d = xn.shape
    tm = _pick(n, (768, 512, 384, 256, 128, 64, 32, 16, 8))
    return pl.pallas_call(
        _small_kernel,
        grid=(n // tm,),
        in_specs=[pl.BlockSpec((tm, d), lambda i: (i, 0)), pl.BlockSpec((d, LANES), lambda i: (0, 0)),
                  pl.BlockSpec((1, LANES), lambda i: (0, 0))],
        out_specs=pl.BlockSpec((tm, LANES), lambda i: (i, 0)),
        out_shape=jax.ShapeDtypeStruct((n, LANES), jnp.float32),
        compiler_params=_cparams(("parallel",)),
        name="small_proj",
    )(xn, w_small, b_small)


def _tri_lower(n):
    r = lax.broadcasted_iota(jnp.int32, (n, n), 0)
    c = lax.broadcasted_iota(jnp.int32, (n, n), 1)
    return (c <= r).astype(jnp.bfloat16)


def _block_cumsum(x, tri):
    hi, mid, lo = _split3(x)
    return _dot(tri, hi) + _dot(tri, mid) + _dot(tri, lo)


def _cumsum_prompt_kernel(x_ref, o_ref, carry_ref):
    @pl.when(pl.program_id(1) == 0)
    def _():
        carry_ref[...] = jnp.zeros_like(carry_ref)

    blk = x_ref.shape[0]
    c = _block_cumsum(x_ref[...], _tri_lower(blk)) + carry_ref[...]
    o_ref[...] = c
    carry_ref[...] = c[blk - 1:blk, :]


def _cumsum_prompt(slab, b, t):
    blk = _pick(t, (512, 256, 128))
    nb = t // blk
    return pl.pallas_call(
        _cumsum_prompt_kernel,
        grid=(b, nb),
        in_specs=[pl.BlockSpec((blk, LANES), lambda i, j: (i * nb + j, 0))],
        out_specs=pl.BlockSpec((blk, LANES), lambda i, j: (i * nb + j, 0)),
        out_shape=jax.ShapeDtypeStruct((b * t, LANES), jnp.float32),
        scratch_shapes=[pltpu.VMEM((1, LANES), jnp.float32)],
        compiler_params=_cparams(("parallel", "arbitrary")),
        name="cumsum_prompt",
    )(slab)


def _cumsum_sample_kernel(pt_ref, pool_ref, new_ref, o_ref, carry_ref, *, n_pages):
    p = pl.program_id(1)

    @pl.when(p == 0)
    def _():
        carry_ref[...] = jnp.zeros_like(carry_ref)

    def run(x):
        blk = x.shape[0]
        c = _block_cumsum(x, _tri_lower(blk)) + carry_ref[...]
        o_ref[0] = c
        carry_ref[...] = c[blk - 1:blk, :]

    @pl.when(p < n_pages)
    def _():
        run(pool_ref[0].astype(jnp.float32))

    @pl.when(p == n_pages)
    def _():
        run(new_ref[0])


def _cumsum_sample(page_table, pool, new_pad):
    bsz, n_pages = page_table.shape
    _, page, h = pool.shape
    kern = functools.partial(_cumsum_sample_kernel, n_pages=n_pages)
    return pl.pallas_call(
        kern,
        grid_spec=pltpu.PrefetchScalarGridSpec(
            num_scalar_prefetch=1,
            grid=(bsz, n_pages + 1),
            in_specs=[
                pl.BlockSpec((1, page, h), lambda b, p, pt: (pt[b, jnp.minimum(p, n_pages - 1)], 0, 0)),
                pl.BlockSpec((1, page, h), lambda b, p, pt: (b, 0, 0)),
            ],
            out_specs=pl.BlockSpec((1, page, h), lambda b, p, pt: (b, p, 0)),
            scratch_shapes=[pltpu.VMEM((1, h), jnp.float32)],
        ),
        out_shape=jax.ShapeDtypeStruct((bsz, (n_pages + 1) * page, h), jnp.float32),
        compiler_params=_cparams(("parallel", "arbitrary")),
        name="cumsum_sample",
    )(page_table, pool, new_pad)


def _pe_term(pe_ref, w_ref):
    acc = jnp.zeros((16, HEAD_DIM), jnp.float32)
    for j in range(CMP_STRIDE):
        lo = jnp.broadcast_to(pe_ref[j:j + 1, :], (16, HEAD_DIM)).astype(jnp.bfloat16)
        hi = jnp.broadcast_to(pe_ref[CMP_STRIDE + j:CMP_STRIDE + j + 1, :], (16, HEAD_DIM)).astype(jnp.bfloat16)
        w = w_ref[j]
        acc = acc + _dot(lo, w[:, :HEAD_DIM]) + _dot(hi, w[:, HEAD_DIM:])
    return acc[0:1]


def _compress_prompt_kernel(x_ref, w_ref, pe_ref, o_ref, *, n_chunks):
    acc = jnp.zeros((n_chunks, 2 * HEAD_DIM), jnp.float32)
    for j in range(CMP_STRIDE):
        xj = x_ref[pl.ds(j, n_chunks, stride=CMP_STRIDE), :].astype(jnp.bfloat16)
        acc = acc + _dot(xj, w_ref[0, j])
    nxt = pltpu.roll(acc[:, HEAD_DIM:], n_chunks - 1, 0)
    o_ref[0, 0, 0] = (acc[:, :HEAD_DIM] + nxt + _pe_term(pe_ref.at[0], w_ref.at[0])).astype(o_ref.dtype)


def _compress_prompt(hf, b, t, col0, w_cat, pe):
    n_chunks = t // CMP_STRIDE
    rows_blk = col0 // HEAD_DIM
    kern = functools.partial(_compress_prompt_kernel, n_chunks=n_chunks)
    return pl.pallas_call(
        kern,
        grid=(b, 2, KV_HEADS),
        in_specs=[
            pl.BlockSpec((t, HEAD_DIM), lambda i, kv, g: (i, rows_blk + kv * KV_HEADS + g)),
            pl.BlockSpec((1, CMP_STRIDE, HEAD_DIM, 2 * HEAD_DIM), lambda i, kv, g: (kv, 0, 0, 0)),
            pl.BlockSpec((1, CMP_BLOCK, HEAD_DIM), lambda i, kv, g: (kv, 0, 0)),
        ],
        out_specs=pl.BlockSpec((1, 1, 1, n_chunks, HEAD_DIM), lambda i, kv, g: (i, kv, g, 0, 0)),
        out_shape=jax.ShapeDtypeStruct((b, 2, KV_HEADS, n_chunks, HEAD_DIM), jnp.bfloat16),
        compiler_params=_cparams(("parallel", "parallel", "parallel")),
        name="compress_prompt",
    )(hf, w_cat, pe)


def _compress_sample_kernel(pt_ref, *refs, n_steps):
    pages = refs[:PAGES_PER_STEP]
    next_ref, new_ref, w_ref, pe_ref, o_ref = refs[PAGES_PER_STEP:]
    s = pl.program_id(1)
    per_row = 2 * KV_HEADS
    page = pages[0].shape[1] // per_row
    cpp = page // CMP_STRIDE
    n_src = PAGES_PER_STEP + 1
    is_last = s == n_steps - 1
    for kv in range(2):
        acc = jnp.zeros((KV_HEADS * n_src * cpp, 2 * HEAD_DIM), jnp.float32)
        for j in range(CMP_STRIDE):
            pieces = []
            for g in range(KV_HEADS):
                rows = pl.ds(j * per_row + kv * KV_HEADS + g, cpp, stride=CMP_STRIDE * per_row)
                for pg in range(PAGES_PER_STEP):
                    pieces.append(pages[pg][0, rows, :])
                pieces.append(jnp.where(is_last, new_ref[0, rows, :], next_ref[0, rows, :]))
            xj = jnp.concatenate(pieces, axis=0).astype(jnp.bfloat16)
            acc = acc + _dot(xj, w_ref[kv, j])
        pe_t = _pe_term(pe_ref.at[kv], w_ref.at[kv])
        per_g = n_src * cpp
        keep = PAGES_PER_STEP * cpp
        for g in range(KV_HEADS):
            a = acc[g * per_g:(g + 1) * per_g]
            nxt = pltpu.roll(a[:, HEAD_DIM:], per_g - 1, 0)
            o_ref[0, kv, g] = (a[:keep, :HEAD_DIM] + nxt[:keep] + pe_t).astype(o_ref.dtype)


def _compress_sample(page_table, pool, new_pad, w_cat, pe):
    bsz, n_pages = page_table.shape
    n_phys, page, cols = pool.shape
    assert n_pages % PAGES_PER_STEP == 0
    n_steps = n_pages // PAGES_PER_STEP
    cps = PAGES_PER_STEP * page // CMP_STRIDE
    page, cols = page * (cols // HEAD_DIM), HEAD_DIM
    pool = pool.reshape(n_phys, page, cols)
    new_pad = new_pad.reshape(bsz, page, cols)

    def page_spec(k):
        return pl.BlockSpec((1, page, cols), lambda b, s, pt: (pt[b, s * PAGES_PER_STEP + k], 0, 0))

    next_spec = pl.BlockSpec(
        (1, page, cols), lambda b, s, pt: (pt[b, jnp.minimum((s + 1) * PAGES_PER_STEP, n_pages - 1)], 0, 0))
    kern = functools.partial(_compress_sample_kernel, n_steps=n_steps)
    return pl.pallas_call(
        kern,
        grid_spec=pltpu.PrefetchScalarGridSpec(
            num_scalar_prefetch=1,
            grid=(bsz, n_steps),
            in_specs=[page_spec(k) for k in range(PAGES_PER_STEP)] + [
                next_spec,
                pl.BlockSpec((1, page, cols), lambda b, s, pt: (b, 0, 0)),
                pl.BlockSpec((2, CMP_STRIDE, HEAD_DIM, 2 * HEAD_DIM), lambda b, s, pt: (0, 0, 0, 0)),
                pl.BlockSpec((2, CMP_BLOCK, HEAD_DIM), lambda b, s, pt: (0, 0, 0)),
            ],
            out_specs=pl.BlockSpec((1, 2, KV_HEADS, cps, HEAD_DIM), lambda b, s, pt: (b, 0, 0, s, 0)),
        ),
        out_shape=jax.ShapeDtypeStruct((bsz, 2, KV_HEADS, n_steps * cps, HEAD_DIM), jnp.bfloat16),
        compiler_params=_cparams(("parallel", "arbitrary")),
        name="compress_sample",
    )(page_table, *([pool] * (PAGES_PER_STEP + 1)), new_pad, w_cat, pe)


def _stack_heads(q):
    return jnp.concatenate([q[:, r * HEAD_DIM:(r + 1) * HEAD_DIM] for r in range(REP)], axis=0).astype(jnp.bfloat16)


def _div_pow2(x, c):
    assert c & (c - 1) == 0
    return lax.shift_right_arithmetic(x, jnp.int32(c.bit_length() - 1))


def _unstack_heads(o, tq):
    return jnp.concatenate([o[r * tq:(r + 1) * tq] for r in range(REP)], axis=1)


def _cmp_select_kernel(q_ref, kc_ref, vc_ref, ov_ref, o_ref, sel_ref, *, tq, pos_base, n_cmp_valid, n_slc):
    qi = pl.program_id(2)
    nc = kc_ref.shape[3]
    nsp = ov_ref.shape[1]
    q = _stack_heads(q_ref[...])
    s = _dot_nt(q, kc_ref[0, 0, 0]) * ATTN_SCALE
    assert tq & (tq - 1) == 0
    t1 = pos_base + qi * tq + (lax.broadcasted_iota(jnp.int32, (REP * tq, nc), 0) & (tq - 1))
    n1 = lax.broadcasted_iota(jnp.int32, (REP * tq, nc), 1)
    vis = (n1 * CMP_STRIDE + CMP_BLOCK - 1 <= t1) & (n1 < n_cmp_valid)
    s = jnp.where(vis, s, -jnp.inf)
    m = jnp.max(s, axis=-1, keepdims=True)
    m = jnp.where(m == -jnp.inf, 0.0, m)
    p = jnp.exp(s - m)
    p = p / jnp.maximum(jnp.sum(p, axis=-1, keepdims=True), jnp.finfo(jnp.float32).tiny)
    o = _dot(p.astype(jnp.bfloat16), vc_ref[0, 0, 0])
    o_ref[...] = _unstack_heads(o, tq)

    psum = p[0:tq]
    for r in range(1, REP):
        psum = psum + p[r * tq:(r + 1) * tq]
    hi, mid, lo = _split3(psum)
    ov = ov_ref[...]
    imp = _dot(hi, ov) + _dot(mid, ov) + _dot(lo, ov)
    t = pos_base + qi * tq + lax.broadcasted_iota(jnp.int32, (tq, nsp), 0)
    jj = lax.broadcasted_iota(jnp.int32, (tq, nsp), 1)
    cur = _div_pow2(t, SLC_BLOCK)
    forced = (jj == 0) | (jj == cur) | (jj == cur - 1)
    causal = (jj * SLC_BLOCK <= t) & (jj < n_slc)
    imp = jnp.where(forced, jnp.inf, imp)
    imp = jnp.where(causal, imp, -jnp.inf)
    rank = jnp.zeros((tq, nsp), jnp.float32)
    for i in range(n_slc):
        col = imp[:, i:i + 1]
        beats = (col > imp) | ((col == imp) & (jj > i))
        rank = rank + beats.astype(jnp.float32)
    sel = (rank < SLC_TOPK) & (imp > -jnp.inf)
    sel_ref[0] = sel.astype(sel_ref.dtype)


def _overlap_matrix(nc, nsp):
    cs = jnp.arange(nc) * CMP_STRIDE
    ss = jnp.arange(nsp) * SLC_BLOCK
    ov = (cs[:, None] < ss[None, :] + SLC_BLOCK) & (cs[:, None] + CMP_BLOCK > ss[None, :])
    return ov.astype(jnp.bfloat16)


def _cmp_select(q_arr, row_blk0, bsz, nq, tq, kc, pos_base, n_cmp_valid, n_slc, nsp, sel_dtype):
    nc = kc.shape[3]
    rows = bsz * nq * tq
    kern = functools.partial(_cmp_select_kernel, tq=tq, pos_base=pos_base, n_cmp_valid=n_cmp_valid, n_slc=n_slc)
    return pl.pallas_call(
        kern,
        grid=(bsz, KV_HEADS, nq),
        in_specs=[
            pl.BlockSpec((tq, GROUP_COLS), lambda b, g, i: (row_blk0 + b * nq + i, g)),
            pl.BlockSpec((1, 1, 1, nc, HEAD_DIM), lambda b, g, i: (b, 0, g, 0, 0)),
            pl.BlockSpec((1, 1, 1, nc, HEAD_DIM), lambda b, g, i: (b, 1, g, 0, 0)),
            pl.BlockSpec((nc, nsp), lambda b, g, i: (0, 0)),
        ],
        out_specs=[
            pl.BlockSpec((tq, GROUP_COLS), lambda b, g, i: (b * nq + i, g)),
            pl.BlockSpec((1, tq, nsp), lambda b, g, i: (g, b * nq + i, 0)),
        ],
        out_shape=[jax.ShapeDtypeStruct((rows, NSA_HEADS * HEAD_DIM), jnp.float32),
                   jax.ShapeDtypeStruct((KV_HEADS, rows, nsp), sel_dtype)],
        compiler_params=_cparams(("parallel", "parallel", "parallel")),
        name="cmp_select",
    )(q_arr, kc, kc, _overlap_matrix(nc, nsp))


def _online_update(s, v, m_ref, l_ref, acc_ref, rows=slice(None)):
    m_prev = m_ref[rows]
    m_new = jnp.maximum(m_prev, jnp.max(s, axis=-1, keepdims=True))
    alpha = jnp.exp(m_prev - m_new)
    p = jnp.exp(s - m_new)
    l_ref[rows] = alpha * l_ref[rows] + jnp.sum(p, axis=-1, keepdims=True)
    acc_ref[rows] = alpha * acc_ref[rows] + _dot(p.astype(jnp.bfloat16), v)
    m_ref[rows] = m_new


def _flash_kernel(*refs, mode, tq, tk, nk):
    if mode == "slc":
        q_ref, k_ref, v_ref, sel_ref, o_ref, m_ref, l_ref, acc_ref = refs
    elif mode == "fox":
        q_ref, k_ref, v_ref, cq_ref, ck_ref, o_ref, m_ref, l_ref, acc_ref = refs
    else:
        q_ref, k_ref, v_ref, o_ref, m_ref, l_ref, acc_ref = refs
    qi = pl.program_id(2)
    kk = pl.program_id(3)
    if mode == "win":
        kidx = qi * (tq // tk) - WINDOW // tk + kk
        active = kidx >= 0
    else:
        kidx = kk
        active = kk <= ((qi + 1) * tq - 1) // tk

    @pl.when(kk == 0)
    def _():
        m_ref[...] = jnp.full_like(m_ref, NEG)
        l_ref[...] = jnp.zeros_like(l_ref)
        acc_ref[...] = jnp.zeros_like(acc_ref)

    @pl.when(active)
    def _():
        q = _stack_heads(q_ref[...])
        s = _dot_nt(q, k_ref[...]) * ATTN_SCALE
        t = qi * tq + lax.broadcasted_iota(jnp.int32, (tq, tk), 0)
        kp = kidx * tk + lax.broadcasted_iota(jnp.int32, (tq, tk), 1)
        mask = kp <= t
        if mode == "win":
            mask = mask & (t - kp < WINDOW)
        if mode == "slc":
            nsb = sel_ref.shape[2]
            jb = lax.broadcasted_iota(jnp.int32, (nsb, tk), 0)
            kb = kidx * (tk // SLC_BLOCK) + _div_pow2(lax.broadcasted_iota(jnp.int32, (nsb, tk), 1), SLC_BLOCK)
            expand = (jb == kb).astype(jnp.bfloat16)
            mask = mask & (_dot(sel_ref[0], expand) > 0.5)
        add = jnp.where(mask, 0.0, NEG)
        if mode == "fox":
            add = jnp.concatenate([(cq_ref[0][:, r:r + 1] - ck_ref[0][r:r + 1, :]) + add for r in range(REP)], axis=0)
        else:
            add = jnp.concatenate([add] * REP, axis=0)
        _online_update(s + add, v_ref[...], m_ref, l_ref, acc_ref)

    @pl.when(kk == nk - 1)
    def _():
        o_ref[...] = _unstack_heads(acc_ref[...] / l_ref[...], tq)


def _flash_prompt(mode, hb, b, t, q_col, k_col, v_col, sel=None, cq=None, ck=None):
    tq = _pick(t, (256, 128))
    tk = _pick(t, (256,) if mode == "win" else (512, 256, 128))
    nq = t // tq
    nkt = t // tk
    nk = (WINDOW // tk + tq // tk) if mode == "win" else nkt
    qb, kb, vb = q_col // GROUP_COLS, k_col // HEAD_DIM, v_col // HEAD_DIM

    def kidx(i, k):
        if mode == "win":
            return jnp.maximum(i * (tq // tk) - WINDOW // tk + k, 0)
        return jnp.minimum(k, ((i + 1) * tq - 1) // tk)

    in_specs = [
        pl.BlockSpec((tq, GROUP_COLS), lambda bi, g, i, k: (bi * nq + i, qb + g)),
        pl.BlockSpec((tk, HEAD_DIM), lambda bi, g, i, k: (bi * nkt + kidx(i, k), kb + g)),
        pl.BlockSpec((tk, HEAD_DIM), lambda bi, g, i, k: (bi * nkt + kidx(i, k), vb + g)),
    ]
    args = [hb, hb, hb]
    if mode == "slc":
        in_specs.append(pl.BlockSpec((1, tq, sel.shape[2]), lambda bi, g, i, k: (g, bi * nq + i, 0)))
        args.append(sel)
    if mode == "fox":
        in_specs.append(pl.BlockSpec((1, tq, REP), lambda bi, g, i, k: (g, bi * nq + i, 0)))
        in_specs.append(pl.BlockSpec((1, REP, tk), lambda bi, g, i, k: (g, 0, bi * nkt + kidx(i, k))))
        args += [cq, ck]
    kern = functools.partial(_flash_kernel, mode=mode, tq=tq, tk=tk, nk=nk)
    return pl.pallas_call(
        kern,
        grid=(b, KV_HEADS, nq, nk),
        in_specs=in_specs,
        out_specs=pl.BlockSpec((tq, GROUP_COLS), lambda bi, g, i, k: (bi * nq + i, g)),
        out_shape=jax.ShapeDtypeStruct((b * t, KV_HEADS * GROUP_COLS), jnp.float32),
        scratch_shapes=[pltpu.VMEM((REP * tq, 1), jnp.float32), pltpu.VMEM((REP * tq, 1), jnp.float32),
                        pltpu.VMEM((REP * tq, HEAD_DIM), jnp.float32)],
        compiler_params=_cparams(("parallel", "parallel", "parallel", "arbitrary")),
        name="flash_" + mode,
    )(*args)


def _rows_of(g, tn):
    return slice(g * REP * tn, (g + 1) * REP * tn)


def _sample_init(m_ref, l_ref, acc_ref):
    m_ref[...] = jnp.full_like(m_ref, NEG)
    l_ref[...] = jnp.zeros_like(l_ref)
    acc_ref[...] = jnp.zeros_like(acc_ref)


def _sample_finish(o_ref, l_ref, acc_ref, tn):
    res = acc_ref[...] / l_ref[...]
    for g in range(KV_HEADS):
        o_ref[:, g * GROUP_COLS:(g + 1) * GROUP_COLS] = _unstack_heads(res[_rows_of(g, tn)], tn)


def _kv_of(page_vals, g):
    k = jnp.concatenate([p[:, g * HEAD_DIM:(g + 1) * HEAD_DIM] for p in page_vals], axis=0)
    off = KV_HEADS * HEAD_DIM
    v = jnp.concatenate([p[:, off + g * HEAD_DIM:off + (g + 1) * HEAD_DIM] for p in page_vals], axis=0)
    return k.astype(jnp.bfloat16), v.astype(jnp.bfloat16)


def _new_rows_mask(tn, page):
    tok = lax.broadcasted_iota(jnp.int32, (tn, page), 0)
    r = lax.broadcasted_iota(jnp.int32, (tn, page), 1)
    return r <= tok


def _paged_attn_kernel(pt_ref, *refs, mode, n_steps, tn):
    pages = refs[:PAGES_PER_STEP]
    if mode == "slc":
        q_ref, new_ref, sel_ref, selnew_ref, o_ref, m_ref, l_ref, acc_ref = refs[PAGES_PER_STEP:]
    else:
        q_ref, new_ref, cq_ref, ck_ref, cknew_ref, o_ref, m_ref, l_ref, acc_ref = refs[PAGES_PER_STEP:]
    s_id = pl.program_id(1)
    page = pages[0].shape[1]
    nkeys = PAGES_PER_STEP * page

    @pl.when(s_id == 0)
    def _():
        _sample_init(m_ref, l_ref, acc_ref)

    def bias_rows(g, ck, add):
        return jnp.concatenate(
            [(cq_ref[0][:, g * REP + r:g * REP + r + 1] - ck[g * REP + r:g * REP + r + 1, :]) + add
             for r in range(REP)], axis=0)

    page_vals = [p[0] for p in pages]
    if mode == "slc":
        key_blk = _div_pow2(lax.broadcasted_iota(jnp.int32, (tn, nkeys), 1), SLC_BLOCK)
    for g in range(KV_HEADS):
        q = _stack_heads(q_ref[:, g * GROUP_COLS:(g + 1) * GROUP_COLS])
        k, v = _kv_of(page_vals, g)
        s = _dot_nt(q, k) * ATTN_SCALE
        if mode == "slc":
            flags = sel_ref[0, 0, g]
            picked = jnp.zeros((tn, nkeys), jnp.float32)
            for jb in range(flags.shape[1]):
                picked = jnp.where(key_blk == jb, flags[:, jb:jb + 1], picked)
            add = jnp.where(picked > 0.5, 0.0, NEG)
            add = jnp.concatenate([add] * REP, axis=0)
        else:
            add = bias_rows(g, ck_ref[0], 0.0)
        _online_update(s + add, v, m_ref, l_ref, acc_ref, _rows_of(g, tn))

    @pl.when(s_id == n_steps - 1)
    def _():
        causal = _new_rows_mask(tn, page)
        for g in range(KV_HEADS):
            q = _stack_heads(q_ref[:, g * GROUP_COLS:(g + 1) * GROUP_COLS])
            k, v = _kv_of([new_ref[0]], g)
            s = _dot_nt(q, k) * ATTN_SCALE
            if mode == "slc":
                add = jnp.where(causal & (selnew_ref[0, 0, g][:, 0:1] > 0.5), 0.0, NEG)
                add = jnp.concatenate([add] * REP, axis=0)
            else:
                add = bias_rows(g, cknew_ref[0], jnp.where(causal, 0.0, NEG))
            _online_update(s + add, v, m_ref, l_ref, acc_ref, _rows_of(g, tn))
        _sample_finish(o_ref, l_ref, acc_ref, tn)


def _paged_attn(mode, page_table, pool, new_pad, q_s, tn, sel5=None, cq=None, ck=None):
    bsz, n_pages = page_table.shape
    _, page, cols = pool.shape
    n_steps = n_pages // PAGES_PER_STEP
    nkeys = PAGES_PER_STEP * page

    def page_spec(k):
        return pl.BlockSpec((1, page, cols), lambda b, s, pt: (pt[b, s * PAGES_PER_STEP + k], 0, 0))

    in_specs = [page_spec(k) for k in range(PAGES_PER_STEP)] + [
        pl.BlockSpec((tn, NSA_HEADS * HEAD_DIM), lambda b, s, pt: (b, 0)),
        pl.BlockSpec((1, page, cols), lambda b, s, pt: (b, 0, 0)),
    ]
    args = [pool] * PAGES_PER_STEP + [q_s, new_pad]
    if mode == "slc":
        nb = sel5.shape[4]
        in_specs += [pl.BlockSpec((1, 1, KV_HEADS, tn, nb), lambda b, s, pt: (b, s, 0, 0, 0)),
                     pl.BlockSpec((1, 1, KV_HEADS, tn, nb), lambda b, s, pt: (b, n_steps, 0, 0, 0))]
        args += [sel5, sel5]
    else:
        nh = cq.shape[2]
        in_specs += [pl.BlockSpec((1, tn, nh), lambda b, s, pt: (b, 0, 0)),
                     pl.BlockSpec((1, nh, nkeys), lambda b, s, pt: (b, 0, s)),
                     pl.BlockSpec((1, nh, page), lambda b, s, pt: (b, 0, n_pages))]
        args += [cq, ck, ck]
    rows = KV_HEADS * REP * tn
    kern = functools.partial(_paged_attn_kernel, mode=mode, n_steps=n_steps, tn=tn)
    return pl.pallas_call(
        kern,
        grid_spec=pltpu.PrefetchScalarGridSpec(
            num_scalar_prefetch=1,
            grid=(bsz, n_steps),
            in_specs=in_specs,
            out_specs=pl.BlockSpec((tn, NSA_HEADS * HEAD_DIM), lambda b, s, pt: (b, 0)),
            scratch_shapes=[pltpu.VMEM((rows, 1), jnp.float32), pltpu.VMEM((rows, 1), jnp.float32),
                            pltpu.VMEM((rows, HEAD_DIM), jnp.float32)],
        ),
        out_shape=jax.ShapeDtypeStruct((bsz * tn, NSA_HEADS * HEAD_DIM), jnp.float32),
        compiler_params=_cparams(("parallel", "arbitrary")),
        name="paged_" + mode,
    )(page_table, *args)


def _win_sample_kernel(q_ref, win_ref, new_ref, o_ref, m_ref, l_ref, acc_ref, *, tn, past):
    _sample_init(m_ref, l_ref, acc_ref)
    win_len = win_ref.shape[1]
    page = new_ref.shape[1]
    pos = past + lax.broadcasted_iota(jnp.int32, (tn, win_len), 0)
    kpos = past - win_len + lax.broadcasted_iota(jnp.int32, (tn, win_len), 1)
    wmask = (kpos <= pos) & (pos - kpos < WINDOW)
    add_w = jnp.concatenate([jnp.where(wmask, 0.0, NEG)] * REP, axis=0)
    add_n = jnp.concatenate([jnp.where(_new_rows_mask(tn, page), 0.0, NEG)] * REP, axis=0)
    for g in range(KV_HEADS):
        q = _stack_heads(q_ref[:, g * GROUP_COLS:(g + 1) * GROUP_COLS])
        k, v = _kv_of([new_ref[0]], g)
        _online_update(_dot_nt(q, k) * ATTN_SCALE + add_n, v, m_ref, l_ref, acc_ref, _rows_of(g, tn))
        k, v = _kv_of([win_ref[0]], g)
        _online_update(_dot_nt(q, k) * ATTN_SCALE + add_w, v, m_ref, l_ref, acc_ref, _rows_of(g, tn))
    _sample_finish(o_ref, l_ref, acc_ref, tn)


def _win_sample(win_buf, new_pad, q_s, tn, past):
    bsz, win_len, cols = win_buf.shape
    page = new_pad.shape[1]
    rows = KV_HEADS * REP * tn
    kern = functools.partial(_win_sample_kernel, tn=tn, past=past)
    return pl.pallas_call(
        kern,
        grid=(bsz,),
        in_specs=[pl.BlockSpec((tn, NSA_HEADS * HEAD_DIM), lambda b: (b, 0)),
                  pl.BlockSpec((1, win_len, cols), lambda b: (b, 0, 0)),
                  pl.BlockSpec((1, page, cols), lambda b: (b, 0, 0))],
        out_specs=pl.BlockSpec((tn, NSA_HEADS * HEAD_DIM), lambda b: (b, 0)),
        out_shape=jax.ShapeDtypeStruct((bsz * tn, NSA_HEADS * HEAD_DIM), jnp.float32),
        scratch_shapes=[pltpu.VMEM((rows, 1), jnp.float32), pltpu.VMEM((rows, 1), jnp.float32),
                        pltpu.VMEM((rows, HEAD_DIM), jnp.float32)],
        compiler_params=_cparams(("parallel",)),
        name="win_sample",
    )(q_s, win_buf, new_pad)


def _merge_kernel(cmp_ref, slc_ref, win_ref, fox_ref, gate_ref, gn_ref, gf_ref, o_ref):
    gates = gate_ref[...]
    parts = []
    for h in range(NSA_HEADS):
        cols = slice(h * HEAD_DIM, (h + 1) * HEAD_DIM)
        c = N_BRANCH * h
        parts.append(gates[:, c:c + 1] * cmp_ref[:, cols] + gates[:, c + 1:c + 2] * slc_ref[:, cols]
                     + gates[:, c + 2:c + 3] * win_ref[:, cols])
    nsa = jnp.concatenate(parts, axis=1)
    nsa = nsa * lax.rsqrt(jnp.mean(nsa * nsa, axis=-1, keepdims=True) + RMS_EPS) * gn_ref[...]
    fox = fox_ref[...]
    fox = fox * lax.rsqrt(jnp.mean(fox * fox, axis=-1, keepdims=True) + RMS_EPS) * gf_ref[...]
    o_ref[...] = jnp.concatenate([nsa, fox], axis=1).astype(o_ref.dtype)


def _merge(o_cmp, o_slc, o_win, o_fox, slab, g_nsa, g_fox):
    n, w = o_cmp.shape
    tm = _pick(n, (256, 128, 64, 32, 16, 8))
    row = lambda i: (i, 0)
    fix = lambda i: (0, 0)
    return pl.pallas_call(
        _merge_kernel,
        grid=(n // tm,),
        in_specs=[pl.BlockSpec((tm, w), row)] * 4 + [pl.BlockSpec((tm, LANES), row), pl.BlockSpec((1, w), fix),
                                                      pl.BlockSpec((1, w), fix)],
        out_specs=pl.BlockSpec((tm, 2 * w), row),
        out_shape=jax.ShapeDtypeStruct((n, 2 * w), jnp.bfloat16),
        compiler_params=_cparams(("parallel",)),
        name="merge_heads",
    )(o_cmp, o_slc, o_win, o_fox, slab, g_nsa.reshape(1, w), g_fox.reshape(1, w))


def _outproj_kernel(a_ref, w_ref, x_ref, o_ref):
    o_ref[...] = x_ref[...] + _dot(a_ref[...], w_ref[...])


def _outproj(a, w, x):
    n, k = a.shape
    d = w.shape[1]
    tm = _pick(n, (768, 512, 384, 256, 128, 64, 32, 16, 8))
    tn = 512
    return pl.pallas_call(
        _outproj_kernel,
        grid=(n // tm, d // tn),
        in_specs=[pl.BlockSpec((tm, k), lambda i, j: (i, 0)), pl.BlockSpec((k, tn), lambda i, j: (0, j)),
                  pl.BlockSpec((tm, tn), lambda i, j: (i, j))],
        out_specs=pl.BlockSpec((tm, tn), lambda i, j: (i, j)),
        out_shape=jax.ShapeDtypeStruct((n, d), jnp.float32),
        compiler_params=_cparams(("parallel", "arbitrary")),
        name="outproj",
    )(a, w, x)


def _router_kernel(h_ref, g_ref, w_ref, b_ref, xn_ref, eid_ref, gate_ref):
    x = h_ref[...]
    xn = x * lax.rsqrt(jnp.mean(x * x, axis=-1, keepdims=True) + RMS_EPS) * g_ref[...]
    xn_ref[...] = xn.astype(xn_ref.dtype)
    lg = jnp.dot(xn, w_ref[...], precision=lax.Precision.HIGHEST, preferred_element_type=jnp.float32) + b_ref[...]
    lane = lax.broadcasted_iota(jnp.int32, lg.shape, 1)
    lane_f = lane.astype(jnp.float32)
    ninf = -jnp.inf
    is_grp = lane < N_GROUPS
    gl = jnp.where(is_grp, lg, ninf)
    gmax = jnp.max(gl, axis=-1, keepdims=True)
    gsel = jnp.min(jnp.where(gl == gmax, lane_f, float(LANES)), axis=-1, keepdims=True)
    p_sel = 1.0 / jnp.sum(jnp.where(is_grp, jnp.exp(lg - gmax), 0.0), axis=-1, keepdims=True)
    e_lane = lane - N_GROUPS
    e_grp = _div_pow2(e_lane, EXPERTS_PER_GROUP).astype(jnp.float32)
    in_grp = (e_lane >= 0) & (e_lane < N_EXPERTS) & (e_grp == gsel)
    el = jnp.where(in_grp, lg, ninf)
    v1 = jnp.max(el, axis=-1, keepdims=True)
    i1 = jnp.min(jnp.where(el == v1, lane_f, float(LANES)), axis=-1, keepdims=True)
    el2 = jnp.where(lane_f == i1, ninf, el)
    v2 = jnp.max(el2, axis=-1, keepdims=True)
    i2 = jnp.min(jnp.where(el2 == v2, lane_f, float(LANES)), axis=-1, keepdims=True)
    e2 = jnp.exp(v2 - v1)
    den = 1.0 + e2
    g1 = p_sel * (1.0 / den)
    g2 = p_sel * (e2 / den)
    eid = jnp.where(lane == 0, i1 - N_GROUPS, jnp.where(lane == 1, i2 - N_GROUPS, 0.0))
    eid_ref[...] = eid.astype(jnp.int32)
    gate_ref[...] = jnp.where(lane == 0, g1, jnp.where(lane == 1, g2, 0.0))


def _router(h1, g_ffn, w_r, b_r):
    n, d = h1.shape
    tm = _pick(n, (256, 128, 64, 32, 16, 8))
    row = lambda i: (i, 0)
    fix = lambda i: (0, 0)
    return pl.pallas_call(
        _router_kernel,
        grid=(n // tm,),
        in_specs=[pl.BlockSpec((tm, d), row), pl.BlockSpec((1, d), fix), pl.BlockSpec((d, LANES), fix),
                  pl.BlockSpec((1, LANES), fix)],
        out_specs=[pl.BlockSpec((tm, d), row), pl.BlockSpec((tm, LANES), row), pl.BlockSpec((tm, LANES), row)],
        out_shape=[jax.ShapeDtypeStruct((n, d), jnp.bfloat16), jax.ShapeDtypeStruct((n, LANES), jnp.int32),
                   jax.ShapeDtypeStruct((n, LANES), jnp.float32)],
        compiler_params=_cparams(("parallel",)),
        name="router",
    )(h1, g_ffn.reshape(1, d), w_r, b_r)


GATHER_WINDOW = 32


def _gather_rows_kernel(idx_ref, src_ref, out_ref, sem):
    n = out_ref.shape[0]

    def copy(i, src_row):
        return pltpu.make_async_copy(src_ref.at[src_row], out_ref.at[i], sem)

    def issue(i, c):
        @pl.when(i >= GATHER_WINDOW)
        def _():
            copy(i - GATHER_WINDOW, 0).wait()
        copy(i, idx_ref[i]).start()
        return c

    lax.fori_loop(0, n, issue, 0)

    def drain(i, c):
        copy(i, 0).wait()
        return c

    lax.fori_loop(max(n - GATHER_WINDOW, 0), n, drain, 0)


def _gather_rows(idx, src):
    n = idx.shape[0]
    d = src.shape[1]
    sub = 8 * (4 // src.dtype.itemsize)
    src3 = src.reshape(src.shape[0], sub, d // sub)
    return _gather_slabs(idx, src3).reshape(n, d)


def _gather_slabs(idx, src):
    n = idx.shape[0]
    return pl.pallas_call(
        _gather_rows_kernel,
        grid_spec=pltpu.PrefetchScalarGridSpec(
            num_scalar_prefetch=1,
            grid=(1,),
            in_specs=[pl.BlockSpec(memory_space=pl.ANY)],
            out_specs=pl.BlockSpec(memory_space=pl.ANY),
            scratch_shapes=[pltpu.SemaphoreType.DMA(())],
        ),
        out_shape=jax.ShapeDtypeStruct((n,) + src.shape[1:], src.dtype),
        compiler_params=pltpu.CompilerParams(dimension_semantics=("arbitrary",)),
        name="gather_rows",
    )(idx, src)


MOE_TM = 256
MOE_FT = 256
MOE_NT = 1024


def _gateup_kernel(tb_ref, e_ref, f_ref, first_ref, valid_ref, x_ref, wg_ref, wu_ref, h_ref, wgb_ref, wub_ref):
    s = pl.program_id(0)

    @pl.when(first_ref[s] == 1)
    def _():
        wgb_ref[...] = wg_ref[0].astype(jnp.bfloat16)
        wub_ref[...] = wu_ref[0].astype(jnp.bfloat16)

    @pl.when(valid_ref[s] == 1)
    def _():
        x = x_ref[...]
        a = _dot(x, wgb_ref[...])
        u = _dot(x, wub_ref[...])
        h_ref[...] = (a * (1.0 / (1.0 + jnp.exp(-a))) * u).astype(h_ref.dtype)


def _down_kernel(tb_ref, e_ref, f_ref, first_ref, valid_ref, h_ref, wd_ref, y_ref, wdb_ref):
    s = pl.program_id(0)

    @pl.when(first_ref[s] == 1)
    def _():
        wdb_ref[...] = wd_ref[0].astype(jnp.bfloat16)

    @pl.when(valid_ref[s] == 1)
    def _():
        y_ref[...] = _dot(h_ref[...], wdb_ref[...])


def _work_list(block_e, start_blk, n_blk, total_blocks, n_tiles, n_tb_max):
    n_steps = n_tb_max * n_tiles
    s = jnp.minimum(jnp.arange(n_steps, dtype=jnp.int32), total_blocks * n_tiles - 1)
    valid = (jnp.arange(n_steps, dtype=jnp.int32) < total_blocks * n_tiles).astype(jnp.int32)
    e = block_e[s // n_tiles]
    local = s - n_tiles * start_blk[e]
    nb = jnp.maximum(n_blk[e], 1)
    tile = local // nb
    within = local % nb
    tb = start_blk[e] + within
    first = ((within == 0) & (valid == 1)).astype(jnp.int32)
    return tb.astype(jnp.int32), e.astype(jnp.int32), tile.astype(jnp.int32), first, valid


def _experts(xs, w_gate, w_up, w_down, block_e, start_blk, n_blk, total_blocks):
    a_pad, d = xs.shape
    n_tb_max = a_pad // MOE_TM
    d_exp = w_gate.shape[2]
    wl1 = _work_list(block_e, start_blk, n_blk, total_blocks, d_exp // MOE_FT, n_tb_max)
    hidden = pl.pallas_call(
        _gateup_kernel,
        grid_spec=pltpu.PrefetchScalarGridSpec(
            num_scalar_prefetch=5,
            grid=(wl1[0].shape[0],),
            in_specs=[
                pl.BlockSpec((MOE_TM, d), lambda s, tb, e, f, fi, va: (tb[s], 0)),
                pl.BlockSpec((1, d, MOE_FT), lambda s, tb, e, f, fi, va: (e[s], 0, f[s])),
                pl.BlockSpec((1, d, MOE_FT), lambda s, tb, e, f, fi, va: (e[s], 0, f[s])),
            ],
            out_specs=pl.BlockSpec((MOE_TM, MOE_FT), lambda s, tb, e, f, fi, va: (tb[s], f[s])),
            scratch_shapes=[pltpu.VMEM((d, MOE_FT), jnp.bfloat16), pltpu.VMEM((d, MOE_FT), jnp.bfloat16)],
        ),
        out_shape=jax.ShapeDtypeStruct((a_pad, d_exp), jnp.bfloat16),
        compiler_params=_cparams(("arbitrary",)),
        name="expert_gate_up",
    )(*wl1, xs, w_gate, w_up)
    wl2 = _work_list(block_e, start_blk, n_blk, total_blocks, d // MOE_NT, n_tb_max)
    return pl.pallas_call(
        _down_kernel,
        grid_spec=pltpu.PrefetchScalarGridSpec(
            num_scalar_prefetch=5,
            grid=(wl2[0].shape[0],),
            in_specs=[
                pl.BlockSpec((MOE_TM, d_exp), lambda s, tb, e, f, fi, va: (tb[s], 0)),
                pl.BlockSpec((1, d_exp, MOE_NT), lambda s, tb, e, f, fi, va: (e[s], 0, f[s])),
            ],
            out_specs=pl.BlockSpec((MOE_TM, MOE_NT), lambda s, tb, e, f, fi, va: (tb[s], f[s])),
            scratch_shapes=[pltpu.VMEM((d_exp, MOE_NT), jnp.bfloat16)],
        ),
        out_shape=jax.ShapeDtypeStruct((a_pad, d), jnp.float32),
        compiler_params=_cparams(("arbitrary",)),
        name="expert_down",
    )(*wl2, hidden, w_down)


def _combine_kernel(h_ref, y0_ref, y1_ref, gate_ref, o_ref):
    g = gate_ref[...]
    o_ref[...] = h_ref[...] + (g[:, 0:1] * y0_ref[0] + g[:, 1:2] * y1_ref[0])


def _combine(h1, y2, gate_slab):
    n, d = h1.shape
    tm = _pick(n, (256, 128, 64, 32, 16, 8))
    return pl.pallas_call(
        _combine_kernel,
        grid=(n // tm,),
        in_specs=[pl.BlockSpec((tm, d), lambda i: (i, 0)), pl.BlockSpec((1, tm, d), lambda i: (0, i, 0)),
                  pl.BlockSpec((1, tm, d), lambda i: (1, i, 0)), pl.BlockSpec((tm, LANES), lambda i: (i, 0))],
        out_specs=pl.BlockSpec((tm, d), lambda i: (i, 0)),
        out_shape=jax.ShapeDtypeStruct((n, d), jnp.float32),
        compiler_params=_cparams(("parallel",)),
        name="moe_combine",
    )(h1, y2, y2, gate_slab)


def _moe(h1, g_ffn, w_rg, b_rg, w_re, b_re, w_gate, w_up, w_down):
    n, d = h1.shape
    w_r = jnp.zeros((d, LANES), jnp.float32).at[:, :N_GROUPS].set(w_rg).at[:, N_GROUPS:N_GROUPS + N_EXPERTS].set(w_re)
    b_r = jnp.zeros((1, LANES), jnp.float32).at[0, :N_GROUPS].set(b_rg).at[0, N_GROUPS:N_GROUPS + N_EXPERTS].set(b_re)
    xn, eid_slab, gate_slab = _router(h1, g_ffn, w_r, b_r)
    flat = eid_slab[:, :EXPERT_TOPK].reshape(n * EXPERT_TOPK)
    onehot = (flat[:, None] == jnp.arange(N_EXPERTS, dtype=jnp.int32)[None, :]).astype(jnp.int32)
    rank = jnp.take_along_axis(jnp.cumsum(onehot, axis=0) - onehot, flat[:, None], axis=1)[:, 0]
    counts = jnp.sum(onehot, axis=0)
    n_blk = (counts + MOE_TM - 1) // MOE_TM
    start_blk = jnp.cumsum(n_blk) - n_blk
    total_blocks = jnp.sum(n_blk)
    slot = start_blk[flat] * MOE_TM + rank
    n_tb_max = -(-(n * EXPERT_TOPK) // MOE_TM) + N_EXPERTS
    a_pad = n_tb_max * MOE_TM
    src_tok = jnp.zeros((a_pad,), jnp.int32).at[slot].set(jnp.arange(n * EXPERT_TOPK, dtype=jnp.int32) // EXPERT_TOPK)
    end_blk = jnp.cumsum(n_blk)
    block_e = jnp.minimum(jnp.searchsorted(end_blk, jnp.arange(n_tb_max, dtype=jnp.int32), side="right"),
                          N_EXPERTS - 1).astype(jnp.int32)
    xs = _gather_rows(src_tok, xn)
    yb = _experts(xs, w_gate, w_up, w_down, block_e, start_blk.astype(jnp.int32), n_blk.astype(jnp.int32),
                  total_blocks.astype(jnp.int32))
    back = slot.reshape(n, EXPERT_TOPK).T.reshape(n * EXPERT_TOPK).astype(jnp.int32)
    y2 = _gather_rows(back, yb).reshape(EXPERT_TOPK, n, d)
    return _combine(h1, y2, gate_slab)


def _rope_tables(pos):
    half = HEAD_DIM // 2
    inv = jnp.exp(-math.log(ROPE_THETA) * jnp.arange(half, dtype=jnp.float32) * (2.0 / HEAD_DIM))
    ang = pos.astype(jnp.float32)[:, None] * inv[None, :]
    cos, sin = jnp.cos(ang), jnp.sin(ang)
    return jnp.concatenate([cos, cos], axis=1), jnp.concatenate([-sin, sin], axis=1)


def _repack_weights(w_in, b_gate, b_forget, g_nsa_q, g_nsa_k, g_fox_q, g_fox_k):
    o1 = NSA_HEADS * HEAD_DIM
    o2 = o1 + N_BRANCH * KV_COLS
    o3 = o2 + N_GATE
    o4 = o3 + FOX_HEADS * HEAD_DIM
    o5 = o4 + KV_COLS
    d = w_in.shape[0]
    w_main = jnp.concatenate([w_in[:, :o2], w_in[:, o3:o5]], axis=1).astype(jnp.bfloat16)
    n_f = w_in.shape[1] - o5
    w_small = jnp.zeros((d, LANES), jnp.float32).at[:, :N_GATE].set(w_in[:, o2:o3])
    w_small = w_small.at[:, COL_LOGF:COL_LOGF + n_f].set(w_in[:, o5:]).astype(jnp.bfloat16)
    b_small = jnp.zeros((1, LANES), jnp.float32).at[0, :N_GATE].set(b_gate.reshape(N_GATE))
    b_small = b_small.at[0, COL_LOGF:COL_LOGF + n_f].set(b_forget)
    ones = jnp.ones((KV_HEADS * HEAD_DIM,), jnp.float32)
    gains = [jnp.tile(g_nsa_q, NSA_HEADS)]
    for br in range(N_BRANCH):
        gains += [jnp.tile(g_nsa_k[br], KV_HEADS), ones]
    gains += [jnp.tile(g_fox_q, FOX_HEADS), jnp.tile(g_fox_k, KV_HEADS), ones]
    return w_main, w_small, b_small, jnp.concatenate(gains).reshape(1, MAIN_COLS)


def _cmp_weights(w_k, w_v, pe_k, pe_v):
    def cat(w):
        return jnp.concatenate([w[:CMP_STRIDE], w[CMP_STRIDE:]], axis=2)
    return jnp.stack([cat(w_k), cat(w_v)]).astype(jnp.bfloat16), jnp.stack([pe_k, pe_v])


def _pad_rows(x, rows):
    return jnp.pad(x, ((0, 0), (0, rows - x.shape[1]), (0, 0)))


def kernel(x_prompt, x_sample, cache_nsa_cmp_kv, cache_nsa_slc_kv, cache_nsa_win_kv, cache_fox_kv, cache_fox_logf, page_table, g_attn_norm, w_in, b_nsa_gate, b_fox_forget, g_nsa_q, g_nsa_k, g_fox_q, g_fox_k, w_cmp_k, w_cmp_v, pe_cmp_k, pe_cmp_v, g_out_nsa, g_out_fox, w_out, g_ffn_norm, w_router_grp, b_router_grp, w_router_exp, b_router_exp, w_exp_gate, w_exp_up, w_exp_down):
    depth = w_in.shape[0]
    assert depth == 1, "single-layer step"
    bp, t, d = x_prompt.shape
    bs, tn, _ = x_sample.shape
    n_pages = page_table.shape[1]
    page = cache_fox_kv.shape[2]
    past = n_pages * page
    n_p, n_s = bp * t, bs * tn
    l = 0

    x_all = jnp.concatenate([x_prompt.reshape(n_p, d), x_sample.reshape(n_s, d)], axis=0)
    pos_all = jnp.concatenate([jnp.tile(jnp.arange(t, dtype=jnp.int32), bp),
                               jnp.tile(past + jnp.arange(tn, dtype=jnp.int32), bs)])
    cos_f, sin_s = _rope_tables(pos_all)
    w_main, w_small, b_small, gain_cols = _repack_weights(
        w_in[l], b_nsa_gate[l], b_fox_forget[l], g_nsa_q[l], g_nsa_k[l], g_fox_q[l], g_fox_k[l])

    xn = _rmsnorm(x_all, g_attn_norm[l], jnp.bfloat16)
    hf, hb = _inproj(xn, w_main, gain_cols, cos_f, sin_s)
    slab = _small_proj(xn, w_small, b_small)

    col_cmp, col_slc, col_win = (COL_KV_NSA + br * KV_COLS for br in range(N_BRANCH))
    w_cat, pe_cat = _cmp_weights(w_cmp_k[l], w_cmp_v[l], pe_cmp_k[l], pe_cmp_v[l])
    v_off = KV_HEADS * HEAD_DIM

    tq = _pick(t, (256, 128))
    kc_p = _compress_prompt(hf, bp, t, col_cmp, w_cat, pe_cat)
    n_slc_p = max(-(-t // SLC_BLOCK), SLC_TOPK)
    nsp_p = -(-n_slc_p // LANES) * LANES if n_slc_p > 64 else 64
    o_cmp_p, sel_p = _cmp_select(hb, 0, bp, t // tq, tq, kc_p, 0, t // CMP_STRIDE - CMP_RATIO + 1, n_slc_p, nsp_p,
                                 jnp.bfloat16)
    o_slc_p = _flash_prompt("slc", hb, bp, t, COL_Q_NSA, col_slc, col_slc + v_off, sel=sel_p)
    o_win_p = _flash_prompt("win", hb, bp, t, COL_Q_NSA, col_win, col_win + v_off)
    c_p = _cumsum_prompt(slab[:n_p], bp, t)[:, COL_LOGF:COL_LOGF + FOX_HEADS]
    cq_p = c_p.reshape(n_p, KV_HEADS, REP).transpose(1, 0, 2)
    ck_p = c_p.T.reshape(KV_HEADS, REP, n_p)
    o_fox_p = _flash_prompt("fox", hb, bp, t, COL_Q_FOX, COL_KV_FOX, COL_KV_FOX + v_off, cq=cq_p, ck=ck_p)
    a_p = _merge(o_cmp_p, o_slc_p, o_win_p, o_fox_p, slab[:n_p], g_out_nsa[l], g_out_fox[l])

    hf_s, slab_s = hf[n_p:], slab[n_p:]
    q_nsa_s = hf_s[:, COL_Q_NSA:COL_Q_NSA + NSA_HEADS * HEAD_DIM]
    q_fox_s = hf_s[:, COL_Q_FOX:COL_Q_FOX + FOX_HEADS * HEAD_DIM]

    def new_rows(col):
        return _pad_rows(hf_s[:, col:col + KV_COLS].reshape(bs, tn, KV_COLS), page)

    pool = lambda c: c[l].reshape(c.shape[1], page, KV_COLS)
    kc_s = _compress_sample(page_table, pool(cache_nsa_cmp_kv), new_rows(col_cmp), w_cat, pe_cat)
    t_ctx = past + tn
    n_cmp_s = -(-t_ctx // CMP_STRIDE) - CMP_RATIO + 1
    n_slc_s = max(-(-t_ctx // SLC_BLOCK), SLC_TOPK)
    blocks_per_step = PAGES_PER_STEP * page // SLC_BLOCK
    nsp_s = -(-(n_slc_s + 1) // LANES) * LANES
    nsp_s = -(-nsp_s // blocks_per_step) * blocks_per_step
    o_cmp_s, sel_s = _cmp_select(q_nsa_s, 0, bs, 1, tn, kc_s, past, n_cmp_s, n_slc_s, nsp_s, jnp.float32)
    sel5 = sel_s.reshape(KV_HEADS, bs, tn, nsp_s // blocks_per_step, blocks_per_step).transpose(1, 3, 0, 2, 4)
    o_slc_s = _paged_attn("slc", page_table, pool(cache_nsa_slc_kv), new_rows(col_slc), q_nsa_s, tn, sel5=sel5)
    win_buf = cache_nsa_win_kv[l].reshape(bs, -1, KV_COLS)
    o_win_s = _win_sample(win_buf, new_rows(col_win), q_nsa_s, tn, past)
    logf_s = slab_s[:, COL_LOGF:COL_LOGF + FOX_HEADS].reshape(bs, tn, FOX_HEADS)
    c_s = _cumsum_sample(page_table, cache_fox_logf[l], _pad_rows(logf_s, page))
    cq_s = c_s[:, past:past + tn]
    ck_s = c_s.transpose(0, 2, 1)
    o_fox_s = _paged_attn("fox", page_table, pool(cache_fox_kv), new_rows(COL_KV_FOX), q_fox_s, tn, cq=cq_s, ck=ck_s)
    a_s = _merge(o_cmp_s, o_slc_s, o_win_s, o_fox_s, slab_s, g_out_nsa[l], g_out_fox[l])

    h1 = _outproj(jnp.concatenate([a_p, a_s], axis=0), w_out[l].astype(jnp.bfloat16), x_all)
    y = _moe(h1, g_ffn_norm[l], w_router_grp[l], b_router_grp[l], w_router_exp[l], b_router_exp[l],
             w_exp_gate[l], w_exp_up[l], w_exp_down[l])

    def kv_out(rows, col, bsz, tt):
        return rows[:, col:col + KV_COLS].reshape(1, bsz, tt, 2, KV_HEADS, HEAD_DIM)

    hf_p = hf[:n_p]
    win_p = kv_out(hf_p, col_win, bp, t)[:, :, t - min(WINDOW, t):]
    win_new = kv_out(hf_s, col_win, bs, tn)
    win_s = jnp.concatenate([cache_nsa_win_kv[l:l + 1][:, :, tn:], win_new], axis=2)
    logf_p = slab[:n_p, COL_LOGF:COL_LOGF + FOX_HEADS].reshape(1, bp, t, FOX_HEADS)
    return (y[:n_p].reshape(bp, t, d), y[n_p:].reshape(bs, tn, d),
            kv_out(hf_p, col_cmp, bp, t), kv_out(hf_s, col_cmp, bs, tn),
            kv_out(hf_p, col_slc, bp, t), kv_out(hf_s, col_slc, bs, tn),
            win_p, win_s,
            kv_out(hf_p, COL_KV_FOX, bp, t), kv_out(hf_s, COL_KV_FOX, bs, tn),
            logf_p, logf_s.reshape(1, bs, tn, FOX_HEADS))
```

```python
import functools
import math

import jax
import jax.numpy as jnp
from jax import lax
from jax.experimental import pallas as pl
from jax.experimental.pallas import tpu as pltpu
from jax.experimental.pallas import tpu_sc as plsc

HEAD_DIM = 128
NSA_HEADS = 16
FOX_HEADS = 16
KV_HEADS = 4
REP = NSA_HEADS // KV_HEADS
N_BRANCH = 3
CMP_BLOCK = 32
CMP_STRIDE = 16
CMP_RATIO = CMP_BLOCK // CMP_STRIDE
SLC_BLOCK = 64
SLC_TOPK = 16
WINDOW = 512
ROPE_THETA = 10000.0
RMS_EPS = 1e-6
N_GROUPS = 4
EXPERTS_PER_GROUP = 8
N_EXPERTS = N_GROUPS * EXPERTS_PER_GROUP
EXPERT_TOPK = 2
ATTN_SCALE = HEAD_DIM ** -0.5

LANES = 128
GROUP_COLS = REP * HEAD_DIM
KV_COLS = 2 * KV_HEADS * HEAD_DIM
ROW_SLABS = 2 * KV_HEADS
NEG = -1e30
VMEM_LIMIT = 48 * 1024 * 1024
PAGES_PER_STEP = 8

COL_Q_NSA = 0
COL_KV_NSA = NSA_HEADS * HEAD_DIM
COL_Q_FOX = COL_KV_NSA + N_BRANCH * KV_COLS
COL_KV_FOX = COL_Q_FOX + FOX_HEADS * HEAD_DIM
MAIN_COLS = COL_KV_FOX + KV_COLS
N_GATE = N_BRANCH * NSA_HEADS
COL_LOGF = N_GATE


def _pick(n, cands):
    for c in cands:
        if n % c == 0:
            return c
    raise ValueError(f"no tile in {cands} divides {n}")


def _cparams(sem, vmem=VMEM_LIMIT):
    return pltpu.CompilerParams(dimension_semantics=sem, vmem_limit_bytes=vmem)


def _split3(x):
    hi = x.astype(jnp.bfloat16)
    r1 = x - hi.astype(jnp.float32)
    mid = r1.astype(jnp.bfloat16)
    lo = (r1 - mid.astype(jnp.float32)).astype(jnp.bfloat16)
    return hi, mid, lo


def _dot(a, b):
    return jnp.dot(a, b, preferred_element_type=jnp.float32)


def _dot_nt(a, b):
    return lax.dot_general(a, b, (((1,), (1,)), ((), ())), preferred_element_type=jnp.float32)


def _rms_kernel(x_ref, g_ref, o_ref):
    x = x_ref[...]
    ms = jnp.mean(x * x, axis=-1, keepdims=True)
    o_ref[...] = (x * lax.rsqrt(ms + RMS_EPS) * g_ref[...]).astype(o_ref.dtype)


def _rmsnorm(x, g, out_dtype):
    n, d = x.shape
    tm = _pick(n, (256, 128, 64, 32, 16, 8))
    return pl.pallas_call(
        _rms_kernel,
        grid=(n // tm,),
        in_specs=[pl.BlockSpec((tm, d), lambda i: (i, 0)), pl.BlockSpec((1, d), lambda i: (0, 0))],
        out_specs=pl.BlockSpec((tm, d), lambda i: (i, 0)),
        out_shape=jax.ShapeDtypeStruct((n, d), out_dtype),
        compiler_params=_cparams(("parallel",)),
        name="rmsnorm",
    )(x, g.reshape(1, d))


N_COLBLK = MAIN_COLS // GROUP_COLS


def _colblock_kinds():
    kinds = []
    kinds += ["rope"] * (NSA_HEADS // REP)
    for _ in range(N_BRANCH):
        kinds += ["rope", "id"]
    kinds += ["norm"] * (FOX_HEADS // REP)
    kinds += ["norm", "id"]
    assert len(kinds) == N_COLBLK
    return kinds


def _any_eq(j, vals):
    return functools.reduce(jnp.logical_or, [j == v for v in vals])


def _inproj_kernel(x_ref, w_ref, gain_ref, cos_ref, sin_ref, of_ref, ob_ref):
    j = pl.program_id(1)
    kinds = _colblock_kinds()
    id_blocks = [i for i, k in enumerate(kinds) if k == "id"]
    rope_blocks = [i for i, k in enumerate(kinds) if k == "rope"]
    norm_blocks = [i for i, k in enumerate(kinds) if k == "norm"]
    acc = _dot(x_ref[...], w_ref[...])

    def write(y):
        of_ref[...] = y
        ob_ref[...] = y.astype(ob_ref.dtype)

    def normed(rope):
        outs = []
        for s in range(GROUP_COLS // HEAD_DIM):
            h = acc[:, s * HEAD_DIM:(s + 1) * HEAD_DIM]
            g = gain_ref[:, s * HEAD_DIM:(s + 1) * HEAD_DIM]
            y = h * lax.rsqrt(jnp.mean(h * h, axis=-1, keepdims=True) + RMS_EPS) * g
            if rope:
                y = y * cos_ref[...] + pltpu.roll(y, HEAD_DIM // 2, 1) * sin_ref[...]
            outs.append(y)
        return jnp.concatenate(outs, axis=1)

    @pl.when(_any_eq(j, id_blocks))
    def _():
        write(acc)

    @pl.when(_any_eq(j, norm_blocks))
    def _():
        write(normed(False))

    @pl.when(_any_eq(j, rope_blocks))
    def _():
        write(normed(True))


def _inproj(xn, w_main, gain_cols, cos_f, sin_s):
    n, d = xn.shape
    tm = _pick(n, (768, 512, 384, 256, 128, 64, 32, 16, 8))
    tn = GROUP_COLS
    return pl.pallas_call(
        _inproj_kernel,
        grid=(n // tm, MAIN_COLS // tn),
        in_specs=[
            pl.BlockSpec((tm, d), lambda i, j: (i, 0)),
            pl.BlockSpec((d, tn), lambda i, j: (0, j)),
            pl.BlockSpec((1, tn), lambda i, j: (0, j)),
            pl.BlockSpec((tm, HEAD_DIM), lambda i, j: (i, 0)),
            pl.BlockSpec((tm, HEAD_DIM), lambda i, j: (i, 0)),
        ],
        out_specs=[pl.BlockSpec((tm, tn), lambda i, j: (i, j)), pl.BlockSpec((tm, tn), lambda i, j: (i, j))],
        out_shape=[jax.ShapeDtypeStruct((n, MAIN_COLS), jnp.float32),
                   jax.ShapeDtypeStruct((n, MAIN_COLS), jnp.bfloat16)],
        compiler_params=_cparams(("parallel", "arbitrary")),
        name="inproj",
    )(xn, w_main, gain_cols, cos_f, sin_s)


def _small_kernel(x_ref, w_ref, b_ref, o_ref):
    z = _dot(x_ref[...], w_ref[...]) + b_ref[...]
    lane = lax.broadcasted_iota(jnp.int32, z.shape, 1)
    sig = 1.0 / (1.0 + jnp.exp(-z))
    logsig = jnp.minimum(z, 0.0) - jnp.log1p(jnp.exp(-jnp.abs(z)))
    o_ref[...] = jnp.where(lane < N_GATE, sig, logsig)


def _small_proj(xn, w_small, b_small):
    n, d = xn.shape
    tm = _pick(n, (768, 512, 384, 256, 128, 64, 32, 16, 8))
    return pl.pallas_call(
        _small_kernel,
        grid=(n // tm,),
        in_specs=[pl.BlockSpec((tm, d), lambda i: (i, 0)), pl.BlockSpec((d, LANES), lambda i: (0, 0)),
                  pl.BlockSpec((1, LANES), lambda i: (0, 0))],
        out_specs=pl.BlockSpec((tm, LANES), lambda i: (i, 0)),
        out_shape=jax.ShapeDtypeStruct((n, LANES), jnp.float32),
        compiler_params=_cparams(("parallel",)),
        name="small_proj",
    )(xn, w_small, b_small)


def _tri_lower(n):
    r = lax.broadcasted_iota(jnp.int32, (n, n), 0)
    c = lax.broadcasted_iota(jnp.int32, (n, n), 1)
    return (c <= r).astype(jnp.bfloat16)


def _block_cumsum(x, tri):
    hi, mid, lo = _split3(x)
    return _dot(tri, hi) + _dot(tri, mid) + _dot(tri, lo)


def _cumsum_prompt_kernel(x_ref, o_ref, carry_ref):
    @pl.when(pl.program_id(1) == 0)
    def _():
        carry_ref[...] = jnp.zeros_like(carry_ref)

    blk = x_ref.shape[0]
    c = _block_cumsum(x_ref[...], _tri_lower(blk)) + carry_ref[...]
    o_ref[...] = c
    carry_ref[...] = c[blk - 1:blk, :]


def _cumsum_prompt(slab, b, t):
    blk = _pick(t, (512, 256, 128))
    nb = t // blk
    return pl.pallas_call(
        _cumsum_prompt_kernel,
        grid=(b, nb),
        in_specs=[pl.BlockSpec((blk, LANES), lambda i, j: (i * nb + j, 0))],
        out_specs=pl.BlockSpec((blk, LANES), lambda i, j: (i * nb + j, 0)),
        out_shape=jax.ShapeDtypeStruct((b * t, LANES), jnp.float32),
        scratch_shapes=[pltpu.VMEM((1, LANES), jnp.float32)],
        compiler_params=_cparams(("parallel", "arbitrary")),
        name="cumsum_prompt",
    )(slab)


def _cumsum_sample_kernel(pt_ref, *refs, n_steps):
    pages = refs[:PAGES_PER_STEP]
    new_ref, o_ref, carry_ref = refs[PAGES_PER_STEP:]
    s = pl.program_id(1)
    _, h, page = pages[0].shape

    @pl.when(s == 0)
    def _():
        carry_ref[...] = jnp.zeros_like(carry_ref)

    def local(x):
        r = lax.broadcasted_iota(jnp.int32, (page, page), 0)
        c = lax.broadcasted_iota(jnp.int32, (page, page), 1)
        tri = (r <= c).astype(jnp.bfloat16)
        hi, mid, lo = _split3(x)
        return _dot(hi, tri) + _dot(mid, tri) + _dot(lo, tri)

    @pl.when(s < n_steps)
    def _():
        loc = local(jnp.concatenate([p[0] for p in pages], axis=0))
        carry = carry_ref[...]
        for p in range(PAGES_PER_STEP):
            blk = loc[p * h:(p + 1) * h] + carry
            o_ref[0, :, p * page:(p + 1) * page] = blk
            carry = blk[:, page - 1:page]
        carry_ref[...] = carry

    @pl.when(s == n_steps)
    def _():
        o_ref[0, :, 0:page] = local(new_ref[0]) + carry_ref[...]
        o_ref[0, :, page:] = jnp.zeros((h, (PAGES_PER_STEP - 1) * page), jnp.float32)


def _cumsum_sample(page_table, pool_t, new_t):
    bsz, n_pages = page_table.shape
    _, h, page = pool_t.shape
    n_steps = n_pages // PAGES_PER_STEP
    width = PAGES_PER_STEP * page

    def page_spec(k):
        return pl.BlockSpec(
            (1, h, page), lambda b, s, pt: (pt[b, jnp.minimum(s, n_steps - 1) * PAGES_PER_STEP + k], 0, 0))

    kern = functools.partial(_cumsum_sample_kernel, n_steps=n_steps)
    return pl.pallas_call(
        kern,
        grid_spec=pltpu.PrefetchScalarGridSpec(
            num_scalar_prefetch=1,
            grid=(bsz, n_steps + 1),
            in_specs=[page_spec(k) for k in range(PAGES_PER_STEP)] + [
                pl.BlockSpec((1, h, page), lambda b, s, pt: (b, 0, 0))],
            out_specs=pl.BlockSpec((1, h, width), lambda b, s, pt: (b, 0, s)),
            scratch_shapes=[pltpu.VMEM((h, 1), jnp.float32)],
        ),
        out_shape=jax.ShapeDtypeStruct((bsz, h, (n_steps + 1) * width), jnp.float32),
        compiler_params=_cparams(("parallel", "arbitrary")),
        name="cumsum_sample",
    )(page_table, *([pool_t] * PAGES_PER_STEP), new_t)


def _pe_term(pe_ref, w_ref):
    acc = jnp.zeros((16, HEAD_DIM), jnp.float32)
    for j in range(CMP_STRIDE):
        lo = jnp.broadcast_to(pe_ref[j:j + 1, :], (16, HEAD_DIM)).astype(jnp.bfloat16)
        hi = jnp.broadcast_to(pe_ref[CMP_STRIDE + j:CMP_STRIDE + j + 1, :], (16, HEAD_DIM)).astype(jnp.bfloat16)
        w = w_ref[j]
        acc = acc + _dot(lo, w[:, :HEAD_DIM]) + _dot(hi, w[:, HEAD_DIM:])
    return acc[0:1]


def _compress_prompt_kernel(x_ref, w_ref, pe_ref, o_ref, *, n_chunks):
    acc = jnp.zeros((n_chunks, 2 * HEAD_DIM), jnp.float32)
    for j in range(CMP_STRIDE):
        xj = x_ref[pl.ds(j, n_chunks, stride=CMP_STRIDE), :].astype(jnp.bfloat16)
        acc = acc + _dot(xj, w_ref[0, j])
    nxt = pltpu.roll(acc[:, HEAD_DIM:], n_chunks - 1, 0)
    o_ref[0, 0, 0] = (acc[:, :HEAD_DIM] + nxt + _pe_term(pe_ref.at[0], w_ref.at[0])).astype(o_ref.dtype)


def _compress_prompt(hf, b, t, col0, w_cat, pe):
    n_chunks = t // CMP_STRIDE
    rows_blk = col0 // HEAD_DIM
    kern = functools.partial(_compress_prompt_kernel, n_chunks=n_chunks)
    return pl.pallas_call(
        kern,
        grid=(b, 2, KV_HEADS),
        in_specs=[
            pl.BlockSpec((t, HEAD_DIM), lambda i, kv, g: (i, rows_blk + kv * KV_HEADS + g)),
            pl.BlockSpec((1, CMP_STRIDE, HEAD_DIM, 2 * HEAD_DIM), lambda i, kv, g: (kv, 0, 0, 0)),
            pl.BlockSpec((1, CMP_BLOCK, HEAD_DIM), lambda i, kv, g: (kv, 0, 0)),
        ],
        out_specs=pl.BlockSpec((1, 1, 1, n_chunks, HEAD_DIM), lambda i, kv, g: (i, kv, g, 0, 0)),
        out_shape=jax.ShapeDtypeStruct((b, 2, KV_HEADS, n_chunks, HEAD_DIM), jnp.bfloat16),
        compiler_params=_cparams(("parallel", "parallel", "parallel")),
        name="compress_prompt",
    )(hf, w_cat, pe)


def _compress_sample_kernel(pt_ref, *refs, n_steps):
    pages = refs[:PAGES_PER_STEP]
    next_ref, new_ref, w_ref, pe_ref, o_ref = refs[PAGES_PER_STEP:]
    s = pl.program_id(1)
    per_row = 2 * KV_HEADS
    page = pages[0].shape[1] // per_row
    cpp = page // CMP_STRIDE
    n_src = PAGES_PER_STEP + 1
    is_last = s == n_steps - 1
    for kv in range(2):
        acc = jnp.zeros((KV_HEADS * n_src * cpp, 2 * HEAD_DIM), jnp.float32)
        for j in range(CMP_STRIDE):
            pieces = []
            for g in range(KV_HEADS):
                rows = pl.ds(j * per_row + kv * KV_HEADS + g, cpp, stride=CMP_STRIDE * per_row)
                for pg in range(PAGES_PER_STEP):
                    pieces.append(pages[pg][0, rows, :])
                pieces.append(jnp.where(is_last, new_ref[0, rows, :], next_ref[0, rows, :]))
            xj = jnp.concatenate(pieces, axis=0).astype(jnp.bfloat16)
            acc = acc + _dot(xj, w_ref[kv, j])
        pe_t = _pe_term(pe_ref.at[kv], w_ref.at[kv])
        per_g = n_src * cpp
        keep = PAGES_PER_STEP * cpp
        for g in range(KV_HEADS):
            a = acc[g * per_g:(g + 1) * per_g]
            nxt = pltpu.roll(a[:, HEAD_DIM:], per_g - 1, 0)
            o_ref[0, kv, g] = (a[:keep, :HEAD_DIM] + nxt[:keep] + pe_t).astype(o_ref.dtype)


def _compress_sample(page_table, pool, new_pad, w_cat, pe):
    bsz, n_pages = page_table.shape
    _, page, cols = pool.shape
    assert n_pages % PAGES_PER_STEP == 0
    n_steps = n_pages // PAGES_PER_STEP
    cps = PAGES_PER_STEP * (page // ROW_SLABS) // CMP_STRIDE

    def page_spec(k):
        return pl.BlockSpec((1, page, cols), lambda b, s, pt: (pt[b, s * PAGES_PER_STEP + k], 0, 0))

    next_spec = pl.BlockSpec(
        (1, page, cols), lambda b, s, pt: (pt[b, jnp.minimum((s + 1) * PAGES_PER_STEP, n_pages - 1)], 0, 0))
    kern = functools.partial(_compress_sample_kernel, n_steps=n_steps)
    return pl.pallas_call(
        kern,
        grid_spec=pltpu.PrefetchScalarGridSpec(
            num_scalar_prefetch=1,
            grid=(bsz, n_steps),
            in_specs=[page_spec(k) for k in range(PAGES_PER_STEP)] + [
                next_spec,
                pl.BlockSpec((1, page, cols), lambda b, s, pt: (b, 0, 0)),
                pl.BlockSpec((2, CMP_STRIDE, HEAD_DIM, 2 * HEAD_DIM), lambda b, s, pt: (0, 0, 0, 0)),
                pl.BlockSpec((2, CMP_BLOCK, HEAD_DIM), lambda b, s, pt: (0, 0, 0)),
            ],
            out_specs=pl.BlockSpec((1, 2, KV_HEADS, cps, HEAD_DIM), lambda b, s, pt: (b, 0, 0, s, 0)),
        ),
        out_shape=jax.ShapeDtypeStruct((bsz, 2, KV_HEADS, n_steps * cps, HEAD_DIM), jnp.bfloat16),
        compiler_params=_cparams(("parallel", "arbitrary")),
        name="compress_sample",
    )(page_table, *([pool] * (PAGES_PER_STEP + 1)), new_pad, w_cat, pe)


def _stack_heads(q):
    return jnp.concatenate([q[:, r * HEAD_DIM:(r + 1) * HEAD_DIM] for r in range(REP)], axis=0).astype(jnp.bfloat16)


def _div_pow2(x, c):
    assert c & (c - 1) == 0
    return lax.shift_right_arithmetic(x, jnp.int32(c.bit_length() - 1))


def _unstack_heads(o, tq):
    return jnp.concatenate([o[r * tq:(r + 1) * tq] for r in range(REP)], axis=1)


def _cmp_select_kernel(q_ref, kc_ref, vc_ref, ov_ref, o_ref, sel_ref, *, tq, pos_base, n_cmp_valid, n_slc):
    qi = pl.program_id(2)
    nc = kc_ref.shape[3]
    nsp = ov_ref.shape[1]
    q = _stack_heads(q_ref[...])
    s = _dot_nt(q, kc_ref[0, 0, 0]) * ATTN_SCALE
    assert tq & (tq - 1) == 0
    t1 = pos_base + qi * tq + (lax.broadcasted_iota(jnp.int32, (REP * tq, nc), 0) & (tq - 1))
    n1 = lax.broadcasted_iota(jnp.int32, (REP * tq, nc), 1)
    vis = (n1 * CMP_STRIDE + CMP_BLOCK - 1 <= t1) & (n1 < n_cmp_valid)
    s = jnp.where(vis, s, -jnp.inf)
    m = jnp.max(s, axis=-1, keepdims=True)
    m = jnp.where(m == -jnp.inf, 0.0, m)
    p = jnp.exp(s - m)
    p = p / jnp.maximum(jnp.sum(p, axis=-1, keepdims=True), jnp.finfo(jnp.float32).tiny)
    o = _dot(p.astype(jnp.bfloat16), vc_ref[0, 0, 0])
    o_ref[...] = _unstack_heads(o, tq)

    psum = p[0:tq]
    for r in range(1, REP):
        psum = psum + p[r * tq:(r + 1) * tq]
    hi, mid, lo = _split3(psum)
    ov = ov_ref[...]
    imp = _dot(hi, ov) + _dot(mid, ov) + _dot(lo, ov)
    t = pos_base + qi * tq + lax.broadcasted_iota(jnp.int32, (tq, nsp), 0)
    jj = lax.broadcasted_iota(jnp.int32, (tq, nsp), 1)
    cur = _div_pow2(t, SLC_BLOCK)
    forced = (jj == 0) | (jj == cur) | (jj == cur - 1)
    causal = (jj * SLC_BLOCK <= t) & (jj < n_slc)
    imp = jnp.where(forced, jnp.inf, imp)
    imp = jnp.where(causal, imp, -jnp.inf)
    rank = jnp.zeros((tq, nsp), jnp.float32)
    for i in range(n_slc):
        col = imp[:, i:i + 1]
        beats = (col > imp) | ((col == imp) & (jj > i))
        rank = rank + beats.astype(jnp.float32)
    sel = (rank < SLC_TOPK) & (imp > -jnp.inf)
    sel_ref[0] = sel.astype(sel_ref.dtype)


def _overlap_matrix(nc, nsp):
    cs = jnp.arange(nc) * CMP_STRIDE
    ss = jnp.arange(nsp) * SLC_BLOCK
    ov = (cs[:, None] < ss[None, :] + SLC_BLOCK) & (cs[:, None] + CMP_BLOCK > ss[None, :])
    return ov.astype(jnp.bfloat16)


def _cmp_select(q_arr, row_blk0, bsz, nq, tq, kc, pos_base, n_cmp_valid, n_slc, nsp, sel_dtype):
    nc = kc.shape[3]
    rows = bsz * nq * tq
    kern = functools.partial(_cmp_select_kernel, tq=tq, pos_base=pos_base, n_cmp_valid=n_cmp_valid, n_slc=n_slc)
    return pl.pallas_call(
        kern,
        grid=(bsz, KV_HEADS, nq),
        in_specs=[
            pl.BlockSpec((tq, GROUP_COLS), lambda b, g, i: (row_blk0 + b * nq + i, g)),
            pl.BlockSpec((1, 1, 1, nc, HEAD_DIM), lambda b, g, i: (b, 0, g, 0, 0)),
            pl.BlockSpec((1, 1, 1, nc, HEAD_DIM), lambda b, g, i: (b, 1, g, 0, 0)),
            pl.BlockSpec((nc, nsp), lambda b, g, i: (0, 0)),
        ],
        out_specs=[
            pl.BlockSpec((tq, GROUP_COLS), lambda b, g, i: (b * nq + i, g)),
            pl.BlockSpec((1, tq, nsp), lambda b, g, i: (g, b * nq + i, 0)),
        ],
        out_shape=[jax.ShapeDtypeStruct((rows, NSA_HEADS * HEAD_DIM), jnp.float32),
                   jax.ShapeDtypeStruct((KV_HEADS, rows, nsp), sel_dtype)],
        compiler_params=_cparams(("parallel", "parallel", "parallel")),
        name="cmp_select",
    )(q_arr, kc, kc, _overlap_matrix(nc, nsp))


def _online_update(s, v, m_ref, l_ref, acc_ref, rows=slice(None)):
    m_prev = m_ref[rows]
    m_new = jnp.maximum(m_prev, jnp.max(s, axis=-1, keepdims=True))
    alpha = jnp.exp(m_prev - m_new)
    p = jnp.exp(s - m_new)
    l_ref[rows] = alpha * l_ref[rows] + jnp.sum(p, axis=-1, keepdims=True)
    acc_ref[rows] = alpha * acc_ref[rows] + _dot(p.astype(jnp.bfloat16), v)
    m_ref[rows] = m_new


def _flash_kernel(*refs, mode, tq, tk, nk):
    if mode == "slc":
        q_ref, k_ref, v_ref, sel_ref, o_ref, m_ref, l_ref, acc_ref = refs
    elif mode == "fox":
        q_ref, k_ref, v_ref, cq_ref, ck_ref, o_ref, m_ref, l_ref, acc_ref = refs
    else:
        q_ref, k_ref, v_ref, o_ref, m_ref, l_ref, acc_ref = refs
    qi = pl.program_id(2)
    kk = pl.program_id(3)
    if mode == "win":
        kidx = qi * (tq // tk) - WINDOW // tk + kk
        active = kidx >= 0
    else:
        kidx = kk
        active = kk <= ((qi + 1) * tq - 1) // tk

    @pl.when(kk == 0)
    def _():
        m_ref[...] = jnp.full_like(m_ref, NEG)
        l_ref[...] = jnp.zeros_like(l_ref)
        acc_ref[...] = jnp.zeros_like(acc_ref)

    @pl.when(active)
    def _():
        q = _stack_heads(q_ref[...])
        s = _dot_nt(q, k_ref[...]) * ATTN_SCALE
        t = qi * tq + lax.broadcasted_iota(jnp.int32, (tq, tk), 0)
        kp = kidx * tk + lax.broadcasted_iota(jnp.int32, (tq, tk), 1)
        mask = kp <= t
        if mode == "win":
            mask = mask & (t - kp < WINDOW)
        if mode == "slc":
            nsb = sel_ref.shape[2]
            jb = lax.broadcasted_iota(jnp.int32, (nsb, tk), 0)
            kb = kidx * (tk // SLC_BLOCK) + _div_pow2(lax.broadcasted_iota(jnp.int32, (nsb, tk), 1), SLC_BLOCK)
            expand = (jb == kb).astype(jnp.bfloat16)
            mask = mask & (_dot(sel_ref[0], expand) > 0.5)
        add = jnp.where(mask, 0.0, NEG)
        if mode == "fox":
            add = jnp.concatenate([(cq_ref[0][:, r:r + 1] - ck_ref[0][r:r + 1, :]) + add for r in range(REP)], axis=0)
        else:
            add = jnp.concatenate([add] * REP, axis=0)
        _online_update(s + add, v_ref[...], m_ref, l_ref, acc_ref)

    @pl.when(kk == nk - 1)
    def _():
        o_ref[...] = _unstack_heads(acc_ref[...] / l_ref[...], tq)


def _flash_prompt(mode, hb, b, t, q_col, k_col, v_col, sel=None, cq=None, ck=None):
    tq = _pick(t, (256, 128))
    tk = _pick(t, (256,) if mode == "win" else (512, 256, 128))
    nq = t // tq
    nkt = t // tk
    nk = (WINDOW // tk + tq // tk) if mode == "win" else nkt
    qb, kb, vb = q_col // GROUP_COLS, k_col // HEAD_DIM, v_col // HEAD_DIM

    def kidx(i, k):
        if mode == "win":
            return jnp.maximum(i * (tq // tk) - WINDOW // tk + k, 0)
        return jnp.minimum(k, ((i + 1) * tq - 1) // tk)

    in_specs = [
        pl.BlockSpec((tq, GROUP_COLS), lambda bi, g, i, k: (bi * nq + i, qb + g)),
        pl.BlockSpec((tk, HEAD_DIM), lambda bi, g, i, k: (bi * nkt + kidx(i, k), kb + g)),
        pl.BlockSpec((tk, HEAD_DIM), lambda bi, g, i, k: (bi * nkt + kidx(i, k), vb + g)),
    ]
    args = [hb, hb, hb]
    if mode == "slc":
        in_specs.append(pl.BlockSpec((1, tq, sel.shape[2]), lambda bi, g, i, k: (g, bi * nq + i, 0)))
        args.append(sel)
    if mode == "fox":
        in_specs.append(pl.BlockSpec((1, tq, REP), lambda bi, g, i, k: (g, bi * nq + i, 0)))
        in_specs.append(pl.BlockSpec((1, REP, tk), lambda bi, g, i, k: (g, 0, bi * nkt + kidx(i, k))))
        args += [cq, ck]
    kern = functools.partial(_flash_kernel, mode=mode, tq=tq, tk=tk, nk=nk)
    return pl.pallas_call(
        kern,
        grid=(b, KV_HEADS, nq, nk),
        in_specs=in_specs,
        out_specs=pl.BlockSpec((tq, GROUP_COLS), lambda bi, g, i, k: (bi * nq + i, g)),
        out_shape=jax.ShapeDtypeStruct((b * t, KV_HEADS * GROUP_COLS), jnp.float32),
        scratch_shapes=[pltpu.VMEM((REP * tq, 1), jnp.float32), pltpu.VMEM((REP * tq, 1), jnp.float32),
                        pltpu.VMEM((REP * tq, HEAD_DIM), jnp.float32)],
        compiler_params=_cparams(("parallel", "parallel", "parallel", "arbitrary")),
        name="flash_" + mode,
    )(*args)


def _rows_of(g, tn):
    return slice(g * REP * tn, (g + 1) * REP * tn)


def _sample_init(m_ref, l_ref, acc_ref):
    m_ref[...] = jnp.full_like(m_ref, NEG)
    l_ref[...] = jnp.zeros_like(l_ref)
    acc_ref[...] = jnp.zeros_like(acc_ref)


def _sample_finish(o_ref, l_ref, acc_ref, tn):
    res = acc_ref[...] / l_ref[...]
    for g in range(KV_HEADS):
        o_ref[:, g * GROUP_COLS:(g + 1) * GROUP_COLS] = _unstack_heads(res[_rows_of(g, tn)], tn)


def _kv_of(page_refs, g):
    def pick(ref, slab):
        return ref[0, pl.ds(slab, ref.shape[1] // ROW_SLABS, stride=ROW_SLABS), :]
    k = jnp.concatenate([pick(p, g) for p in page_refs], axis=0)
    v = jnp.concatenate([pick(p, KV_HEADS + g) for p in page_refs], axis=0)
    return k.astype(jnp.bfloat16), v.astype(jnp.bfloat16)


def _new_rows_mask(tn, page):
    tok = lax.broadcasted_iota(jnp.int32, (tn, page), 0)
    r = lax.broadcasted_iota(jnp.int32, (tn, page), 1)
    return r <= tok


def _paged_attn_kernel(pt_ref, *refs, mode, n_steps, tn):
    pages = refs[:PAGES_PER_STEP]
    if mode == "slc":
        q_ref, new_ref, sel_ref, selnew_ref, o_ref, m_ref, l_ref, acc_ref = refs[PAGES_PER_STEP:]
    else:
        q_ref, new_ref, cq_ref, ck_ref, cknew_ref, o_ref, m_ref, l_ref, acc_ref = refs[PAGES_PER_STEP:]
    s_id = pl.program_id(1)
    page = pages[0].shape[1] // ROW_SLABS
    nkeys = PAGES_PER_STEP * page

    @pl.when(s_id == 0)
    def _():
        _sample_init(m_ref, l_ref, acc_ref)

    def bias_rows(g, ck, add):
        return jnp.concatenate(
            [(cq_ref[0][:, g * REP + r:g * REP + r + 1] - ck[g * REP + r:g * REP + r + 1, :]) + add
             for r in range(REP)], axis=0)

    if mode == "slc":
        key_blk = _div_pow2(lax.broadcasted_iota(jnp.int32, (tn, nkeys), 1), SLC_BLOCK)
    for g in range(KV_HEADS):
        q = _stack_heads(q_ref[:, g * GROUP_COLS:(g + 1) * GROUP_COLS])
        k, v = _kv_of(pages, g)
        s = _dot_nt(q, k) * ATTN_SCALE
        if mode == "slc":
            flags = sel_ref[0, 0, g]
            picked = jnp.zeros((tn, nkeys), jnp.float32)
            for jb in range(flags.shape[1]):
                picked = jnp.where(key_blk == jb, flags[:, jb:jb + 1], picked)
            add = jnp.where(picked > 0.5, 0.0, NEG)
            add = jnp.concatenate([add] * REP, axis=0)
        else:
            add = bias_rows(g, ck_ref[0], 0.0)
        _online_update(s + add, v, m_ref, l_ref, acc_ref, _rows_of(g, tn))

    @pl.when(s_id == n_steps - 1)
    def _():
        causal = _new_rows_mask(tn, page)
        for g in range(KV_HEADS):
            q = _stack_heads(q_ref[:, g * GROUP_COLS:(g + 1) * GROUP_COLS])
            k, v = _kv_of([new_ref], g)
            s = _dot_nt(q, k) * ATTN_SCALE
            if mode == "slc":
                add = jnp.where(causal & (selnew_ref[0, 0, g][:, 0:1] > 0.5), 0.0, NEG)
                add = jnp.concatenate([add] * REP, axis=0)
            else:
                add = bias_rows(g, cknew_ref[0], jnp.where(causal, 0.0, NEG))
            _online_update(s + add, v, m_ref, l_ref, acc_ref, _rows_of(g, tn))
        _sample_finish(o_ref, l_ref, acc_ref, tn)


def _paged_attn(mode, page_table, pool, new_pad, q_s, tn, sel5=None, cq=None, ck=None):
    bsz, n_pages = page_table.shape
    _, slab_rows, cols = pool.shape
    page = slab_rows // ROW_SLABS
    n_steps = n_pages // PAGES_PER_STEP
    nkeys = PAGES_PER_STEP * page

    def page_spec(k):
        return pl.BlockSpec((1, slab_rows, cols), lambda b, s, pt: (pt[b, s * PAGES_PER_STEP + k], 0, 0))

    in_specs = [page_spec(k) for k in range(PAGES_PER_STEP)] + [
        pl.BlockSpec((tn, NSA_HEADS * HEAD_DIM), lambda b, s, pt: (b, 0)),
        pl.BlockSpec((1, slab_rows, cols), lambda b, s, pt: (b, 0, 0)),
    ]
    args = [pool] * PAGES_PER_STEP + [q_s, new_pad]
    if mode == "slc":
        nb = sel5.shape[4]
        in_specs += [pl.BlockSpec((1, 1, KV_HEADS, tn, nb), lambda b, s, pt: (b, s, 0, 0, 0)),
                     pl.BlockSpec((1, 1, KV_HEADS, tn, nb), lambda b, s, pt: (b, n_steps, 0, 0, 0))]
        args += [sel5, sel5]
    else:
        nh = cq.shape[2]
        in_specs += [pl.BlockSpec((1, tn, nh), lambda b, s, pt: (b, 0, 0)),
                     pl.BlockSpec((1, nh, nkeys), lambda b, s, pt: (b, 0, s)),
                     pl.BlockSpec((1, nh, page), lambda b, s, pt: (b, 0, n_pages))]
        args += [cq, ck, ck]
    rows = KV_HEADS * REP * tn
    kern = functools.partial(_paged_attn_kernel, mode=mode, n_steps=n_steps, tn=tn)
    return pl.pallas_call(
        kern,
        grid_spec=pltpu.PrefetchScalarGridSpec(
            num_scalar_prefetch=1,
            grid=(bsz, n_steps),
            in_specs=in_specs,
            out_specs=pl.BlockSpec((tn, NSA_HEADS * HEAD_DIM), lambda b, s, pt: (b, 0)),
            scratch_shapes=[pltpu.VMEM((rows, 1), jnp.float32), pltpu.VMEM((rows, 1), jnp.float32),
                            pltpu.VMEM((rows, HEAD_DIM), jnp.float32)],
        ),
        out_shape=jax.ShapeDtypeStruct((bsz * tn, NSA_HEADS * HEAD_DIM), jnp.float32),
        compiler_params=_cparams(("parallel", "arbitrary")),
        name="paged_" + mode,
    )(page_table, *args)


def _win_sample_kernel(q_ref, win_ref, new_ref, o_ref, m_ref, l_ref, acc_ref, *, tn, past):
    _sample_init(m_ref, l_ref, acc_ref)
    win_len = win_ref.shape[1] // ROW_SLABS
    page = new_ref.shape[1] // ROW_SLABS
    pos = past + lax.broadcasted_iota(jnp.int32, (tn, win_len), 0)
    kpos = past - win_len + lax.broadcasted_iota(jnp.int32, (tn, win_len), 1)
    wmask = (kpos <= pos) & (pos - kpos < WINDOW)
    add_w = jnp.concatenate([jnp.where(wmask, 0.0, NEG)] * REP, axis=0)
    add_n = jnp.concatenate([jnp.where(_new_rows_mask(tn, page), 0.0, NEG)] * REP, axis=0)
    for g in range(KV_HEADS):
        q = _stack_heads(q_ref[:, g * GROUP_COLS:(g + 1) * GROUP_COLS])
        k, v = _kv_of([new_ref], g)
        _online_update(_dot_nt(q, k) * ATTN_SCALE + add_n, v, m_ref, l_ref, acc_ref, _rows_of(g, tn))
        k, v = _kv_of([win_ref], g)
        _online_update(_dot_nt(q, k) * ATTN_SCALE + add_w, v, m_ref, l_ref, acc_ref, _rows_of(g, tn))
    _sample_finish(o_ref, l_ref, acc_ref, tn)


def _win_sample(win_buf, new_pad, q_s, tn, past):
    bsz, win_rows, cols = win_buf.shape
    new_rows = new_pad.shape[1]
    rows = KV_HEADS * REP * tn
    kern = functools.partial(_win_sample_kernel, tn=tn, past=past)
    return pl.pallas_call(
        kern,
        grid=(bsz,),
        in_specs=[pl.BlockSpec((tn, NSA_HEADS * HEAD_DIM), lambda b: (b, 0)),
                  pl.BlockSpec((1, win_rows, cols), lambda b: (b, 0, 0)),
                  pl.BlockSpec((1, new_rows, cols), lambda b: (b, 0, 0))],
        out_specs=pl.BlockSpec((tn, NSA_HEADS * HEAD_DIM), lambda b: (b, 0)),
        out_shape=jax.ShapeDtypeStruct((bsz * tn, NSA_HEADS * HEAD_DIM), jnp.float32),
        scratch_shapes=[pltpu.VMEM((rows, 1), jnp.float32), pltpu.VMEM((rows, 1), jnp.float32),
                        pltpu.VMEM((rows, HEAD_DIM), jnp.float32)],
        compiler_params=_cparams(("parallel",)),
        name="win_sample",
    )(q_s, win_buf, new_pad)


def _merge_kernel(cmp_ref, slc_ref, win_ref, fox_ref, gate_ref, gn_ref, gf_ref, o_ref):
    gates = gate_ref[...]
    parts = []
    for h in range(NSA_HEADS):
        cols = slice(h * HEAD_DIM, (h + 1) * HEAD_DIM)
        c = N_BRANCH * h
        parts.append(gates[:, c:c + 1] * cmp_ref[:, cols] + gates[:, c + 1:c + 2] * slc_ref[:, cols]
                     + gates[:, c + 2:c + 3] * win_ref[:, cols])
    nsa = jnp.concatenate(parts, axis=1)
    nsa = nsa * lax.rsqrt(jnp.mean(nsa * nsa, axis=-1, keepdims=True) + RMS_EPS) * gn_ref[...]
    fox = fox_ref[...]
    fox = fox * lax.rsqrt(jnp.mean(fox * fox, axis=-1, keepdims=True) + RMS_EPS) * gf_ref[...]
    o_ref[...] = jnp.concatenate([nsa, fox], axis=1).astype(o_ref.dtype)


def _merge(o_cmp, o_slc, o_win, o_fox, slab, g_nsa, g_fox):
    n, w = o_cmp.shape
    tm = _pick(n, (256, 128, 64, 32, 16, 8))
    row = lambda i: (i, 0)
    fix = lambda i: (0, 0)
    return pl.pallas_call(
        _merge_kernel,
        grid=(n // tm,),
        in_specs=[pl.BlockSpec((tm, w), row)] * 4 + [pl.BlockSpec((tm, LANES), row), pl.BlockSpec((1, w), fix),
                                                      pl.BlockSpec((1, w), fix)],
        out_specs=pl.BlockSpec((tm, 2 * w), row),
        out_shape=jax.ShapeDtypeStruct((n, 2 * w), jnp.bfloat16),
        compiler_params=_cparams(("parallel",)),
        name="merge_heads",
    )(o_cmp, o_slc, o_win, o_fox, slab, g_nsa.reshape(1, w), g_fox.reshape(1, w))


def _outproj_kernel(a_ref, w_ref, x_ref, o_ref):
    o_ref[...] = x_ref[...] + _dot(a_ref[...], w_ref[...])


def _outproj(a, w, x):
    n, k = a.shape
    d = w.shape[1]
    tm = _pick(n, (768, 512, 384, 256, 128, 64, 32, 16, 8))
    tn = 512
    return pl.pallas_call(
        _outproj_kernel,
        grid=(n // tm, d // tn),
        in_specs=[pl.BlockSpec((tm, k), lambda i, j: (i, 0)), pl.BlockSpec((k, tn), lambda i, j: (0, j)),
                  pl.BlockSpec((tm, tn), lambda i, j: (i, j))],
        out_specs=pl.BlockSpec((tm, tn), lambda i, j: (i, j)),
        out_shape=jax.ShapeDtypeStruct((n, d), jnp.float32),
        compiler_params=_cparams(("parallel", "arbitrary")),
        name="outproj",
    )(a, w, x)


def _router_kernel(h_ref, g_ref, w_ref, b_ref, xn_ref, eid_ref, gate_ref):
    x = h_ref[...]
    xn = x * lax.rsqrt(jnp.mean(x * x, axis=-1, keepdims=True) + RMS_EPS) * g_ref[...]
    xn_ref[...] = xn.astype(xn_ref.dtype)
    lg = jnp.dot(xn, w_ref[...], precision=lax.Precision.HIGHEST, preferred_element_type=jnp.float32) + b_ref[...]
    lane = lax.broadcasted_iota(jnp.int32, lg.shape, 1)
    lane_f = lane.astype(jnp.float32)
    ninf = -jnp.inf
    is_grp = lane < N_GROUPS
    gl = jnp.where(is_grp, lg, ninf)
    gmax = jnp.max(gl, axis=-1, keepdims=True)
    gsel = jnp.min(jnp.where(gl == gmax, lane_f, float(LANES)), axis=-1, keepdims=True)
    p_sel = 1.0 / jnp.sum(jnp.where(is_grp, jnp.exp(lg - gmax), 0.0), axis=-1, keepdims=True)
    e_lane = lane - N_GROUPS
    e_grp = _div_pow2(e_lane, EXPERTS_PER_GROUP).astype(jnp.float32)
    in_grp = (e_lane >= 0) & (e_lane < N_EXPERTS) & (e_grp == gsel)
    el = jnp.where(in_grp, lg, ninf)
    v1 = jnp.max(el, axis=-1, keepdims=True)
    i1 = jnp.min(jnp.where(el == v1, lane_f, float(LANES)), axis=-1, keepdims=True)
    el2 = jnp.where(lane_f == i1, ninf, el)
    v2 = jnp.max(el2, axis=-1, keepdims=True)
    i2 = jnp.min(jnp.where(el2 == v2, lane_f, float(LANES)), axis=-1, keepdims=True)
    e2 = jnp.exp(v2 - v1)
    den = 1.0 + e2
    g1 = p_sel * (1.0 / den)
    g2 = p_sel * (e2 / den)
    eid = jnp.where(lane == 0, i1 - N_GROUPS, jnp.where(lane == 1, i2 - N_GROUPS, 0.0))
    eid_ref[...] = eid.astype(jnp.int32)
    gate_ref[...] = jnp.where(lane == 0, g1, jnp.where(lane == 1, g2, 0.0))


def _router(h1, g_ffn, w_r, b_r):
    n, d = h1.shape
    tm = _pick(n, (256, 128, 64, 32, 16, 8))
    row = lambda i: (i, 0)
    fix = lambda i: (0, 0)
    return pl.pallas_call(
        _router_kernel,
        grid=(n // tm,),
        in_specs=[pl.BlockSpec((tm, d), row), pl.BlockSpec((1, d), fix), pl.BlockSpec((d, LANES), fix),
                  pl.BlockSpec((1, LANES), fix)],
        out_specs=[pl.BlockSpec((tm, d), row), pl.BlockSpec((tm, LANES), row), pl.BlockSpec((tm, LANES), row)],
        out_shape=[jax.ShapeDtypeStruct((n, d), jnp.float32), jax.ShapeDtypeStruct((n, LANES), jnp.int32),
                   jax.ShapeDtypeStruct((n, LANES), jnp.float32)],
        compiler_params=_cparams(("parallel",)),
        name="router",
    )(h1, g_ffn.reshape(1, d), w_r, b_r)


SC_CORES = 2
SC_SUBCORES = 16
SC_WORKERS = SC_CORES * SC_SUBCORES
SC_GATHER_ROWS = 16


def _gather_rows(idx, table):
    n = idx.shape[0]
    _, d = table.shape
    assert n % (SC_WORKERS * SC_GATHER_ROWS) == 0, n
    per_worker = n // SC_WORKERS
    n_chunks = per_worker // SC_GATHER_ROWS
    mesh = plsc.VectorSubcoreMesh(core_axis_name="c", subcore_axis_name="s")

    @functools.partial(
        pl.kernel, mesh=mesh,
        out_type=jax.ShapeDtypeStruct((n, d), table.dtype),
        scratch_types=[pltpu.VMEM((SC_GATHER_ROWS,), jnp.int32),
                       pltpu.VMEM((SC_GATHER_ROWS, d), table.dtype),
                       pltpu.SemaphoreType.DMA],
    )
    def gather(table_hbm, idx_hbm, out_hbm, idx_v, rows_v, sem):
        wid = lax.axis_index("s") * SC_CORES + lax.axis_index("c")
        base = wid * per_worker

        @pl.loop(0, n_chunks)
        def _(i):
            off = pl.multiple_of(base + i * SC_GATHER_ROWS, SC_GATHER_ROWS)
            pltpu.sync_copy(idx_hbm.at[pl.ds(off, SC_GATHER_ROWS)], idx_v)
            pltpu.async_copy(table_hbm.at[idx_v], rows_v, sem).wait()
            pltpu.sync_copy(rows_v, out_hbm.at[pl.ds(off, SC_GATHER_ROWS)])

    return gather(table, idx)


MOE_TM = 256
MOE_FT = 256
MOE_NT = 1024


def _gateup_kernel(tb_ref, e_ref, f_ref, first_ref, valid_ref, x_ref, wg_ref, wu_ref, h_ref, wgb_ref, wub_ref):
    s = pl.program_id(0)

    @pl.when(first_ref[s] == 1)
    def _():
        wgb_ref[...] = wg_ref[0].astype(jnp.bfloat16)
        wub_ref[...] = wu_ref[0].astype(jnp.bfloat16)

    @pl.when(valid_ref[s] == 1)
    def _():
        x = x_ref[...].astype(jnp.bfloat16)
        a = _dot(x, wgb_ref[...])
        u = _dot(x, wub_ref[...])
        h_ref[...] = (a * (1.0 / (1.0 + jnp.exp(-a))) * u).astype(h_ref.dtype)


def _down_kernel(tb_ref, e_ref, f_ref, first_ref, valid_ref, h_ref, wd_ref, y_ref, wdb_ref):
    s = pl.program_id(0)

    @pl.when(first_ref[s] == 1)
    def _():
        wdb_ref[...] = wd_ref[0].astype(jnp.bfloat16)

    @pl.when(valid_ref[s] == 1)
    def _():
        y_ref[...] = _dot(h_ref[...], wdb_ref[...])


def _work_list(block_e, start_blk, n_blk, total_blocks, n_tiles, n_tb_max):
    n_steps = n_tb_max * n_tiles
    s = jnp.minimum(jnp.arange(n_steps, dtype=jnp.int32), total_blocks * n_tiles - 1)
    valid = (jnp.arange(n_steps, dtype=jnp.int32) < total_blocks * n_tiles).astype(jnp.int32)
    e = block_e[s // n_tiles]
    local = s - n_tiles * start_blk[e]
    nb = jnp.maximum(n_blk[e], 1)
    tile = local // nb
    within = local % nb
    tb = start_blk[e] + within
    first = ((within == 0) & (valid == 1)).astype(jnp.int32)
    return tb.astype(jnp.int32), e.astype(jnp.int32), tile.astype(jnp.int32), first, valid


def _experts(xs, w_gate, w_up, w_down, block_e, start_blk, n_blk, total_blocks):
    a_pad, d = xs.shape
    n_tb_max = a_pad // MOE_TM
    d_exp = w_gate.shape[2]
    wl1 = _work_list(block_e, start_blk, n_blk, total_blocks, d_exp // MOE_FT, n_tb_max)
    hidden = pl.pallas_call(
        _gateup_kernel,
        grid_spec=pltpu.PrefetchScalarGridSpec(
            num_scalar_prefetch=5,
            grid=(wl1[0].shape[0],),
            in_specs=[
                pl.BlockSpec((MOE_TM, d), lambda s, tb, e, f, fi, va: (tb[s], 0)),
                pl.BlockSpec((1, d, MOE_FT), lambda s, tb, e, f, fi, va: (e[s], 0, f[s])),
                pl.BlockSpec((1, d, MOE_FT), lambda s, tb, e, f, fi, va: (e[s], 0, f[s])),
            ],
            out_specs=pl.BlockSpec((MOE_TM, MOE_FT), lambda s, tb, e, f, fi, va: (tb[s], f[s])),
            scratch_shapes=[pltpu.VMEM((d, MOE_FT), jnp.bfloat16), pltpu.VMEM((d, MOE_FT), jnp.bfloat16)],
        ),
        out_shape=jax.ShapeDtypeStruct((a_pad, d_exp), jnp.bfloat16),
        compiler_params=_cparams(("arbitrary",)),
        name="expert_gate_up",
    )(*wl1, xs, w_gate, w_up)
    wl2 = _work_list(block_e, start_blk, n_blk, total_blocks, d // MOE_NT, n_tb_max)
    return pl.pallas_call(
        _down_kernel,
        grid_spec=pltpu.PrefetchScalarGridSpec(
            num_scalar_prefetch=5,
            grid=(wl2[0].shape[0],),
            in_specs=[
                pl.BlockSpec((MOE_TM, d_exp), lambda s, tb, e, f, fi, va: (tb[s], 0)),
                pl.BlockSpec((1, d_exp, MOE_NT), lambda s, tb, e, f, fi, va: (e[s], 0, f[s])),
            ],
            out_specs=pl.BlockSpec((MOE_TM, MOE_NT), lambda s, tb, e, f, fi, va: (tb[s], f[s])),
            scratch_shapes=[pltpu.VMEM((d_exp, MOE_NT), jnp.bfloat16)],
        ),
        out_shape=jax.ShapeDtypeStruct((a_pad, d), jnp.float32),
        compiler_params=_cparams(("arbitrary",)),
        name="expert_down",
    )(*wl2, hidden, w_down)


def _combine_kernel(h_ref, y0_ref, y1_ref, gate_ref, o_ref):
    g = gate_ref[...]
    o_ref[...] = h_ref[...] + (g[:, 0:1] * y0_ref[0] + g[:, 1:2] * y1_ref[0])


def _combine(h1, y2, gate_slab):
    n, d = h1.shape
    tm = _pick(n, (256, 128, 64, 32, 16, 8))
    return pl.pallas_call(
        _combine_kernel,
        grid=(n // tm,),
        in_specs=[pl.BlockSpec((tm, d), lambda i: (i, 0)), pl.BlockSpec((1, tm, d), lambda i: (0, i, 0)),
                  pl.BlockSpec((1, tm, d), lambda i: (1, i, 0)), pl.BlockSpec((tm, LANES), lambda i: (i, 0))],
        out_specs=pl.BlockSpec((tm, d), lambda i: (i, 0)),
        out_shape=jax.ShapeDtypeStruct((n, d), jnp.float32),
        compiler_params=_cparams(("parallel",)),
        name="moe_combine",
    )(h1, y2, y2, gate_slab)


def _moe(h1, g_ffn, w_rg, b_rg, w_re, b_re, w_gate, w_up, w_down):
    n, d = h1.shape
    w_r = jnp.zeros((d, LANES), jnp.float32).at[:, :N_GROUPS].set(w_rg).at[:, N_GROUPS:N_GROUPS + N_EXPERTS].set(w_re)
    b_r = jnp.zeros((1, LANES), jnp.float32).at[0, :N_GROUPS].set(b_rg).at[0, N_GROUPS:N_GROUPS + N_EXPERTS].set(b_re)
    xn, eid_slab, gate_slab = _router(h1, g_ffn, w_r, b_r)
    flat = eid_slab[:, :EXPERT_TOPK].reshape(n * EXPERT_TOPK)
    onehot = (flat[:, None] == jnp.arange(N_EXPERTS, dtype=jnp.int32)[None, :]).astype(jnp.int32)
    rank = jnp.take_along_axis(jnp.cumsum(onehot, axis=0) - onehot, flat[:, None], axis=1)[:, 0]
    counts = jnp.sum(onehot, axis=0)
    n_blk = (counts + MOE_TM - 1) // MOE_TM
    start_blk = jnp.cumsum(n_blk) - n_blk
    total_blocks = jnp.sum(n_blk)
    slot = start_blk[flat] * MOE_TM + rank
    n_tb_max = -(-(n * EXPERT_TOPK) // MOE_TM) + N_EXPERTS
    a_pad = n_tb_max * MOE_TM
    src_tok = jnp.zeros((a_pad,), jnp.int32).at[slot].set(jnp.arange(n * EXPERT_TOPK, dtype=jnp.int32) // EXPERT_TOPK)
    end_blk = jnp.cumsum(n_blk)
    owner = jnp.sum((end_blk[None, :] <= jnp.arange(n_tb_max, dtype=jnp.int32)[:, None]).astype(jnp.int32), axis=1)
    block_e = jnp.minimum(owner, N_EXPERTS - 1).astype(jnp.int32)
    xs = _gather_rows(src_tok, xn)
    yb = _experts(xs, w_gate, w_up, w_down, block_e, start_blk.astype(jnp.int32), n_blk.astype(jnp.int32),
                  total_blocks.astype(jnp.int32))
    back = slot.reshape(n, EXPERT_TOPK).T.reshape(n * EXPERT_TOPK).astype(jnp.int32)
    y2 = _gather_rows(back, yb).reshape(EXPERT_TOPK, n, d)
    return _combine(h1, y2, gate_slab)


def _rope_tables(pos):
    half = HEAD_DIM // 2
    inv = jnp.exp(-math.log(ROPE_THETA) * jnp.arange(half, dtype=jnp.float32) * (2.0 / HEAD_DIM))
    ang = pos.astype(jnp.float32)[:, None] * inv[None, :]
    cos, sin = jnp.cos(ang), jnp.sin(ang)
    return jnp.concatenate([cos, cos], axis=1), jnp.concatenate([-sin, sin], axis=1)


def _repack_weights(w_in, b_gate, b_forget, g_nsa_q, g_nsa_k, g_fox_q, g_fox_k):
    o1 = NSA_HEADS * HEAD_DIM
    o2 = o1 + N_BRANCH * KV_COLS
    o3 = o2 + N_GATE
    o4 = o3 + FOX_HEADS * HEAD_DIM
    o5 = o4 + KV_COLS
    d = w_in.shape[0]
    w_main = jnp.concatenate([w_in[:, :o2], w_in[:, o3:o5]], axis=1).astype(jnp.bfloat16)
    n_f = w_in.shape[1] - o5
    w_small = jnp.zeros((d, LANES), jnp.float32).at[:, :N_GATE].set(w_in[:, o2:o3])
    w_small = w_small.at[:, COL_LOGF:COL_LOGF + n_f].set(w_in[:, o5:]).astype(jnp.bfloat16)
    b_small = jnp.zeros((1, LANES), jnp.float32).at[0, :N_GATE].set(b_gate.reshape(N_GATE))
    b_small = b_small.at[0, COL_LOGF:COL_LOGF + n_f].set(b_forget)
    ones = jnp.ones((KV_HEADS * HEAD_DIM,), jnp.float32)
    gains = [jnp.tile(g_nsa_q, NSA_HEADS)]
    for br in range(N_BRANCH):
        gains += [jnp.tile(g_nsa_k[br], KV_HEADS), ones]
    gains += [jnp.tile(g_fox_q, FOX_HEADS), jnp.tile(g_fox_k, KV_HEADS), ones]
    return w_main, w_small, b_small, jnp.concatenate(gains).reshape(1, MAIN_COLS)


def _cmp_weights(w_k, w_v, pe_k, pe_v):
    def cat(w):
        return jnp.concatenate([w[:CMP_STRIDE], w[CMP_STRIDE:]], axis=2)
    return jnp.stack([cat(w_k), cat(w_v)]).astype(jnp.bfloat16), jnp.stack([pe_k, pe_v])


def _pad_rows(x, rows):
    return jnp.pad(x, ((0, 0), (0, rows - x.shape[1]), (0, 0)))


def kernel(x_prompt, x_sample, cache_nsa_cmp_kv, cache_nsa_slc_kv, cache_nsa_win_kv, cache_fox_kv, cache_fox_logf, page_table, g_attn_norm, w_in, b_nsa_gate, b_fox_forget, g_nsa_q, g_nsa_k, g_fox_q, g_fox_k, w_cmp_k, w_cmp_v, pe_cmp_k, pe_cmp_v, g_out_nsa, g_out_fox, w_out, g_ffn_norm, w_router_grp, b_router_grp, w_router_exp, b_router_exp, w_exp_gate, w_exp_up, w_exp_down):
    depth = w_in.shape[0]
    assert depth == 1, "single-layer step"
    bp, t, d = x_prompt.shape
    bs, tn, _ = x_sample.shape
    n_pages = page_table.shape[1]
    page = cache_fox_kv.shape[2]
    past = n_pages * page
    n_p, n_s = bp * t, bs * tn
    l = 0

    x_all = jnp.concatenate([x_prompt.reshape(n_p, d), x_sample.reshape(n_s, d)], axis=0)
    pos_all = jnp.concatenate([jnp.tile(jnp.arange(t, dtype=jnp.int32), bp),
                               jnp.tile(past + jnp.arange(tn, dtype=jnp.int32), bs)])
    cos_f, sin_s = _rope_tables(pos_all)
    w_main, w_small, b_small, gain_cols = _repack_weights(
        w_in[l], b_nsa_gate[l], b_fox_forget[l], g_nsa_q[l], g_nsa_k[l], g_fox_q[l], g_fox_k[l])

    xn = _rmsnorm(x_all, g_attn_norm[l], jnp.bfloat16)
    hf, hb = _inproj(xn, w_main, gain_cols, cos_f, sin_s)
    slab = _small_proj(xn, w_small, b_small)

    col_cmp, col_slc, col_win = (COL_KV_NSA + br * KV_COLS for br in range(N_BRANCH))
    w_cat, pe_cat = _cmp_weights(w_cmp_k[l], w_cmp_v[l], pe_cmp_k[l], pe_cmp_v[l])
    v_off = KV_HEADS * HEAD_DIM

    tq = _pick(t, (256, 128))
    kc_p = _compress_prompt(hf, bp, t, col_cmp, w_cat, pe_cat)
    n_slc_p = max(-(-t // SLC_BLOCK), SLC_TOPK)
    nsp_p = -(-n_slc_p // LANES) * LANES if n_slc_p > 64 else 64
    o_cmp_p, sel_p = _cmp_select(hb, 0, bp, t // tq, tq, kc_p, 0, t // CMP_STRIDE - CMP_RATIO + 1, n_slc_p, nsp_p,
                                 jnp.bfloat16)
    o_slc_p = _flash_prompt("slc", hb, bp, t, COL_Q_NSA, col_slc, col_slc + v_off, sel=sel_p)
    o_win_p = _flash_prompt("win", hb, bp, t, COL_Q_NSA, col_win, col_win + v_off)
    c_p = _cumsum_prompt(slab[:n_p], bp, t)[:, COL_LOGF:COL_LOGF + FOX_HEADS]
    cq_p = c_p.reshape(n_p, KV_HEADS, REP).transpose(1, 0, 2)
    ck_p = c_p.T.reshape(KV_HEADS, REP, n_p)
    o_fox_p = _flash_prompt("fox", hb, bp, t, COL_Q_FOX, COL_KV_FOX, COL_KV_FOX + v_off, cq=cq_p, ck=ck_p)
    a_p = _merge(o_cmp_p, o_slc_p, o_win_p, o_fox_p, slab[:n_p], g_out_nsa[l], g_out_fox[l])

    hf_s, slab_s = hf[n_p:], slab[n_p:]
    q_nsa_s = hf_s[:, COL_Q_NSA:COL_Q_NSA + NSA_HEADS * HEAD_DIM]
    q_fox_s = hf_s[:, COL_Q_FOX:COL_Q_FOX + FOX_HEADS * HEAD_DIM]

    def new_rows(col):
        rows = _pad_rows(hf_s[:, col:col + KV_COLS].reshape(bs, tn, KV_COLS), page)
        return rows.reshape(bs, page * ROW_SLABS, HEAD_DIM)

    pool = lambda c: c[l].reshape(c.shape[1], page * ROW_SLABS, HEAD_DIM)
    kc_s = _compress_sample(page_table, pool(cache_nsa_cmp_kv), new_rows(col_cmp), w_cat, pe_cat)
    t_ctx = past + tn
    n_cmp_s = -(-t_ctx // CMP_STRIDE) - CMP_RATIO + 1
    n_slc_s = max(-(-t_ctx // SLC_BLOCK), SLC_TOPK)
    blocks_per_step = PAGES_PER_STEP * page // SLC_BLOCK
    nsp_s = -(-(n_slc_s + 1) // LANES) * LANES
    nsp_s = -(-nsp_s // blocks_per_step) * blocks_per_step
    o_cmp_s, sel_s = _cmp_select(q_nsa_s, 0, bs, 1, tn, kc_s, past, n_cmp_s, n_slc_s, nsp_s, jnp.float32)
    sel5 = sel_s.reshape(KV_HEADS, bs, tn, nsp_s // blocks_per_step, blocks_per_step).transpose(1, 3, 0, 2, 4)
    o_slc_s = _paged_attn("slc", page_table, pool(cache_nsa_slc_kv), new_rows(col_slc), q_nsa_s, tn, sel5=sel5)
    win_buf = cache_nsa_win_kv[l].reshape(bs, -1, HEAD_DIM)
    o_win_s = _win_sample(win_buf, new_rows(col_win), q_nsa_s, tn, past)
    logf_s = slab_s[:, COL_LOGF:COL_LOGF + FOX_HEADS].reshape(bs, tn, FOX_HEADS)
    ck_s = _cumsum_sample(page_table, cache_fox_logf[l].transpose(0, 2, 1),
                          _pad_rows(logf_s, page).transpose(0, 2, 1))
    cq_s = ck_s[:, :, past:past + tn].transpose(0, 2, 1)
    o_fox_s = _paged_attn("fox", page_table, pool(cache_fox_kv), new_rows(COL_KV_FOX), q_fox_s, tn, cq=cq_s, ck=ck_s)
    a_s = _merge(o_cmp_s, o_slc_s, o_win_s, o_fox_s, slab_s, g_out_nsa[l], g_out_fox[l])

    h1 = _outproj(jnp.concatenate([a_p, a_s], axis=0), w_out[l].astype(jnp.bfloat16), x_all)
    y = _moe(h1, g_ffn_norm[l], w_router_grp[l], b_router_grp[l], w_router_exp[l], b_router_exp[l],
             w_exp_gate[l], w_exp_up[l], w_exp_down[l])

    def kv_out(rows, col, bsz, tt):
        return rows[:, col:col + KV_COLS].reshape(1, bsz, tt, 2, KV_HEADS, HEAD_DIM)

    hf_p = hf[:n_p]
    win_p = kv_out(hf_p, col_win, bp, t)[:, :, t - min(WINDOW, t):]
    win_new = kv_out(hf_s, col_win, bs, tn)
    win_s = jnp.concatenate([cache_nsa_win_kv[l:l + 1][:, :, tn:], win_new], axis=2)
    logf_p = slab[:n_p, COL_LOGF:COL_LOGF + FOX_HEADS].reshape(1, bp, t, FOX_HEADS)
    return (y[:n_p].reshape(bp, t, d), y[n_p:].reshape(bs, tn, d),
            kv_out(hf_p, col_cmp, bp, t), kv_out(hf_s, col_cmp, bs, tn),
            kv_out(hf_p, col_slc, bp, t), kv_out(hf_s, col_slc, bs, tn),
            win_p, win_s,
            kv_out(hf_p, COL_KV_FOX, bp, t), kv_out(hf_s, COL_KV_FOX, bs, tn),
            logf_p, logf_s.reshape(1, bs, tn, FOX_HEADS))
```

```python
import functools
import math

import jax
import jax.numpy as jnp
from jax import lax
from jax.experimental import pallas as pl
from jax.experimental.pallas import tpu as pltpu
from jax.experimental.pallas import tpu_sc as plsc

HEAD_DIM = 128
NSA_HEADS = 16
FOX_HEADS = 16
KV_HEADS = 4
REP = NSA_HEADS // KV_HEADS
N_BRANCH = 3
CMP_BLOCK = 32
CMP_STRIDE = 16
CMP_RATIO = CMP_BLOCK // CMP_STRIDE
SLC_BLOCK = 64
SLC_TOPK = 16
WINDOW = 512
ROPE_THETA = 10000.0
RMS_EPS = 1e-6
N_GROUPS = 4
EXPERTS_PER_GROUP = 8
N_EXPERTS = N_GROUPS * EXPERTS_PER_GROUP
EXPERT_TOPK = 2
ATTN_SCALE = HEAD_DIM ** -0.5

LANES = 128
GROUP_COLS = REP * HEAD_DIM
KV_COLS = 2 * KV_HEADS * HEAD_DIM
ROW_SLABS = 2 * KV_HEADS
NEG = -1e30
VMEM_LIMIT = 48 * 1024 * 1024
PAGES_PER_STEP = 8

COL_Q_NSA = 0
COL_KV_NSA = NSA_HEADS * HEAD_DIM
COL_Q_FOX = COL_KV_NSA + N_BRANCH * KV_COLS
COL_KV_FOX = COL_Q_FOX + FOX_HEADS * HEAD_DIM
MAIN_COLS = COL_KV_FOX + KV_COLS
N_GATE = N_BRANCH * NSA_HEADS
COL_LOGF = N_GATE


def _pick(n, cands):
    for c in cands:
        if n % c == 0:
            return c
    raise ValueError(f"no tile in {cands} divides {n}")


def _cparams(sem, vmem=VMEM_LIMIT):
    return pltpu.CompilerParams(dimension_semantics=sem, vmem_limit_bytes=vmem)


def _split3(x):
    hi = x.astype(jnp.bfloat16)
    r1 = x - hi.astype(jnp.float32)
    mid = r1.astype(jnp.bfloat16)
    lo = (r1 - mid.astype(jnp.float32)).astype(jnp.bfloat16)
    return hi, mid, lo


def _dot(a, b):
    return jnp.dot(a, b, preferred_element_type=jnp.float32)


def _dot_nt(a, b):
    return lax.dot_general(a, b, (((1,), (1,)), ((), ())), preferred_element_type=jnp.float32)


def _rms_kernel(x_ref, g_ref, o_ref):
    x = x_ref[...]
    ms = jnp.mean(x * x, axis=-1, keepdims=True)
    o_ref[...] = (x * lax.rsqrt(ms + RMS_EPS) * g_ref[...]).astype(o_ref.dtype)


def _rmsnorm(x, g, out_dtype):
    n, d = x.shape
    tm = _pick(n, (256, 128, 64, 32, 16, 8))
    return pl.pallas_call(
        _rms_kernel,
        grid=(n // tm,),
        in_specs=[pl.BlockSpec((tm, d), lambda i: (i, 0)), pl.BlockSpec((1, d), lambda i: (0, 0))],
        out_specs=pl.BlockSpec((tm, d), lambda i: (i, 0)),
        out_shape=jax.ShapeDtypeStruct((n, d), out_dtype),
        compiler_params=_cparams(("parallel",)),
        name="rmsnorm",
    )(x, g.reshape(1, d))


N_COLBLK = MAIN_COLS // GROUP_COLS


def _colblock_kinds():
    kinds = []
    kinds += ["rope"] * (NSA_HEADS // REP)
    for _ in range(N_BRANCH):
        kinds += ["rope", "id"]
    kinds += ["norm"] * (FOX_HEADS // REP)
    kinds += ["norm", "id"]
    assert len(kinds) == N_COLBLK
    return kinds


def _any_eq(j, vals):
    return functools.reduce(jnp.logical_or, [j == v for v in vals])


def _inproj_kernel(x_ref, w_ref, gain_ref, cos_ref, sin_ref, of_ref, ob_ref):
    j = pl.program_id(1)
    kinds = _colblock_kinds()
    id_blocks = [i for i, k in enumerate(kinds) if k == "id"]
    rope_blocks = [i for i, k in enumerate(kinds) if k == "rope"]
    norm_blocks = [i for i, k in enumerate(kinds) if k == "norm"]
    acc = _dot(x_ref[...], w_ref[...])

    def write(y):
        of_ref[...] = y
        ob_ref[...] = y.astype(ob_ref.dtype)

    def normed(rope):
        outs = []
        for s in range(GROUP_COLS // HEAD_DIM):
            h = acc[:, s * HEAD_DIM:(s + 1) * HEAD_DIM]
            g = gain_ref[:, s * HEAD_DIM:(s + 1) * HEAD_DIM]
            y = h * lax.rsqrt(jnp.mean(h * h, axis=-1, keepdims=True) + RMS_EPS) * g
            if rope:
                y = y * cos_ref[...] + pltpu.roll(y, HEAD_DIM // 2, 1) * sin_ref[...]
            outs.append(y)
        return jnp.concatenate(outs, axis=1)

    @pl.when(_any_eq(j, id_blocks))
    def _():
        write(acc)

    @pl.when(_any_eq(j, norm_blocks))
    def _():
        write(normed(False))

    @pl.when(_any_eq(j, rope_blocks))
    def _():
        write(normed(True))


def _inproj(xn, w_main, gain_cols, cos_f, sin_s):
    n, d = xn.shape
    tm = _pick(n, (768, 512, 384, 256, 128, 64, 32, 16, 8))
    tn = GROUP_COLS
    return pl.pallas_call(
        _inproj_kernel,
        grid=(n // tm, MAIN_COLS // tn),
        in_specs=[
            pl.BlockSpec((tm, d), lambda i, j: (i, 0)),
            pl.BlockSpec((d, tn), lambda i, j: (0, j)),
            pl.BlockSpec((1, tn), lambda i, j: (0, j)),
            pl.BlockSpec((tm, HEAD_DIM), lambda i, j: (i, 0)),
            pl.BlockSpec((tm, HEAD_DIM), lambda i, j: (i, 0)),
        ],
        out_specs=[pl.BlockSpec((tm, tn), lambda i, j: (i, j)), pl.BlockSpec((tm, tn), lambda i, j: (i, j))],
        out_shape=[jax.ShapeDtypeStruct((n, MAIN_COLS), jnp.float32),
                   jax.ShapeDtypeStruct((n, MAIN_COLS), jnp.bfloat16)],
        compiler_params=_cparams(("parallel", "arbitrary")),
        name="inproj",
    )(xn, w_main, gain_cols, cos_f, sin_s)


def _small_kernel(x_ref, w_ref, b_ref, o_ref):
    z = _dot(x_ref[...], w_ref[...]) + b_ref[...]
    lane = lax.broadcasted_iota(jnp.int32, z.shape, 1)
    sig = 1.0 / (1.0 + jnp.exp(-z))
    logsig = jnp.minimum(z, 0.0) - jnp.log1p(jnp.exp(-jnp.abs(z)))
    o_ref[...] = jnp.where(lane < N_GATE, sig, logsig)


def _small_proj(xn, w_small, b_small):
    n, d = xn.shape
    tm = _pick(n, (768, 512, 384, 256, 128, 64, 32, 16, 8))
    return pl.pallas_call(
        _small_kernel,
        grid=(n // tm,),
        in_specs=[pl.BlockSpec((tm, d), lambda i: (i, 0)), pl.BlockSpec((d, LANES), lambda i: (0, 0)),
                  pl.BlockSpec((1, LANES), lambda i: (0, 0))],
        out_specs=pl.BlockSpec((tm, LANES), lambda i: (i, 0)),
        out_shape=jax.ShapeDtypeStruct((n, LANES), jnp.float32),
        compiler_params=_cparams(("parallel",)),
        name="small_proj",
    )(xn, w_small, b_small)


def _tri_lower(n):
    r = lax.broadcasted_iota(jnp.int32, (n, n), 0)
    c = lax.broadcasted_iota(jnp.int32, (n, n), 1)
    return (c <= r).astype(jnp.bfloat16)


def _block_cumsum(x, tri):
    hi, mid, lo = _split3(x)
    return _dot(tri, hi) + _dot(tri, mid) + _dot(tri, lo)


def _cumsum_prompt_kernel(x_ref, o_ref, carry_ref):
    @pl.when(pl.program_id(1) == 0)
    def _():
        carry_ref[...] = jnp.zeros_like(carry_ref)

    blk = x_ref.shape[0]
    c = _block_cumsum(x_ref[...], _tri_lower(blk)) + carry_ref[...]
    o_ref[...] = c
    carry_ref[...] = c[blk - 1:blk, :]


def _cumsum_prompt(slab, b, t):
    blk = _pick(t, (512, 256, 128))
    nb = t // blk
    return pl.pallas_call(
        _cumsum_prompt_kernel,
        grid=(b, nb),
        in_specs=[pl.BlockSpec((blk, LANES), lambda i, j: (i * nb + j, 0))],
        out_specs=pl.BlockSpec((blk, LANES), lambda i, j: (i * nb + j, 0)),
        out_shape=jax.ShapeDtypeStruct((b * t, LANES), jnp.float32),
        scratch_shapes=[pltpu.VMEM((1, LANES), jnp.float32)],
        compiler_params=_cparams(("parallel", "arbitrary")),
        name="cumsum_prompt",
    )(slab)


def _cumsum_sample_kernel(pt_ref, *refs, n_steps):
    pages = refs[:PAGES_PER_STEP]
    new_ref, o_ref, carry_ref = refs[PAGES_PER_STEP:]
    s = pl.program_id(1)
    _, h, page = pages[0].shape

    @pl.when(s == 0)
    def _():
        carry_ref[...] = jnp.zeros_like(carry_ref)

    def local(x):
        r = lax.broadcasted_iota(jnp.int32, (page, page), 0)
        c = lax.broadcasted_iota(jnp.int32, (page, page), 1)
        tri = (r <= c).astype(jnp.bfloat16)
        hi, mid, lo = _split3(x)
        return _dot(hi, tri) + _dot(mid, tri) + _dot(lo, tri)

    @pl.when(s < n_steps)
    def _():
        loc = local(jnp.concatenate([p[0] for p in pages], axis=0))
        carry = carry_ref[...]
        for p in range(PAGES_PER_STEP):
            blk = loc[p * h:(p + 1) * h] + carry
            o_ref[0, :, p * page:(p + 1) * page] = blk
            carry = blk[:, page - 1:page]
        carry_ref[...] = carry

    @pl.when(s == n_steps)
    def _():
        o_ref[0, :, 0:page] = local(new_ref[0]) + carry_ref[...]
        o_ref[0, :, page:] = jnp.zeros((h, (PAGES_PER_STEP - 1) * page), jnp.float32)


def _cumsum_sample(page_table, pool_t, new_t):
    bsz, n_pages = page_table.shape
    _, h, page = pool_t.shape
    n_steps = n_pages // PAGES_PER_STEP
    width = PAGES_PER_STEP * page

    def page_spec(k):
        return pl.BlockSpec(
            (1, h, page), lambda b, s, pt: (pt[b, jnp.minimum(s, n_steps - 1) * PAGES_PER_STEP + k], 0, 0))

    kern = functools.partial(_cumsum_sample_kernel, n_steps=n_steps)
    return pl.pallas_call(
        kern,
        grid_spec=pltpu.PrefetchScalarGridSpec(
            num_scalar_prefetch=1,
            grid=(bsz, n_steps + 1),
            in_specs=[page_spec(k) for k in range(PAGES_PER_STEP)] + [
                pl.BlockSpec((1, h, page), lambda b, s, pt: (b, 0, 0))],
            out_specs=pl.BlockSpec((1, h, width), lambda b, s, pt: (b, 0, s)),
            scratch_shapes=[pltpu.VMEM((h, 1), jnp.float32)],
        ),
        out_shape=jax.ShapeDtypeStruct((bsz, h, (n_steps + 1) * width), jnp.float32),
        compiler_params=_cparams(("parallel", "arbitrary")),
        name="cumsum_sample",
    )(page_table, *([pool_t] * PAGES_PER_STEP), new_t)


def _pe_term(pe_ref, w_ref):
    acc = jnp.zeros((16, HEAD_DIM), jnp.float32)
    for j in range(CMP_STRIDE):
        lo = jnp.broadcast_to(pe_ref[j:j + 1, :], (16, HEAD_DIM)).astype(jnp.bfloat16)
        hi = jnp.broadcast_to(pe_ref[CMP_STRIDE + j:CMP_STRIDE + j + 1, :], (16, HEAD_DIM)).astype(jnp.bfloat16)
        w = w_ref[j]
        acc = acc + _dot(lo, w[:, :HEAD_DIM]) + _dot(hi, w[:, HEAD_DIM:])
    return acc[0:1]


def _compress_prompt_kernel(x_ref, w_ref, pe_ref, o_ref, *, n_chunks):
    acc = jnp.zeros((n_chunks, 2 * HEAD_DIM), jnp.float32)
    for j in range(CMP_STRIDE):
        xj = x_ref[pl.ds(j, n_chunks, stride=CMP_STRIDE), :].astype(jnp.bfloat16)
        acc = acc + _dot(xj, w_ref[0, j])
    nxt = pltpu.roll(acc[:, HEAD_DIM:], n_chunks - 1, 0)
    o_ref[0, 0, 0] = (acc[:, :HEAD_DIM] + nxt + _pe_term(pe_ref.at[0], w_ref.at[0])).astype(o_ref.dtype)


def _compress_prompt(hf, b, t, col0, w_cat, pe):
    n_chunks = t // CMP_STRIDE
    rows_blk = col0 // HEAD_DIM
    kern = functools.partial(_compress_prompt_kernel, n_chunks=n_chunks)
    return pl.pallas_call(
        kern,
        grid=(b, 2, KV_HEADS),
        in_specs=[
            pl.BlockSpec((t, HEAD_DIM), lambda i, kv, g: (i, rows_blk + kv * KV_HEADS + g)),
            pl.BlockSpec((1, CMP_STRIDE, HEAD_DIM, 2 * HEAD_DIM), lambda i, kv, g: (kv, 0, 0, 0)),
            pl.BlockSpec((1, CMP_BLOCK, HEAD_DIM), lambda i, kv, g: (kv, 0, 0)),
        ],
        out_specs=pl.BlockSpec((1, 1, 1, n_chunks, HEAD_DIM), lambda i, kv, g: (i, kv, g, 0, 0)),
        out_shape=jax.ShapeDtypeStruct((b, 2, KV_HEADS, n_chunks, HEAD_DIM), jnp.bfloat16),
        compiler_params=_cparams(("parallel", "parallel", "parallel")),
        name="compress_prompt",
    )(hf, w_cat, pe)


def _compress_sample_kernel(pt_ref, *refs, n_steps):
    pages = refs[:PAGES_PER_STEP]
    next_ref, new_ref, w_ref, pe_ref, o_ref = refs[PAGES_PER_STEP:]
    s = pl.program_id(1)
    chunk_rows = CMP_STRIDE * ROW_SLABS
    cpp = pages[0].shape[1] // chunk_rows
    n_out = PAGES_PER_STEP * cpp * ROW_SLABS
    is_last = s == n_steps - 1

    def chunk_lhs(ref, c):
        base = c * chunk_rows
        return jnp.concatenate([ref[0, base + j * ROW_SLABS:base + (j + 1) * ROW_SLABS, :]
                                for j in range(CMP_STRIDE)], axis=1)

    lhs = [chunk_lhs(p, c) for p in pages for c in range(cpp)]
    lhs.append(jnp.where(is_last, chunk_lhs(new_ref, 0), chunk_lhs(next_ref, 0)))
    out = _dot(jnp.concatenate(lhs, axis=0).astype(jnp.bfloat16), w_ref[...])
    pe = _dot(pe_ref[...].astype(jnp.bfloat16), w_ref[...])
    own, nxt = out[:n_out], out[ROW_SLABS:n_out + ROW_SLABS]
    d = HEAD_DIM
    is_k = (lax.broadcasted_iota(jnp.int32, (n_out, d), 0) & (ROW_SLABS - 1)) < KV_HEADS
    pe_k = pe[0:1, 0:d] + pe[1:2, d:2 * d]
    pe_v = pe[2:3, 2 * d:3 * d] + pe[3:4, 3 * d:4 * d]
    o_ref[0] = jnp.where(is_k, own[:, 0:d] + nxt[:, d:2 * d] + pe_k, own[:, 2 * d:3 * d] + nxt[:, 3 * d:4 * d] + pe_v)


def _compress_sample(page_table, pool, new_pad, w_cat, pe):
    bsz, n_pages = page_table.shape
    _, page, cols = pool.shape
    assert n_pages % PAGES_PER_STEP == 0
    n_steps = n_pages // PAGES_PER_STEP
    chunk_rows = CMP_STRIDE * ROW_SLABS
    cps = PAGES_PER_STEP * page // chunk_rows
    w_flat = jnp.concatenate([w_cat[0].reshape(CMP_STRIDE * HEAD_DIM, 2 * HEAD_DIM),
                              w_cat[1].reshape(CMP_STRIDE * HEAD_DIM, 2 * HEAD_DIM)], axis=1)
    pe_lhs = jnp.pad(pe.reshape(2 * CMP_RATIO, CMP_STRIDE * HEAD_DIM), ((0, ROW_SLABS - 2 * CMP_RATIO), (0, 0)))

    def page_spec(k):
        return pl.BlockSpec((1, page, cols), lambda b, s, pt: (pt[b, s * PAGES_PER_STEP + k], 0, 0))

    next_spec = pl.BlockSpec(
        (1, chunk_rows, cols), lambda b, s, pt: (pt[b, jnp.minimum((s + 1) * PAGES_PER_STEP, n_pages - 1)], 0, 0))
    kern = functools.partial(_compress_sample_kernel, n_steps=n_steps)
    out = pl.pallas_call(
        kern,
        grid_spec=pltpu.PrefetchScalarGridSpec(
            num_scalar_prefetch=1,
            grid=(bsz, n_steps),
            in_specs=[page_spec(k) for k in range(PAGES_PER_STEP)] + [
                next_spec,
                pl.BlockSpec((1, chunk_rows, cols), lambda b, s, pt: (b, 0, 0)),
                pl.BlockSpec(w_flat.shape, lambda b, s, pt: (0, 0)),
                pl.BlockSpec(pe_lhs.shape, lambda b, s, pt: (0, 0)),
            ],
            out_specs=pl.BlockSpec((1, cps * ROW_SLABS, HEAD_DIM), lambda b, s, pt: (b, s, 0)),
        ),
        out_shape=jax.ShapeDtypeStruct((bsz, n_steps * cps * ROW_SLABS, HEAD_DIM), jnp.float32),
        compiler_params=_cparams(("parallel", "arbitrary")),
        name="compress_sample",
    )(page_table, *([pool] * (PAGES_PER_STEP + 1)), new_pad, w_flat, pe_lhs)
    out = out.reshape(bsz, n_steps * cps, 2, KV_HEADS, HEAD_DIM).transpose(0, 2, 3, 1, 4)
    return out.astype(jnp.bfloat16)


def _stack_heads(q):
    return jnp.concatenate([q[:, r * HEAD_DIM:(r + 1) * HEAD_DIM] for r in range(REP)], axis=0).astype(jnp.bfloat16)


def _div_pow2(x, c):
    assert c & (c - 1) == 0
    return lax.shift_right_arithmetic(x, jnp.int32(c.bit_length() - 1))


def _unstack_heads(o, tq):
    return jnp.concatenate([o[r * tq:(r + 1) * tq] for r in range(REP)], axis=1)


def _cmp_select_kernel(q_ref, kc_ref, vc_ref, ov_ref, o_ref, sel_ref, *, tq, pos_base, n_cmp_valid, n_slc):
    qi = pl.program_id(2)
    nc = kc_ref.shape[3]
    nsp = ov_ref.shape[1]
    q = _stack_heads(q_ref[...])
    s = _dot_nt(q, kc_ref[0, 0, 0]) * ATTN_SCALE
    assert tq & (tq - 1) == 0
    t1 = pos_base + qi * tq + (lax.broadcasted_iota(jnp.int32, (REP * tq, nc), 0) & (tq - 1))
    n1 = lax.broadcasted_iota(jnp.int32, (REP * tq, nc), 1)
    vis = (n1 * CMP_STRIDE + CMP_BLOCK - 1 <= t1) & (n1 < n_cmp_valid)
    s = jnp.where(vis, s, -jnp.inf)
    m = jnp.max(s, axis=-1, keepdims=True)
    m = jnp.where(m == -jnp.inf, 0.0, m)
    p = jnp.exp(s - m)
    p = p / jnp.maximum(jnp.sum(p, axis=-1, keepdims=True), jnp.finfo(jnp.float32).tiny)
    o = _dot(p.astype(jnp.bfloat16), vc_ref[0, 0, 0])
    o_ref[...] = _unstack_heads(o, tq)

    psum = p[0:tq]
    for r in range(1, REP):
        psum = psum + p[r * tq:(r + 1) * tq]
    hi, mid, lo = _split3(psum)
    ov = ov_ref[...]
    imp = _dot(hi, ov) + _dot(mid, ov) + _dot(lo, ov)
    t = pos_base + qi * tq + lax.broadcasted_iota(jnp.int32, (tq, nsp), 0)
    jj = lax.broadcasted_iota(jnp.int32, (tq, nsp), 1)
    cur = _div_pow2(t, SLC_BLOCK)
    forced = (jj == 0) | (jj == cur) | (jj == cur - 1)
    causal = (jj * SLC_BLOCK <= t) & (jj < n_slc)
    imp = jnp.where(forced, jnp.inf, imp)
    imp = jnp.where(causal, imp, -jnp.inf)
    rank = jnp.zeros((tq, nsp), jnp.float32)
    for i in range(n_slc):
        col = imp[:, i:i + 1]
        beats = (col > imp) | ((col == imp) & (jj > i))
        rank = rank + beats.astype(jnp.float32)
    sel = (rank < SLC_TOPK) & (imp > -jnp.inf)
    sel_ref[0] = sel.astype(sel_ref.dtype)


def _overlap_matrix(nc, nsp):
    cs = jnp.arange(nc) * CMP_STRIDE
    ss = jnp.arange(nsp) * SLC_BLOCK
    ov = (cs[:, None] < ss[None, :] + SLC_BLOCK) & (cs[:, None] + CMP_BLOCK > ss[None, :])
    return ov.astype(jnp.bfloat16)


def _cmp_select(q_arr, row_blk0, bsz, nq, tq, kc, pos_base, n_cmp_valid, n_slc, nsp, sel_dtype):
    nc = kc.shape[3]
    rows = bsz * nq * tq
    kern = functools.partial(_cmp_select_kernel, tq=tq, pos_base=pos_base, n_cmp_valid=n_cmp_valid, n_slc=n_slc)
    return pl.pallas_call(
        kern,
        grid=(bsz, KV_HEADS, nq),
        in_specs=[
            pl.BlockSpec((tq, GROUP_COLS), lambda b, g, i: (row_blk0 + b * nq + i, g)),
            pl.BlockSpec((1, 1, 1, nc, HEAD_DIM), lambda b, g, i: (b, 0, g, 0, 0)),
            pl.BlockSpec((1, 1, 1, nc, HEAD_DIM), lambda b, g, i: (b, 1, g, 0, 0)),
            pl.BlockSpec((nc, nsp), lambda b, g, i: (0, 0)),
        ],
        out_specs=[
            pl.BlockSpec((tq, GROUP_COLS), lambda b, g, i: (b * nq + i, g)),
            pl.BlockSpec((1, tq, nsp), lambda b, g, i: (g, b * nq + i, 0)),
        ],
        out_shape=[jax.ShapeDtypeStruct((rows, NSA_HEADS * HEAD_DIM), jnp.float32),
                   jax.ShapeDtypeStruct((KV_HEADS, rows, nsp), sel_dtype)],
        compiler_params=_cparams(("parallel", "parallel", "parallel")),
        name="cmp_select",
    )(q_arr, kc, kc, _overlap_matrix(nc, nsp))


def _online_update(s, v, m_ref, l_ref, acc_ref, rows=slice(None)):
    m_prev = m_ref[rows]
    m_new = jnp.maximum(m_prev, jnp.max(s, axis=-1, keepdims=True))
    alpha = jnp.exp(m_prev - m_new)
    p = jnp.exp(s - m_new)
    l_ref[rows] = alpha * l_ref[rows] + jnp.sum(p, axis=-1, keepdims=True)
    acc_ref[rows] = alpha * acc_ref[rows] + _dot(p.astype(jnp.bfloat16), v)
    m_ref[rows] = m_new


def _flash_kernel(*refs, mode, tq, tk, nk):
    if mode == "slc":
        q_ref, k_ref, v_ref, sel_ref, o_ref, m_ref, l_ref, acc_ref = refs
    else:
        q_ref, k_ref, v_ref, cq_ref, ck_ref, o_ref, m_ref, l_ref, acc_ref = refs
    qi = pl.program_id(2)
    kk = pl.program_id(3)
    kidx = kk
    active = kk <= ((qi + 1) * tq - 1) // tk

    @pl.when(kk == 0)
    def _():
        m_ref[...] = jnp.full_like(m_ref, NEG)
        l_ref[...] = jnp.zeros_like(l_ref)
        acc_ref[...] = jnp.zeros_like(acc_ref)

    @pl.when(active)
    def _():
        q = _stack_heads(q_ref[...])
        s = _dot_nt(q, k_ref[...]) * ATTN_SCALE
        t = qi * tq + lax.broadcasted_iota(jnp.int32, (tq, tk), 0)
        kp = kidx * tk + lax.broadcasted_iota(jnp.int32, (tq, tk), 1)
        mask = kp <= t
        if mode == "slc":
            nsb = sel_ref.shape[2]
            jb = lax.broadcasted_iota(jnp.int32, (nsb, tk), 0)
            kb = kidx * (tk // SLC_BLOCK) + _div_pow2(lax.broadcasted_iota(jnp.int32, (nsb, tk), 1), SLC_BLOCK)
            expand = (jb == kb).astype(jnp.bfloat16)
            mask = mask & (_dot(sel_ref[0], expand) > 0.5)
        add = jnp.where(mask, 0.0, NEG)
        if mode == "fox":
            add = jnp.concatenate([(cq_ref[0][:, r:r + 1] - ck_ref[0][r:r + 1, :]) + add for r in range(REP)], axis=0)
        else:
            add = jnp.concatenate([add] * REP, axis=0)
        _online_update(s + add, v_ref[...], m_ref, l_ref, acc_ref)

    @pl.when(kk == nk - 1)
    def _():
        o_ref[...] = _unstack_heads(acc_ref[...] / l_ref[...], tq)


def _flash_prompt(mode, hb, b, t, q_col, k_col, v_col, sel=None, cq=None, ck=None):
    tq = _pick(t, (256, 128))
    tk = _pick(t, (512, 256, 128))
    nq = t // tq
    nkt = t // tk
    nk = nkt
    qb, kb, vb = q_col // GROUP_COLS, k_col // HEAD_DIM, v_col // HEAD_DIM

    def kidx(i, k):
        return jnp.minimum(k, ((i + 1) * tq - 1) // tk)

    in_specs = [
        pl.BlockSpec((tq, GROUP_COLS), lambda bi, g, i, k: (bi * nq + i, qb + g)),
        pl.BlockSpec((tk, HEAD_DIM), lambda bi, g, i, k: (bi * nkt + kidx(i, k), kb + g)),
        pl.BlockSpec((tk, HEAD_DIM), lambda bi, g, i, k: (bi * nkt + kidx(i, k), vb + g)),
    ]
    args = [hb, hb, hb]
    if mode == "slc":
        in_specs.append(pl.BlockSpec((1, tq, sel.shape[2]), lambda bi, g, i, k: (g, bi * nq + i, 0)))
        args.append(sel)
    if mode == "fox":
        in_specs.append(pl.BlockSpec((1, tq, REP), lambda bi, g, i, k: (g, bi * nq + i, 0)))
        in_specs.append(pl.BlockSpec((1, REP, tk), lambda bi, g, i, k: (g, 0, bi * nkt + kidx(i, k))))
        args += [cq, ck]
    kern = functools.partial(_flash_kernel, mode=mode, tq=tq, tk=tk, nk=nk)
    return pl.pallas_call(
        kern,
        grid=(b, KV_HEADS, nq, nk),
        in_specs=in_specs,
        out_specs=pl.BlockSpec((tq, GROUP_COLS), lambda bi, g, i, k: (bi * nq + i, g)),
        out_shape=jax.ShapeDtypeStruct((b * t, KV_HEADS * GROUP_COLS), jnp.float32),
        scratch_shapes=[pltpu.VMEM((REP * tq, 1), jnp.float32), pltpu.VMEM((REP * tq, 1), jnp.float32),
                        pltpu.VMEM((REP * tq, HEAD_DIM), jnp.float32)],
        compiler_params=_cparams(("parallel", "parallel", "parallel", "arbitrary")),
        name="flash_" + mode,
    )(*args)


def _win_prompt_kernel(q_ref, *refs, tq, n_kb):
    k_refs, v_refs, o_ref = refs[:n_kb], refs[n_kb:2 * n_kb], refs[2 * n_kb]
    qi = pl.program_id(2)
    span = n_kb * tq
    k = jnp.concatenate([r[...] for r in k_refs], axis=0)
    v = jnp.concatenate([r[...] for r in v_refs], axis=0)
    s = _dot_nt(_stack_heads(q_ref[...]), k) * ATTN_SCALE
    t = qi * tq + lax.broadcasted_iota(jnp.int32, (tq, span), 0)
    kp = (qi - (n_kb - 1)) * tq + lax.broadcasted_iota(jnp.int32, (tq, span), 1)
    mask = (kp >= 0) & (kp <= t) & (t - kp < WINDOW)
    s = s + jnp.concatenate([jnp.where(mask, 0.0, NEG)] * REP, axis=0)
    p = jnp.exp(s - jnp.max(s, axis=-1, keepdims=True))
    o = _dot(p.astype(jnp.bfloat16), v) / jnp.sum(p, axis=-1, keepdims=True)
    o_ref[...] = _unstack_heads(o, tq)


def _win_prompt(hb, b, t, q_col, k_col, v_col):
    tq = _pick(t, (256, 128))
    assert WINDOW % tq == 0
    n_kb = WINDOW // tq + 1
    nq = t // tq
    qb, kb, vb = q_col // GROUP_COLS, k_col // HEAD_DIM, v_col // HEAD_DIM

    def kv_spec(col_blk, j):
        return pl.BlockSpec((tq, HEAD_DIM),
                            lambda bi, g, i: (bi * nq + jnp.maximum(i - (n_kb - 1) + j, 0), col_blk + g))

    in_specs = ([pl.BlockSpec((tq, GROUP_COLS), lambda bi, g, i: (bi * nq + i, qb + g))]
                + [kv_spec(kb, j) for j in range(n_kb)] + [kv_spec(vb, j) for j in range(n_kb)])
    kern = functools.partial(_win_prompt_kernel, tq=tq, n_kb=n_kb)
    return pl.pallas_call(
        kern,
        grid=(b, KV_HEADS, nq),
        in_specs=in_specs,
        out_specs=pl.BlockSpec((tq, GROUP_COLS), lambda bi, g, i: (bi * nq + i, g)),
        out_shape=jax.ShapeDtypeStruct((b * t, KV_HEADS * GROUP_COLS), jnp.float32),
        compiler_params=_cparams(("parallel", "parallel", "parallel")),
        name="win_prompt",
    )(*([hb] * (1 + 2 * n_kb)))


def _rows_of(g, tn):
    return slice(g * REP * tn, (g + 1) * REP * tn)


def _sample_init(m_ref, l_ref, acc_ref):
    m_ref[...] = jnp.full_like(m_ref, NEG)
    l_ref[...] = jnp.zeros_like(l_ref)
    acc_ref[...] = jnp.zeros_like(acc_ref)


def _sample_finish(o_ref, l_ref, acc_ref, tn):
    res = acc_ref[...] / l_ref[...]
    for g in range(KV_HEADS):
        o_ref[:, g * GROUP_COLS:(g + 1) * GROUP_COLS] = _unstack_heads(res[_rows_of(g, tn)], tn)


def _online_update_groups(scores, adds, values, m_ref, l_ref, acc_ref, tn):
    s = jnp.concatenate(scores, axis=0) * ATTN_SCALE + jnp.concatenate(adds, axis=0)
    m_prev = m_ref[...]
    m_new = jnp.maximum(m_prev, jnp.max(s, axis=-1, keepdims=True))
    alpha = jnp.exp(m_prev - m_new)
    p = jnp.exp(s - m_new)
    l_ref[...] = alpha * l_ref[...] + jnp.sum(p, axis=-1, keepdims=True)
    pb = p.astype(jnp.bfloat16)
    pv = jnp.concatenate([_dot(pb[_rows_of(g, tn)], values[g]) for g in range(KV_HEADS)], axis=0)
    acc_ref[...] = alpha * acc_ref[...] + pv
    m_ref[...] = m_new


def _kv_of(page_refs, g):
    def pick(ref, slab):
        return ref[0, pl.ds(slab, ref.shape[1] // ROW_SLABS, stride=ROW_SLABS), :]
    k = jnp.concatenate([pick(p, g) for p in page_refs], axis=0)
    v = jnp.concatenate([pick(p, KV_HEADS + g) for p in page_refs], axis=0)
    return k.astype(jnp.bfloat16), v.astype(jnp.bfloat16)


def _new_rows_mask(tn, page):
    tok = lax.broadcasted_iota(jnp.int32, (tn, page), 0)
    r = lax.broadcasted_iota(jnp.int32, (tn, page), 1)
    return r <= tok


def _paged_attn_kernel(pt_ref, *refs, mode, n_steps, tn):
    pages = refs[:PAGES_PER_STEP]
    if mode == "slc":
        q_ref, new_ref, sel_ref, selnew_ref, o_ref, m_ref, l_ref, acc_ref = refs[PAGES_PER_STEP:]
    else:
        q_ref, new_ref, cq_ref, ck_ref, cknew_ref, o_ref, m_ref, l_ref, acc_ref = refs[PAGES_PER_STEP:]
    s_id = pl.program_id(1)
    page = pages[0].shape[1] // ROW_SLABS
    nkeys = PAGES_PER_STEP * page

    @pl.when(s_id == 0)
    def _():
        _sample_init(m_ref, l_ref, acc_ref)

    def bias_rows(g, ck, add):
        return jnp.concatenate(
            [(cq_ref[0][:, g * REP + r:g * REP + r + 1] - ck[g * REP + r:g * REP + r + 1, :]) + add
             for r in range(REP)], axis=0)

    if mode == "slc":
        key_blk = _div_pow2(lax.broadcasted_iota(jnp.int32, (tn, nkeys), 1), SLC_BLOCK)
    scores, values, adds = [], [], []
    for g in range(KV_HEADS):
        q = _stack_heads(q_ref[:, g * GROUP_COLS:(g + 1) * GROUP_COLS])
        k, v = _kv_of(pages, g)
        scores.append(_dot_nt(q, k))
        values.append(v)
        if mode == "slc":
            flags = sel_ref[0, 0, g]
            picked = jnp.zeros((tn, nkeys), jnp.float32)
            for jb in range(flags.shape[1]):
                picked = jnp.where(key_blk == jb, flags[:, jb:jb + 1], picked)
            adds += [jnp.where(picked > 0.5, 0.0, NEG)] * REP
        else:
            adds.append(bias_rows(g, ck_ref[0], 0.0))
    _online_update_groups(scores, adds, values, m_ref, l_ref, acc_ref, tn)

    @pl.when(s_id == n_steps - 1)
    def _():
        causal = _new_rows_mask(tn, page)
        scores, values, adds = [], [], []
        for g in range(KV_HEADS):
            q = _stack_heads(q_ref[:, g * GROUP_COLS:(g + 1) * GROUP_COLS])
            k, v = _kv_of([new_ref], g)
            scores.append(_dot_nt(q, k))
            values.append(v)
            if mode == "slc":
                adds += [jnp.where(causal & (selnew_ref[0, 0, g][:, 0:1] > 0.5), 0.0, NEG)] * REP
            else:
                adds.append(bias_rows(g, cknew_ref[0], jnp.where(causal, 0.0, NEG)))
        _online_update_groups(scores, adds, values, m_ref, l_ref, acc_ref, tn)
        _sample_finish(o_ref, l_ref, acc_ref, tn)


def _paged_attn(mode, page_table, pool, new_pad, q_s, tn, sel5=None, cq=None, ck=None):
    bsz, n_pages = page_table.shape
    _, slab_rows, cols = pool.shape
    page = slab_rows // ROW_SLABS
    n_steps = n_pages // PAGES_PER_STEP
    nkeys = PAGES_PER_STEP * page

    def page_spec(k):
        return pl.BlockSpec((1, slab_rows, cols), lambda b, s, pt: (pt[b, s * PAGES_PER_STEP + k], 0, 0))

    in_specs = [page_spec(k) for k in range(PAGES_PER_STEP)] + [
        pl.BlockSpec((tn, NSA_HEADS * HEAD_DIM), lambda b, s, pt: (b, 0)),
        pl.BlockSpec((1, slab_rows, cols), lambda b, s, pt: (b, 0, 0)),
    ]
    args = [pool] * PAGES_PER_STEP + [q_s, new_pad]
    if mode == "slc":
        nb = sel5.shape[4]
        in_specs += [pl.BlockSpec((1, 1, KV_HEADS, tn, nb), lambda b, s, pt: (b, s, 0, 0, 0)),
                     pl.BlockSpec((1, 1, KV_HEADS, tn, nb), lambda b, s, pt: (b, n_steps, 0, 0, 0))]
        args += [sel5, sel5]
    else:
        nh = cq.shape[2]
        in_specs += [pl.BlockSpec((1, tn, nh), lambda b, s, pt: (b, 0, 0)),
                     pl.BlockSpec((1, nh, nkeys), lambda b, s, pt: (b, 0, s)),
                     pl.BlockSpec((1, nh, page), lambda b, s, pt: (b, 0, n_pages))]
        args += [cq, ck, ck]
    rows = KV_HEADS * REP * tn
    kern = functools.partial(_paged_attn_kernel, mode=mode, n_steps=n_steps, tn=tn)
    return pl.pallas_call(
        kern,
        grid_spec=pltpu.PrefetchScalarGridSpec(
            num_scalar_prefetch=1,
            grid=(bsz, n_steps),
            in_specs=in_specs,
            out_specs=pl.BlockSpec((tn, NSA_HEADS * HEAD_DIM), lambda b, s, pt: (b, 0)),
            scratch_shapes=[pltpu.VMEM((rows, 1), jnp.float32), pltpu.VMEM((rows, 1), jnp.float32),
                            pltpu.VMEM((rows, HEAD_DIM), jnp.float32)],
        ),
        out_shape=jax.ShapeDtypeStruct((bsz * tn, NSA_HEADS * HEAD_DIM), jnp.float32),
        compiler_params=_cparams(("parallel", "arbitrary")),
        name="paged_" + mode,
    )(page_table, *args)


def _win_sample_kernel(q_ref, win_ref, new_ref, o_ref, m_ref, l_ref, acc_ref, *, tn, past):
    _sample_init(m_ref, l_ref, acc_ref)
    win_len = win_ref.shape[1] // ROW_SLABS
    page = new_ref.shape[1] // ROW_SLABS
    pos = past + lax.broadcasted_iota(jnp.int32, (tn, win_len), 0)
    kpos = past - win_len + lax.broadcasted_iota(jnp.int32, (tn, win_len), 1)
    wmask = (kpos <= pos) & (pos - kpos < WINDOW)
    add_w = [jnp.where(wmask, 0.0, NEG)] * (REP * KV_HEADS)
    add_n = [jnp.where(_new_rows_mask(tn, page), 0.0, NEG)] * (REP * KV_HEADS)
    for src, adds in ((new_ref, add_n), (win_ref, add_w)):
        scores, values = [], []
        for g in range(KV_HEADS):
            q = _stack_heads(q_ref[:, g * GROUP_COLS:(g + 1) * GROUP_COLS])
            k, v = _kv_of([src], g)
            scores.append(_dot_nt(q, k))
            values.append(v)
        _online_update_groups(scores, adds, values, m_ref, l_ref, acc_ref, tn)
    _sample_finish(o_ref, l_ref, acc_ref, tn)


def _win_sample(win_buf, new_pad, q_s, tn, past):
    bsz, win_rows, cols = win_buf.shape
    new_rows = new_pad.shape[1]
    rows = KV_HEADS * REP * tn
    kern = functools.partial(_win_sample_kernel, tn=tn, past=past)
    return pl.pallas_call(
        kern,
        grid=(bsz,),
        in_specs=[pl.BlockSpec((tn, NSA_HEADS * HEAD_DIM), lambda b: (b, 0)),
                  pl.BlockSpec((1, win_rows, cols), lambda b: (b, 0, 0)),
                  pl.BlockSpec((1, new_rows, cols), lambda b: (b, 0, 0))],
        out_specs=pl.BlockSpec((tn, NSA_HEADS * HEAD_DIM), lambda b: (b, 0)),
        out_shape=jax.ShapeDtypeStruct((bsz * tn, NSA_HEADS * HEAD_DIM), jnp.float32),
        scratch_shapes=[pltpu.VMEM((rows, 1), jnp.float32), pltpu.VMEM((rows, 1), jnp.float32),
                        pltpu.VMEM((rows, HEAD_DIM), jnp.float32)],
        compiler_params=_cparams(("parallel",)),
        name="win_sample",
    )(q_s, win_buf, new_pad)


def _merge_kernel(cmp_ref, slc_ref, win_ref, fox_ref, gate_ref, gn_ref, gf_ref, o_ref):
    gates = gate_ref[...]
    parts = []
    for h in range(NSA_HEADS):
        cols = slice(h * HEAD_DIM, (h + 1) * HEAD_DIM)
        c = N_BRANCH * h
        parts.append(gates[:, c:c + 1] * cmp_ref[:, cols] + gates[:, c + 1:c + 2] * slc_ref[:, cols]
                     + gates[:, c + 2:c + 3] * win_ref[:, cols])
    nsa = jnp.concatenate(parts, axis=1)
    nsa = nsa * lax.rsqrt(jnp.mean(nsa * nsa, axis=-1, keepdims=True) + RMS_EPS) * gn_ref[...]
    fox = fox_ref[...]
    fox = fox * lax.rsqrt(jnp.mean(fox * fox, axis=-1, keepdims=True) + RMS_EPS) * gf_ref[...]
    o_ref[...] = jnp.concatenate([nsa, fox], axis=1).astype(o_ref.dtype)


def _merge(o_cmp, o_slc, o_win, o_fox, slab, g_nsa, g_fox):
    n, w = o_cmp.shape
    tm = _pick(n, (256, 128, 64, 32, 16, 8))
    row = lambda i: (i, 0)
    fix = lambda i: (0, 0)
    return pl.pallas_call(
        _merge_kernel,
        grid=(n // tm,),
        in_specs=[pl.BlockSpec((tm, w), row)] * 4 + [pl.BlockSpec((tm, LANES), row), pl.BlockSpec((1, w), fix),
                                                      pl.BlockSpec((1, w), fix)],
        out_specs=pl.BlockSpec((tm, 2 * w), row),
        out_shape=jax.ShapeDtypeStruct((n, 2 * w), jnp.bfloat16),
        compiler_params=_cparams(("parallel",)),
        name="merge_heads",
    )(o_cmp, o_slc, o_win, o_fox, slab, g_nsa.reshape(1, w), g_fox.reshape(1, w))


def _outproj_kernel(a_ref, w_ref, x_ref, o_ref):
    o_ref[...] = x_ref[...] + _dot(a_ref[...], w_ref[...])


def _outproj(a, w, x):
    n, k = a.shape
    d = w.shape[1]
    tm = _pick(n, (768, 512, 384, 256, 128, 64, 32, 16, 8))
    tn = 512
    return pl.pallas_call(
        _outproj_kernel,
        grid=(n // tm, d // tn),
        in_specs=[pl.BlockSpec((tm, k), lambda i, j: (i, 0)), pl.BlockSpec((k, tn), lambda i, j: (0, j)),
                  pl.BlockSpec((tm, tn), lambda i, j: (i, j))],
        out_specs=pl.BlockSpec((tm, tn), lambda i, j: (i, j)),
        out_shape=jax.ShapeDtypeStruct((n, d), jnp.float32),
        compiler_params=_cparams(("parallel", "arbitrary")),
        name="outproj",
    )(a, w, x)


def _router_kernel(h_ref, g_ref, w_ref, b_ref, xn_ref, eid_ref, gate_ref):
    x = h_ref[...]
    xn = x * lax.rsqrt(jnp.mean(x * x, axis=-1, keepdims=True) + RMS_EPS) * g_ref[...]
    xn_ref[...] = xn.astype(xn_ref.dtype)
    lg = jnp.dot(xn, w_ref[...], precision=lax.Precision.HIGHEST, preferred_element_type=jnp.float32) + b_ref[...]
    lane = lax.broadcasted_iota(jnp.int32, lg.shape, 1)
    lane_f = lane.astype(jnp.float32)
    ninf = -jnp.inf
    is_grp = lane < N_GROUPS
    gl = jnp.where(is_grp, lg, ninf)
    gmax = jnp.max(gl, axis=-1, keepdims=True)
    gsel = jnp.min(jnp.where(gl == gmax, lane_f, float(LANES)), axis=-1, keepdims=True)
    p_sel = 1.0 / jnp.sum(jnp.where(is_grp, jnp.exp(lg - gmax), 0.0), axis=-1, keepdims=True)
    e_lane = lane - N_GROUPS
    e_grp = _div_pow2(e_lane, EXPERTS_PER_GROUP).astype(jnp.float32)
    in_grp = (e_lane >= 0) & (e_lane < N_EXPERTS) & (e_grp == gsel)
    el = jnp.where(in_grp, lg, ninf)
    v1 = jnp.max(el, axis=-1, keepdims=True)
    i1 = jnp.min(jnp.where(el == v1, lane_f, float(LANES)), axis=-1, keepdims=True)
    el2 = jnp.where(lane_f == i1, ninf, el)
    v2 = jnp.max(el2, axis=-1, keepdims=True)
    i2 = jnp.min(jnp.where(el2 == v2, lane_f, float(LANES)), axis=-1, keepdims=True)
    e2 = jnp.exp(v2 - v1)
    den = 1.0 + e2
    g1 = p_sel * (1.0 / den)
    g2 = p_sel * (e2 / den)
    eid = jnp.where(lane == 0, i1 - N_GROUPS, jnp.where(lane == 1, i2 - N_GROUPS, 0.0))
    eid_ref[...] = eid.astype(jnp.int32)
    gate_ref[...] = jnp.where(lane == 0, g1, jnp.where(lane == 1, g2, 0.0))


def _router(h1, g_ffn, w_r, b_r):
    n, d = h1.shape
    tm = _pick(n, (256, 128, 64, 32, 16, 8))
    row = lambda i: (i, 0)
    fix = lambda i: (0, 0)
    return pl.pallas_call(
        _router_kernel,
        grid=(n // tm,),
        in_specs=[pl.BlockSpec((tm, d), row), pl.BlockSpec((1, d), fix), pl.BlockSpec((d, LANES), fix),
                  pl.BlockSpec((1, LANES), fix)],
        out_specs=[pl.BlockSpec((tm, d), row), pl.BlockSpec((tm, LANES), row), pl.BlockSpec((tm, LANES), row)],
        out_shape=[jax.ShapeDtypeStruct((n, d), jnp.float32), jax.ShapeDtypeStruct((n, LANES), jnp.int32),
                   jax.ShapeDtypeStruct((n, LANES), jnp.float32)],
        compiler_params=_cparams(("parallel",)),
        name="router",
    )(h1, g_ffn.reshape(1, d), w_r, b_r)


SC_CORES = 2
SC_SUBCORES = 16
SC_WORKERS = SC_CORES * SC_SUBCORES
SC_GATHER_ROWS = 16


def _gather_rows(idx, table):
    n = idx.shape[0]
    _, d = table.shape
    assert n % (SC_WORKERS * SC_GATHER_ROWS) == 0, n
    per_worker = n // SC_WORKERS
    n_chunks = per_worker // SC_GATHER_ROWS
    mesh = plsc.VectorSubcoreMesh(core_axis_name="c", subcore_axis_name="s")

    @functools.partial(
        pl.kernel, mesh=mesh,
        out_type=jax.ShapeDtypeStruct((n, d), table.dtype),
        scratch_types=[pltpu.VMEM((SC_GATHER_ROWS,), jnp.int32),
                       pltpu.VMEM((SC_GATHER_ROWS, d), table.dtype),
                       pltpu.SemaphoreType.DMA],
    )
    def gather(table_hbm, idx_hbm, out_hbm, idx_v, rows_v, sem):
        wid = lax.axis_index("s") * SC_CORES + lax.axis_index("c")
        base = wid * per_worker

        @pl.loop(0, n_chunks)
        def _(i):
            off = pl.multiple_of(base + i * SC_GATHER_ROWS, SC_GATHER_ROWS)
            pltpu.sync_copy(idx_hbm.at[pl.ds(off, SC_GATHER_ROWS)], idx_v)
            pltpu.async_copy(table_hbm.at[idx_v], rows_v, sem).wait()
            pltpu.sync_copy(rows_v, out_hbm.at[pl.ds(off, SC_GATHER_ROWS)])

    return gather(table, idx)


MOE_TM = 256
MOE_FT = 256
MOE_NT = 1024


def _gateup_kernel(tb_ref, e_ref, f_ref, first_ref, valid_ref, x_ref, wg_ref, wu_ref, h_ref, wgb_ref, wub_ref):
    s = pl.program_id(0)

    @pl.when(first_ref[s] == 1)
    def _():
        wgb_ref[...] = wg_ref[0].astype(jnp.bfloat16)
        wub_ref[...] = wu_ref[0].astype(jnp.bfloat16)

    @pl.when(valid_ref[s] == 1)
    def _():
        x = x_ref[...].astype(jnp.bfloat16)
        a = _dot(x, wgb_ref[...])
        u = _dot(x, wub_ref[...])
        h_ref[...] = (a * (1.0 / (1.0 + jnp.exp(-a))) * u).astype(h_ref.dtype)


def _down_kernel(tb_ref, e_ref, f_ref, first_ref, valid_ref, h_ref, wd_ref, y_ref, wdb_ref):
    s = pl.program_id(0)

    @pl.when(first_ref[s] == 1)
    def _():
        wdb_ref[...] = wd_ref[0].astype(jnp.bfloat16)

    @pl.when(valid_ref[s] == 1)
    def _():
        y_ref[...] = _dot(h_ref[...], wdb_ref[...])


def _work_list(block_e, start_blk, n_blk, total_blocks, n_tiles, n_tb_max):
    n_steps = n_tb_max * n_tiles
    s = jnp.minimum(jnp.arange(n_steps, dtype=jnp.int32), total_blocks * n_tiles - 1)
    valid = (jnp.arange(n_steps, dtype=jnp.int32) < total_blocks * n_tiles).astype(jnp.int32)
    e = block_e[s // n_tiles]
    local = s - n_tiles * start_blk[e]
    nb = jnp.maximum(n_blk[e], 1)
    tile = local // nb
    within = local % nb
    tb = start_blk[e] + within
    first = ((within == 0) & (valid == 1)).astype(jnp.int32)
    return tb.astype(jnp.int32), e.astype(jnp.int32), tile.astype(jnp.int32), first, valid


def _experts(xs, w_gate, w_up, w_down, block_e, start_blk, n_blk, total_blocks):
    a_pad, d = xs.shape
    n_tb_max = a_pad // MOE_TM
    d_exp = w_gate.shape[2]
    wl1 = _work_list(block_e, start_blk, n_blk, total_blocks, d_exp // MOE_FT, n_tb_max)
    hidden = pl.pallas_call(
        _gateup_kernel,
        grid_spec=pltpu.PrefetchScalarGridSpec(
            num_scalar_prefetch=5,
            grid=(wl1[0].shape[0],),
            in_specs=[
                pl.BlockSpec((MOE_TM, d), lambda s, tb, e, f, fi, va: (tb[s], 0)),
                pl.BlockSpec((1, d, MOE_FT), lambda s, tb, e, f, fi, va: (e[s], 0, f[s])),
                pl.BlockSpec((1, d, MOE_FT), lambda s, tb, e, f, fi, va: (e[s], 0, f[s])),
            ],
            out_specs=pl.BlockSpec((MOE_TM, MOE_FT), lambda s, tb, e, f, fi, va: (tb[s], f[s])),
            scratch_shapes=[pltpu.VMEM((d, MOE_FT), jnp.bfloat16), pltpu.VMEM((d, MOE_FT), jnp.bfloat16)],
        ),
        out_shape=jax.ShapeDtypeStruct((a_pad, d_exp), jnp.bfloat16),
        compiler_params=_cparams(("arbitrary",)),
        name="expert_gate_up",
    )(*wl1, xs, w_gate, w_up)
    wl2 = _work_list(block_e, start_blk, n_blk, total_blocks, d // MOE_NT, n_tb_max)
    return pl.pallas_call(
        _down_kernel,
        grid_spec=pltpu.PrefetchScalarGridSpec(
            num_scalar_prefetch=5,
            grid=(wl2[0].shape[0],),
            in_specs=[
                pl.BlockSpec((MOE_TM, d_exp), lambda s, tb, e, f, fi, va: (tb[s], 0)),
                pl.BlockSpec((1, d_exp, MOE_NT), lambda s, tb, e, f, fi, va: (e[s], 0, f[s])),
            ],
            out_specs=pl.BlockSpec((MOE_TM, MOE_NT), lambda s, tb, e, f, fi, va: (tb[s], f[s])),
            scratch_shapes=[pltpu.VMEM((d_exp, MOE_NT), jnp.bfloat16)],
        ),
        out_shape=jax.ShapeDtypeStruct((a_pad, d), jnp.float32),
        compiler_params=_cparams(("arbitrary",)),
        name="expert_down",
    )(*wl2, hidden, w_down)


def _combine_kernel(h_ref, y0_ref, y1_ref, gate_ref, o_ref):
    g = gate_ref[...]
    o_ref[...] = h_ref[...] + (g[:, 0:1] * y0_ref[0] + g[:, 1:2] * y1_ref[0])


def _combine(h1, y2, gate_slab):
    n, d = h1.shape
    tm = _pick(n, (256, 128, 64, 32, 16, 8))
    return pl.pallas_call(
        _combine_kernel,
        grid=(n // tm,),
        in_specs=[pl.BlockSpec((tm, d), lambda i: (i, 0)), pl.BlockSpec((1, tm, d), lambda i: (0, i, 0)),
                  pl.BlockSpec((1, tm, d), lambda i: (1, i, 0)), pl.BlockSpec((tm, LANES), lambda i: (i, 0))],
        out_specs=pl.BlockSpec((tm, d), lambda i: (i, 0)),
        out_shape=jax.ShapeDtypeStruct((n, d), jnp.float32),
        compiler_params=_cparams(("parallel",)),
        name="moe_combine",
    )(h1, y2, y2, gate_slab)


def _moe(h1, g_ffn, w_rg, b_rg, w_re, b_re, w_gate, w_up, w_down):
    n, d = h1.shape
    w_r = jnp.zeros((d, LANES), jnp.float32).at[:, :N_GROUPS].set(w_rg).at[:, N_GROUPS:N_GROUPS + N_EXPERTS].set(w_re)
    b_r = jnp.zeros((1, LANES), jnp.float32).at[0, :N_GROUPS].set(b_rg).at[0, N_GROUPS:N_GROUPS + N_EXPERTS].set(b_re)
    xn, eid_slab, gate_slab = _router(h1, g_ffn, w_r, b_r)
    flat = eid_slab[:, :EXPERT_TOPK].reshape(n * EXPERT_TOPK)
    onehot = (flat[:, None] == jnp.arange(N_EXPERTS, dtype=jnp.int32)[None, :]).astype(jnp.int32)
    rank = jnp.take_along_axis(jnp.cumsum(onehot, axis=0) - onehot, flat[:, None], axis=1)[:, 0]
    counts = jnp.sum(onehot, axis=0)
    n_blk = (counts + MOE_TM - 1) // MOE_TM
    start_blk = jnp.cumsum(n_blk) - n_blk
    total_blocks = jnp.sum(n_blk)
    slot = start_blk[flat] * MOE_TM + rank
    n_tb_max = -(-(n * EXPERT_TOPK) // MOE_TM) + N_EXPERTS
    a_pad = n_tb_max * MOE_TM
    src_tok = jnp.zeros((a_pad,), jnp.int32).at[slot].set(jnp.arange(n * EXPERT_TOPK, dtype=jnp.int32) // EXPERT_TOPK)
    end_blk = jnp.cumsum(n_blk)
    owner = jnp.sum((end_blk[None, :] <= jnp.arange(n_tb_max, dtype=jnp.int32)[:, None]).astype(jnp.int32), axis=1)
    block_e = jnp.minimum(owner, N_EXPERTS - 1).astype(jnp.int32)
    xs = _gather_rows(src_tok, xn)
    yb = _experts(xs, w_gate, w_up, w_down, block_e, start_blk.astype(jnp.int32), n_blk.astype(jnp.int32),
                  total_blocks.astype(jnp.int32))
    back = slot.reshape(n, EXPERT_TOPK).T.reshape(n * EXPERT_TOPK).astype(jnp.int32)
    y2 = _gather_rows(back, yb).reshape(EXPERT_TOPK, n, d)
    return _combine(h1, y2, gate_slab)


def _rope_tables(pos):
    half = HEAD_DIM // 2
    inv = jnp.exp(-math.log(ROPE_THETA) * jnp.arange(half, dtype=jnp.float32) * (2.0 / HEAD_DIM))
    ang = pos.astype(jnp.float32)[:, None] * inv[None, :]
    cos, sin = jnp.cos(ang), jnp.sin(ang)
    return jnp.concatenate([cos, cos], axis=1), jnp.concatenate([-sin, sin], axis=1)


def _repack_weights(w_in, b_gate, b_forget, g_nsa_q, g_nsa_k, g_fox_q, g_fox_k):
    o1 = NSA_HEADS * HEAD_DIM
    o2 = o1 + N_BRANCH * KV_COLS
    o3 = o2 + N_GATE
    o4 = o3 + FOX_HEADS * HEAD_DIM
    o5 = o4 + KV_COLS
    d = w_in.shape[0]
    w_main = jnp.concatenate([w_in[:, :o2], w_in[:, o3:o5]], axis=1).astype(jnp.bfloat16)
    n_f = w_in.shape[1] - o5
    w_small = jnp.zeros((d, LANES), jnp.float32).at[:, :N_GATE].set(w_in[:, o2:o3])
    w_small = w_small.at[:, COL_LOGF:COL_LOGF + n_f].set(w_in[:, o5:]).astype(jnp.bfloat16)
    b_small = jnp.zeros((1, LANES), jnp.float32).at[0, :N_GATE].set(b_gate.reshape(N_GATE))
    b_small = b_small.at[0, COL_LOGF:COL_LOGF + n_f].set(b_forget)
    ones = jnp.ones((KV_HEADS * HEAD_DIM,), jnp.float32)
    gains = [jnp.tile(g_nsa_q, NSA_HEADS)]
    for br in range(N_BRANCH):
        gains += [jnp.tile(g_nsa_k[br], KV_HEADS), ones]
    gains += [jnp.tile(g_fox_q, FOX_HEADS), jnp.tile(g_fox_k, KV_HEADS), ones]
    return w_main, w_small, b_small, jnp.concatenate(gains).reshape(1, MAIN_COLS)


def _cmp_weights(w_k, w_v, pe_k, pe_v):
    def cat(w):
        return jnp.concatenate([w[:CMP_STRIDE], w[CMP_STRIDE:]], axis=2)
    return jnp.stack([cat(w_k), cat(w_v)]).astype(jnp.bfloat16), jnp.stack([pe_k, pe_v])


def _pad_rows(x, rows):
    return jnp.pad(x, ((0, 0), (0, rows - x.shape[1]), (0, 0)))


def kernel(x_prompt, x_sample, cache_nsa_cmp_kv, cache_nsa_slc_kv, cache_nsa_win_kv, cache_fox_kv, cache_fox_logf, page_table, g_attn_norm, w_in, b_nsa_gate, b_fox_forget, g_nsa_q, g_nsa_k, g_fox_q, g_fox_k, w_cmp_k, w_cmp_v, pe_cmp_k, pe_cmp_v, g_out_nsa, g_out_fox, w_out, g_ffn_norm, w_router_grp, b_router_grp, w_router_exp, b_router_exp, w_exp_gate, w_exp_up, w_exp_down):
    depth = w_in.shape[0]
    assert depth == 1, "single-layer step"
    bp, t, d = x_prompt.shape
    bs, tn, _ = x_sample.shape
    n_pages = page_table.shape[1]
    page = cache_fox_kv.shape[2]
    past = n_pages * page
    n_p, n_s = bp * t, bs * tn
    l = 0

    x_all = jnp.concatenate([x_prompt.reshape(n_p, d), x_sample.reshape(n_s, d)], axis=0)
    pos_all = jnp.concatenate([jnp.tile(jnp.arange(t, dtype=jnp.int32), bp),
                               jnp.tile(past + jnp.arange(tn, dtype=jnp.int32), bs)])
    cos_f, sin_s = _rope_tables(pos_all)
    w_main, w_small, b_small, gain_cols = _repack_weights(
        w_in[l], b_nsa_gate[l], b_fox_forget[l], g_nsa_q[l], g_nsa_k[l], g_fox_q[l], g_fox_k[l])

    xn = _rmsnorm(x_all, g_attn_norm[l], jnp.bfloat16)
    hf, hb = _inproj(xn, w_main, gain_cols, cos_f, sin_s)
    slab = _small_proj(xn, w_small, b_small)

    col_cmp, col_slc, col_win = (COL_KV_NSA + br * KV_COLS for br in range(N_BRANCH))
    w_cat, pe_cat = _cmp_weights(w_cmp_k[l], w_cmp_v[l], pe_cmp_k[l], pe_cmp_v[l])
    v_off = KV_HEADS * HEAD_DIM

    tq = _pick(t, (256, 128))
    kc_p = _compress_prompt(hf, bp, t, col_cmp, w_cat, pe_cat)
    n_slc_p = max(-(-t // SLC_BLOCK), SLC_TOPK)
    nsp_p = -(-n_slc_p // LANES) * LANES if n_slc_p > 64 else 64
    o_cmp_p, sel_p = _cmp_select(hb, 0, bp, t // tq, tq, kc_p, 0, t // CMP_STRIDE - CMP_RATIO + 1, n_slc_p, nsp_p,
                                 jnp.bfloat16)
    o_slc_p = _flash_prompt("slc", hb, bp, t, COL_Q_NSA, col_slc, col_slc + v_off, sel=sel_p)
    o_win_p = _win_prompt(hb, bp, t, COL_Q_NSA, col_win, col_win + v_off)
    c_p = _cumsum_prompt(slab[:n_p], bp, t)[:, COL_LOGF:COL_LOGF + FOX_HEADS]
    cq_p = c_p.reshape(n_p, KV_HEADS, REP).transpose(1, 0, 2)
    ck_p = c_p.T.reshape(KV_HEADS, REP, n_p)
    o_fox_p = _flash_prompt("fox", hb, bp, t, COL_Q_FOX, COL_KV_FOX, COL_KV_FOX + v_off, cq=cq_p, ck=ck_p)
    a_p = _merge(o_cmp_p, o_slc_p, o_win_p, o_fox_p, slab[:n_p], g_out_nsa[l], g_out_fox[l])

    hf_s, slab_s = hf[n_p:], slab[n_p:]
    q_nsa_s = hf_s[:, COL_Q_NSA:COL_Q_NSA + NSA_HEADS * HEAD_DIM]
    q_fox_s = hf_s[:, COL_Q_FOX:COL_Q_FOX + FOX_HEADS * HEAD_DIM]

    def new_rows(col):
        rows = _pad_rows(hf_s[:, col:col + KV_COLS].reshape(bs, tn, KV_COLS), page)
        return rows.reshape(bs, page * ROW_SLABS, HEAD_DIM)

    pool = lambda c: c[l].reshape(c.shape[1], page * ROW_SLABS, HEAD_DIM)
    kc_s = _compress_sample(page_table, pool(cache_nsa_cmp_kv), new_rows(col_cmp), w_cat, pe_cat)
    t_ctx = past + tn
    n_cmp_s = -(-t_ctx // CMP_STRIDE) - CMP_RATIO + 1
    n_slc_s = max(-(-t_ctx // SLC_BLOCK), SLC_TOPK)
    blocks_per_step = PAGES_PER_STEP * page // SLC_BLOCK
    nsp_s = -(-(n_slc_s + 1) // LANES) * LANES
    nsp_s = -(-nsp_s // blocks_per_step) * blocks_per_step
    o_cmp_s, sel_s = _cmp_select(q_nsa_s, 0, bs, 1, tn, kc_s, past, n_cmp_s, n_slc_s, nsp_s, jnp.float32)
    sel5 = sel_s.reshape(KV_HEADS, bs, tn, nsp_s // blocks_per_step, blocks_per_step).transpose(1, 3, 0, 2, 4)
    o_slc_s = _paged_attn("slc", page_table, pool(cache_nsa_slc_kv), new_rows(col_slc), q_nsa_s, tn, sel5=sel5)
    win_buf = cache_nsa_win_kv[l].reshape(bs, -1, HEAD_DIM)
    o_win_s = _win_sample(win_buf, new_rows(col_win), q_nsa_s, tn, past)
    logf_s = slab_s[:, COL_LOGF:COL_LOGF + FOX_HEADS].reshape(bs, tn, FOX_HEADS)
    ck_s = _cumsum_sample(page_table, cache_fox_logf[l].transpose(0, 2, 1),
                          _pad_rows(logf_s, page).transpose(0, 2, 1))
    cq_s = ck_s[:, :, past:past + tn].transpose(0, 2, 1)
    o_fox_s = _paged_attn("fox", page_table, pool(cache_fox_kv), new_rows(COL_KV_FOX), q_fox_s, tn, cq=cq_s, ck=ck_s)
    a_s = _merge(o_cmp_s, o_slc_s, o_win_s, o_fox_s, slab_s, g_out_nsa[l], g_out_fox[l])

    h1 = _outproj(jnp.concatenate([a_p, a_s], axis=0), w_out[l].astype(jnp.bfloat16), x_all)
    y = _moe(h1, g_ffn_norm[l], w_router_grp[l], b_router_grp[l], w_router_exp[l], b_router_exp[l],
             w_exp_gate[l], w_exp_up[l], w_exp_down[l])

    def kv_out(rows, col, bsz, tt):
        return rows[:, col:col + KV_COLS].reshape(1, bsz, tt, 2, KV_HEADS, HEAD_DIM)

    hf_p = hf[:n_p]
    win_p = kv_out(hf_p, col_win, bp, t)[:, :, t - min(WINDOW, t):]
    win_new = kv_out(hf_s, col_win, bs, tn)
    win_s = jnp.concatenate([cache_nsa_win_kv[l:l + 1][:, :, tn:], win_new], axis=2)
    logf_p = slab[:n_p, COL_LOGF:COL_LOGF + FOX_HEADS].reshape(1, bp, t, FOX_HEADS)
    return (y[:n_p].reshape(bp, t, d), y[n_p:].reshape(bs, tn, d),
            kv_out(hf_p, col_cmp, bp, t), kv_out(hf_s, col_cmp, bs, tn),
            kv_out(hf_p, col_slc, bp, t), kv_out(hf_s, col_slc, bs, tn),
            win_p, win_s,
            kv_out(hf_p, COL_KV_FOX, bp, t), kv_out(hf_s, COL_KV_FOX, bs, tn),
            logf_p, logf_s.reshape(1, bs, tn, FOX_HEADS))
```

```python
import functools
import math

import jax
import jax.numpy as jnp
from jax import lax
from jax.experimental import pallas as pl
from jax.experimental.pallas import tpu as pltpu
from jax.experimental.pallas import tpu_sc as plsc

HEAD_DIM = 128
NSA_HEADS = 16
FOX_HEADS = 16
KV_HEADS = 4
REP = NSA_HEADS // KV_HEADS
N_BRANCH = 3
CMP_BLOCK = 32
CMP_STRIDE = 16
CMP_RATIO = CMP_BLOCK // CMP_STRIDE
SLC_BLOCK = 64
SLC_TOPK = 16
WINDOW = 512
ROPE_THETA = 10000.0
RMS_EPS = 1e-6
N_GROUPS = 4
EXPERTS_PER_GROUP = 8
N_EXPERTS = N_GROUPS * EXPERTS_PER_GROUP
EXPERT_TOPK = 2
ATTN_SCALE = HEAD_DIM ** -0.5

LANES = 128
GROUP_COLS = REP * HEAD_DIM
KV_COLS = 2 * KV_HEADS * HEAD_DIM
ROW_SLABS = 2 * KV_HEADS
NEG = -1e30
VMEM_LIMIT = 48 * 1024 * 1024
PAGES_PER_STEP = 8

COL_Q_NSA = 0
COL_KV_NSA = NSA_HEADS * HEAD_DIM
COL_Q_FOX = COL_KV_NSA + N_BRANCH * KV_COLS
COL_KV_FOX = COL_Q_FOX + FOX_HEADS * HEAD_DIM
MAIN_COLS = COL_KV_FOX + KV_COLS
N_GATE = N_BRANCH * NSA_HEADS
COL_LOGF = N_GATE


def _pick(n, cands):
    for c in cands:
        if n % c == 0:
            return c
    raise ValueError(f"no tile in {cands} divides {n}")


def _cparams(sem, vmem=VMEM_LIMIT):
    return pltpu.CompilerParams(dimension_semantics=sem, vmem_limit_bytes=vmem)


def _split3(x):
    hi = x.astype(jnp.bfloat16)
    r1 = x - hi.astype(jnp.float32)
    mid = r1.astype(jnp.bfloat16)
    lo = (r1 - mid.astype(jnp.float32)).astype(jnp.bfloat16)
    return hi, mid, lo


def _dot(a, b):
    return jnp.dot(a, b, preferred_element_type=jnp.float32)


def _dot_nt(a, b):
    return lax.dot_general(a, b, (((1,), (1,)), ((), ())), preferred_element_type=jnp.float32)


def _rms_kernel(x_ref, g_ref, o_ref):
    x = x_ref[...]
    ms = jnp.mean(x * x, axis=-1, keepdims=True)
    o_ref[...] = (x * lax.rsqrt(ms + RMS_EPS) * g_ref[...]).astype(o_ref.dtype)


def _rmsnorm(x, g, out_dtype):
    n, d = x.shape
    tm = _pick(n, (256, 128, 64, 32, 16, 8))
    return pl.pallas_call(
        _rms_kernel,
        grid=(n // tm,),
        in_specs=[pl.BlockSpec((tm, d), lambda i: (i, 0)), pl.BlockSpec((1, d), lambda i: (0, 0))],
        out_specs=pl.BlockSpec((tm, d), lambda i: (i, 0)),
        out_shape=jax.ShapeDtypeStruct((n, d), out_dtype),
        compiler_params=_cparams(("parallel",)),
        name="rmsnorm",
    )(x, g.reshape(1, d))


N_COLBLK = MAIN_COLS // GROUP_COLS


def _colblock_kinds():
    kinds = []
    kinds += ["rope"] * (NSA_HEADS // REP)
    for _ in range(N_BRANCH):
        kinds += ["rope", "id"]
    kinds += ["norm"] * (FOX_HEADS // REP)
    kinds += ["norm", "id"]
    assert len(kinds) == N_COLBLK
    return kinds


def _any_eq(j, vals):
    return functools.reduce(jnp.logical_or, [j == v for v in vals])


def _inproj_kernel(x_ref, w_ref, gain_ref, cos_ref, sin_ref, of_ref, ob_ref):
    j = pl.program_id(1)
    kinds = _colblock_kinds()
    id_blocks = [i for i, k in enumerate(kinds) if k == "id"]
    rope_blocks = [i for i, k in enumerate(kinds) if k == "rope"]
    norm_blocks = [i for i, k in enumerate(kinds) if k == "norm"]
    acc = _dot(x_ref[...], w_ref[...])

    def write(y):
        of_ref[...] = y
        ob_ref[...] = y.astype(ob_ref.dtype)

    def normed(rope):
        outs = []
        for s in range(GROUP_COLS // HEAD_DIM):
            h = acc[:, s * HEAD_DIM:(s + 1) * HEAD_DIM]
            g = gain_ref[:, s * HEAD_DIM:(s + 1) * HEAD_DIM]
            y = h * lax.rsqrt(jnp.mean(h * h, axis=-1, keepdims=True) + RMS_EPS) * g
            if rope:
                y = y * cos_ref[...] + pltpu.roll(y, HEAD_DIM // 2, 1) * sin_ref[...]
            outs.append(y)
        return jnp.concatenate(outs, axis=1)

    @pl.when(_any_eq(j, id_blocks))
    def _():
        write(acc)

    @pl.when(_any_eq(j, norm_blocks))
    def _():
        write(normed(False))

    @pl.when(_any_eq(j, rope_blocks))
    def _():
        write(normed(True))


def _inproj(xn, w_main, gain_cols, cos_f, sin_s):
    n, d = xn.shape
    tm = _pick(n, (768, 512, 384, 256, 128, 64, 32, 16, 8))
    tn = GROUP_COLS
    return pl.pallas_call(
        _inproj_kernel,
        grid=(n // tm, MAIN_COLS // tn),
        in_specs=[
            pl.BlockSpec((tm, d), lambda i, j: (i, 0)),
            pl.BlockSpec((d, tn), lambda i, j: (0, j)),
            pl.BlockSpec((1, tn), lambda i, j: (0, j)),
            pl.BlockSpec((tm, HEAD_DIM), lambda i, j: (i, 0)),
            pl.BlockSpec((tm, HEAD_DIM), lambda i, j: (i, 0)),
        ],
        out_specs=[pl.BlockSpec((tm, tn), lambda i, j: (i, j)), pl.BlockSpec((tm, tn), lambda i, j: (i, j))],
        out_shape=[jax.ShapeDtypeStruct((n, MAIN_COLS), jnp.float32),
                   jax.ShapeDtypeStruct((n, MAIN_COLS), jnp.bfloat16)],
        compiler_params=_cparams(("parallel", "arbitrary")),
        name="inproj",
    )(xn, w_main, gain_cols, cos_f, sin_s)


def _small_kernel(x_ref, w_ref, b_ref, o_ref):
    z = _dot(x_ref[...], w_ref[...]) + b_ref[...]
    lane = lax.broadcasted_iota(jnp.int32, z.shape, 1)
    sig = 1.0 / (1.0 + jnp.exp(-z))
    logsig = jnp.minimum(z, 0.0) - jnp.log1p(jnp.exp(-jnp.abs(z)))
    o_ref[...] = jnp.where(lane < N_GATE, sig, logsig)


def _small_proj(xn, w_small, b_small):
    n, d = xn.shape
    tm = _pick(n, (768, 512, 384, 256, 128, 64, 32, 16, 8))
    return pl.pallas_call(
        _small_kernel,
        grid=(n // tm,),
        in_specs=[pl.BlockSpec((tm, d), lambda i: (i, 0)), pl.BlockSpec((d, LANES), lambda i: (0, 0)),
                  pl.BlockSpec((1, LANES), lambda i: (0, 0))],
        out_specs=pl.BlockSpec((tm, LANES), lambda i: (i, 0)),
        out_shape=jax.ShapeDtypeStruct((n, LANES), jnp.float32),
        compiler_params=_cparams(("parallel",)),
        name="small_proj",
    )(xn, w_small, b_small)


def _tri_lower(n):
    r = lax.broadcasted_iota(jnp.int32, (n, n), 0)
    c = lax.broadcasted_iota(jnp.int32, (n, n), 1)
    return (c <= r).astype(jnp.bfloat16)


def _block_cumsum(x, tri):
    hi, mid, lo = _split3(x)
    return _dot(tri, hi) + _dot(tri, mid) + _dot(tri, lo)


def _cumsum_prompt_kernel(x_ref, o_ref, carry_ref):
    @pl.when(pl.program_id(1) == 0)
    def _():
        carry_ref[...] = jnp.zeros_like(carry_ref)

    blk = x_ref.shape[0]
    c = _block_cumsum(x_ref[...], _tri_lower(blk)) + carry_ref[...]
    o_ref[...] = c
    carry_ref[...] = c[blk - 1:blk, :]


def _cumsum_prompt(slab, b, t):
    blk = _pick(t, (512, 256, 128))
    nb = t // blk
    return pl.pallas_call(
        _cumsum_prompt_kernel,
        grid=(b, nb),
        in_specs=[pl.BlockSpec((blk, LANES), lambda i, j: (i * nb + j, 0))],
        out_specs=pl.BlockSpec((blk, LANES), lambda i, j: (i * nb + j, 0)),
        out_shape=jax.ShapeDtypeStruct((b * t, LANES), jnp.float32),
        scratch_shapes=[pltpu.VMEM((1, LANES), jnp.float32)],
        compiler_params=_cparams(("parallel", "arbitrary")),
        name="cumsum_prompt",
    )(slab)


def _cumsum_sample_kernel(pt_ref, *refs, n_steps):
    pages = refs[:PAGES_PER_STEP]
    new_ref, o_ref, carry_ref = refs[PAGES_PER_STEP:]
    s = pl.program_id(1)
    _, h, page = pages[0].shape

    @pl.when(s == 0)
    def _():
        carry_ref[...] = jnp.zeros_like(carry_ref)

    def local(x):
        r = lax.broadcasted_iota(jnp.int32, (page, page), 0)
        c = lax.broadcasted_iota(jnp.int32, (page, page), 1)
        tri = (r <= c).astype(jnp.bfloat16)
        hi, mid, lo = _split3(x)
        return _dot(hi, tri) + _dot(mid, tri) + _dot(lo, tri)

    @pl.when(s < n_steps)
    def _():
        loc = local(jnp.concatenate([p[0] for p in pages], axis=0))
        carry = carry_ref[...]
        for p in range(PAGES_PER_STEP):
            blk = loc[p * h:(p + 1) * h] + carry
            o_ref[0, :, p * page:(p + 1) * page] = blk
            carry = blk[:, page - 1:page]
        carry_ref[...] = carry

    @pl.when(s == n_steps)
    def _():
        o_ref[0, :, 0:page] = local(new_ref[0]) + carry_ref[...]
        o_ref[0, :, page:] = jnp.zeros((h, (PAGES_PER_STEP - 1) * page), jnp.float32)


def _cumsum_sample(page_table, pool_t, new_t):
    bsz, n_pages = page_table.shape
    _, h, page = pool_t.shape
    n_steps = n_pages // PAGES_PER_STEP
    width = PAGES_PER_STEP * page

    def page_spec(k):
        return pl.BlockSpec(
            (1, h, page), lambda b, s, pt: (pt[b, jnp.minimum(s, n_steps - 1) * PAGES_PER_STEP + k], 0, 0))

    kern = functools.partial(_cumsum_sample_kernel, n_steps=n_steps)
    return pl.pallas_call(
        kern,
        grid_spec=pltpu.PrefetchScalarGridSpec(
            num_scalar_prefetch=1,
            grid=(bsz, n_steps + 1),
            in_specs=[page_spec(k) for k in range(PAGES_PER_STEP)] + [
                pl.BlockSpec((1, h, page), lambda b, s, pt: (b, 0, 0))],
            out_specs=pl.BlockSpec((1, h, width), lambda b, s, pt: (b, 0, s)),
            scratch_shapes=[pltpu.VMEM((h, 1), jnp.float32)],
        ),
        out_shape=jax.ShapeDtypeStruct((bsz, h, (n_steps + 1) * width), jnp.float32),
        compiler_params=_cparams(("parallel", "arbitrary")),
        name="cumsum_sample",
    )(page_table, *([pool_t] * PAGES_PER_STEP), new_t)


def _pe_term(pe_ref, w_ref):
    acc = jnp.zeros((16, HEAD_DIM), jnp.float32)
    for j in range(CMP_STRIDE):
        lo = jnp.broadcast_to(pe_ref[j:j + 1, :], (16, HEAD_DIM)).astype(jnp.bfloat16)
        hi = jnp.broadcast_to(pe_ref[CMP_STRIDE + j:CMP_STRIDE + j + 1, :], (16, HEAD_DIM)).astype(jnp.bfloat16)
        w = w_ref[j]
        acc = acc + _dot(lo, w[:, :HEAD_DIM]) + _dot(hi, w[:, HEAD_DIM:])
    return acc[0:1]


def _compress_prompt_kernel(x_ref, w_ref, pe_ref, o_ref, *, n_chunks):
    acc = jnp.zeros((n_chunks, 2 * HEAD_DIM), jnp.float32)
    for j in range(CMP_STRIDE):
        xj = x_ref[pl.ds(j, n_chunks, stride=CMP_STRIDE), :].astype(jnp.bfloat16)
        acc = acc + _dot(xj, w_ref[0, j])
    nxt = pltpu.roll(acc[:, HEAD_DIM:], n_chunks - 1, 0)
    o_ref[0, 0, 0] = (acc[:, :HEAD_DIM] + nxt + _pe_term(pe_ref.at[0], w_ref.at[0])).astype(o_ref.dtype)


def _compress_prompt(hf, b, t, col0, w_cat, pe):
    n_chunks = t // CMP_STRIDE
    rows_blk = col0 // HEAD_DIM
    kern = functools.partial(_compress_prompt_kernel, n_chunks=n_chunks)
    return pl.pallas_call(
        kern,
        grid=(b, 2, KV_HEADS),
        in_specs=[
            pl.BlockSpec((t, HEAD_DIM), lambda i, kv, g: (i, rows_blk + kv * KV_HEADS + g)),
            pl.BlockSpec((1, CMP_STRIDE, HEAD_DIM, 2 * HEAD_DIM), lambda i, kv, g: (kv, 0, 0, 0)),
            pl.BlockSpec((1, CMP_BLOCK, HEAD_DIM), lambda i, kv, g: (kv, 0, 0)),
        ],
        out_specs=pl.BlockSpec((1, 1, 1, n_chunks, HEAD_DIM), lambda i, kv, g: (i, kv, g, 0, 0)),
        out_shape=jax.ShapeDtypeStruct((b, 2, KV_HEADS, n_chunks, HEAD_DIM), jnp.bfloat16),
        compiler_params=_cparams(("parallel", "parallel", "parallel")),
        name="compress_prompt",
    )(hf, w_cat, pe)


def _compress_sample_kernel(pt_ref, *refs, n_steps):
    pages = refs[:PAGES_PER_STEP]
    next_ref, new_ref, w_ref, pe_ref, o_ref = refs[PAGES_PER_STEP:]
    s = pl.program_id(1)
    chunk_rows = CMP_STRIDE * ROW_SLABS
    cpp = pages[0].shape[1] // chunk_rows
    n_out = PAGES_PER_STEP * cpp * ROW_SLABS
    is_last = s == n_steps - 1

    def chunk_lhs(ref, c):
        base = c * chunk_rows
        return jnp.concatenate([ref[0, base + j * ROW_SLABS:base + (j + 1) * ROW_SLABS, :]
                                for j in range(CMP_STRIDE)], axis=1)

    lhs = [chunk_lhs(p, c) for p in pages for c in range(cpp)]
    lhs.append(jnp.where(is_last, chunk_lhs(new_ref, 0), chunk_lhs(next_ref, 0)))
    out = _dot(jnp.concatenate(lhs, axis=0).astype(jnp.bfloat16), w_ref[...])
    pe = _dot(pe_ref[...].astype(jnp.bfloat16), w_ref[...])
    own, nxt = out[:n_out], out[ROW_SLABS:n_out + ROW_SLABS]
    d = HEAD_DIM
    is_k = (lax.broadcasted_iota(jnp.int32, (n_out, d), 0) & (ROW_SLABS - 1)) < KV_HEADS
    pe_k = pe[0:1, 0:d] + pe[1:2, d:2 * d]
    pe_v = pe[2:3, 2 * d:3 * d] + pe[3:4, 3 * d:4 * d]
    o_ref[0] = jnp.where(is_k, own[:, 0:d] + nxt[:, d:2 * d] + pe_k, own[:, 2 * d:3 * d] + nxt[:, 3 * d:4 * d] + pe_v)


def _compress_sample(page_table, pool, new_pad, w_cat, pe):
    bsz, n_pages = page_table.shape
    _, page, cols = pool.shape
    assert n_pages % PAGES_PER_STEP == 0
    n_steps = n_pages // PAGES_PER_STEP
    chunk_rows = CMP_STRIDE * ROW_SLABS
    cps = PAGES_PER_STEP * page // chunk_rows
    w_flat = jnp.concatenate([w_cat[0].reshape(CMP_STRIDE * HEAD_DIM, 2 * HEAD_DIM),
                              w_cat[1].reshape(CMP_STRIDE * HEAD_DIM, 2 * HEAD_DIM)], axis=1)
    pe_lhs = jnp.pad(pe.reshape(2 * CMP_RATIO, CMP_STRIDE * HEAD_DIM), ((0, ROW_SLABS - 2 * CMP_RATIO), (0, 0)))

    def page_spec(k):
        return pl.BlockSpec((1, page, cols), lambda b, s, pt: (pt[b, s * PAGES_PER_STEP + k], 0, 0))

    next_spec = pl.BlockSpec(
        (1, chunk_rows, cols), lambda b, s, pt: (pt[b, jnp.minimum((s + 1) * PAGES_PER_STEP, n_pages - 1)], 0, 0))
    kern = functools.partial(_compress_sample_kernel, n_steps=n_steps)
    out = pl.pallas_call(
        kern,
        grid_spec=pltpu.PrefetchScalarGridSpec(
            num_scalar_prefetch=1,
            grid=(bsz, n_steps),
            in_specs=[page_spec(k) for k in range(PAGES_PER_STEP)] + [
                next_spec,
                pl.BlockSpec((1, chunk_rows, cols), lambda b, s, pt: (b, 0, 0)),
                pl.BlockSpec(w_flat.shape, lambda b, s, pt: (0, 0)),
                pl.BlockSpec(pe_lhs.shape, lambda b, s, pt: (0, 0)),
            ],
            out_specs=pl.BlockSpec((1, cps * ROW_SLABS, HEAD_DIM), lambda b, s, pt: (b, s, 0)),
        ),
        out_shape=jax.ShapeDtypeStruct((bsz, n_steps * cps * ROW_SLABS, HEAD_DIM), jnp.float32),
        compiler_params=_cparams(("parallel", "arbitrary")),
        name="compress_sample",
    )(page_table, *([pool] * (PAGES_PER_STEP + 1)), new_pad, w_flat, pe_lhs)
    out = out.reshape(bsz, n_steps * cps, 2, KV_HEADS, HEAD_DIM).transpose(0, 2, 3, 1, 4)
    return out.astype(jnp.bfloat16)


def _stack_heads(q):
    return jnp.concatenate([q[:, r * HEAD_DIM:(r + 1) * HEAD_DIM] for r in range(REP)], axis=0).astype(jnp.bfloat16)


def _div_pow2(x, c):
    assert c & (c - 1) == 0
    return lax.shift_right_arithmetic(x, jnp.int32(c.bit_length() - 1))


def _unstack_heads(o, tq):
    return jnp.concatenate([o[r * tq:(r + 1) * tq] for r in range(REP)], axis=1)


RANK_GROUP = 8


def _cmp_select_kernel(q_ref, kc_ref, vc_ref, ov_ref, o_ref, sel_ref, rank_ref, *, tq, pos_base, n_cmp_valid, n_slc):
    qi = pl.program_id(2)
    nc = kc_ref.shape[3]
    nsp = ov_ref.shape[1]
    q = _stack_heads(q_ref[...])
    s = _dot_nt(q, kc_ref[0, 0, 0]) * ATTN_SCALE
    assert tq & (tq - 1) == 0
    t1 = pos_base + qi * tq + (lax.broadcasted_iota(jnp.int32, (REP * tq, nc), 0) & (tq - 1))
    n1 = lax.broadcasted_iota(jnp.int32, (REP * tq, nc), 1)
    vis = (n1 * CMP_STRIDE + CMP_BLOCK - 1 <= t1) & (n1 < n_cmp_valid)
    s = jnp.where(vis, s, -jnp.inf)
    m = jnp.max(s, axis=-1, keepdims=True)
    m = jnp.where(m == -jnp.inf, 0.0, m)
    p = jnp.exp(s - m)
    p = p / jnp.maximum(jnp.sum(p, axis=-1, keepdims=True), jnp.finfo(jnp.float32).tiny)
    o = _dot(p.astype(jnp.bfloat16), vc_ref[0, 0, 0])
    o_ref[...] = _unstack_heads(o, tq)

    psum = p[0:tq]
    for r in range(1, REP):
        psum = psum + p[r * tq:(r + 1) * tq]
    hi, mid, lo = _split3(psum)
    ov = ov_ref[...]
    imp = _dot(hi, ov) + _dot(mid, ov) + _dot(lo, ov)
    t = pos_base + qi * tq + lax.broadcasted_iota(jnp.int32, (tq, nsp), 0)
    jj = lax.broadcasted_iota(jnp.int32, (tq, nsp), 1)
    cur = _div_pow2(t, SLC_BLOCK)
    forced = (jj == 0) | (jj == cur) | (jj == cur - 1)
    causal = (jj * SLC_BLOCK <= t) & (jj < n_slc)
    imp = jnp.where(forced, jnp.inf, imp)
    imp = jnp.where(causal, imp, -jnp.inf)
    rank_ref[...] = jnp.zeros_like(rank_ref)
    t_last = pos_base + qi * tq + tq - 1
    for i0 in range(0, n_slc, RANK_GROUP):
        @pl.when(i0 * SLC_BLOCK <= t_last)
        def _():
            part = jnp.zeros((tq, nsp), jnp.float32)
            for i in range(i0, min(i0 + RANK_GROUP, n_slc)):
                col = imp[:, i:i + 1]
                beats = (col > imp) | ((col == imp) & (jj > i))
                part = part + beats.astype(jnp.float32)
            rank_ref[...] += part
    sel = (rank_ref[...] < SLC_TOPK) & (imp > -jnp.inf)
    sel_ref[0] = sel.astype(sel_ref.dtype)


def _overlap_matrix(nc, nsp):
    cs = jnp.arange(nc) * CMP_STRIDE
    ss = jnp.arange(nsp) * SLC_BLOCK
    ov = (cs[:, None] < ss[None, :] + SLC_BLOCK) & (cs[:, None] + CMP_BLOCK > ss[None, :])
    return ov.astype(jnp.bfloat16)


def _cmp_select(q_arr, row_blk0, bsz, nq, tq, kc, pos_base, n_cmp_valid, n_slc, nsp, sel_dtype):
    nc = kc.shape[3]
    rows = bsz * nq * tq
    kern = functools.partial(_cmp_select_kernel, tq=tq, pos_base=pos_base, n_cmp_valid=n_cmp_valid, n_slc=n_slc)
    return pl.pallas_call(
        kern,
        grid=(bsz, KV_HEADS, nq),
        in_specs=[
            pl.BlockSpec((tq, GROUP_COLS), lambda b, g, i: (row_blk0 + b * nq + i, g)),
            pl.BlockSpec((1, 1, 1, nc, HEAD_DIM), lambda b, g, i: (b, 0, g, 0, 0)),
            pl.BlockSpec((1, 1, 1, nc, HEAD_DIM), lambda b, g, i: (b, 1, g, 0, 0)),
            pl.BlockSpec((nc, nsp), lambda b, g, i: (0, 0)),
        ],
        out_specs=[
            pl.BlockSpec((tq, GROUP_COLS), lambda b, g, i: (b * nq + i, g)),
            pl.BlockSpec((1, tq, nsp), lambda b, g, i: (g, b * nq + i, 0)),
        ],
        out_shape=[jax.ShapeDtypeStruct((rows, NSA_HEADS * HEAD_DIM), jnp.float32),
                   jax.ShapeDtypeStruct((KV_HEADS, rows, nsp), sel_dtype)],
        scratch_shapes=[pltpu.VMEM((tq, nsp), jnp.float32)],
        compiler_params=_cparams(("parallel", "parallel", "parallel")),
        name="cmp_select",
    )(q_arr, kc, kc, _overlap_matrix(nc, nsp))


def _online_update(s, v, m_ref, l_ref, acc_ref, rows=slice(None)):
    m_prev = m_ref[rows]
    m_new = jnp.maximum(m_prev, jnp.max(s, axis=-1, keepdims=True))
    alpha = jnp.exp(m_prev - m_new)
    p = jnp.exp(s - m_new)
    l_ref[rows] = alpha * l_ref[rows] + jnp.sum(p, axis=-1, keepdims=True)
    acc_ref[rows] = alpha * acc_ref[rows] + _dot(p.astype(jnp.bfloat16), v)
    m_ref[rows] = m_new


def _flash_kernel(*refs, mode, tq, tk, nk):
    if mode == "slc":
        q_ref, k_ref, v_ref, sel_ref, o_ref, m_ref, l_ref, acc_ref = refs
    else:
        q_ref, k_ref, v_ref, cq_ref, ck_ref, o_ref, m_ref, l_ref, acc_ref = refs
    qi = pl.program_id(2)
    kk = pl.program_id(3)
    kidx = kk
    last = ((qi + 1) * tq - 1) // tk

    @pl.when(kk == 0)
    def _():
        m_ref[...] = jnp.full_like(m_ref, NEG)
        l_ref[...] = jnp.zeros_like(l_ref)
        acc_ref[...] = jnp.zeros_like(acc_ref)

    def step(causal_edge):
        s = _dot_nt(_stack_heads(q_ref[...]), k_ref[...])
        if mode == "slc":
            nsb = sel_ref.shape[2]
            jb = lax.broadcasted_iota(jnp.int32, (nsb, tk), 0)
            kb = kidx * (tk // SLC_BLOCK) + _div_pow2(lax.broadcasted_iota(jnp.int32, (nsb, tk), 1), SLC_BLOCK)
            picked = _dot(sel_ref[0], (jb == kb).astype(jnp.bfloat16))
            sel_bias = (picked - 1.0) * (-NEG)
        if causal_edge:
            t = qi * tq + lax.broadcasted_iota(jnp.int32, (tq, tk), 0)
            kp = kidx * tk + lax.broadcasted_iota(jnp.int32, (tq, tk), 1)
            vis = kp <= t
        us, maxes = [], []
        for r in range(REP):
            u = s[r * tq:(r + 1) * tq] * ATTN_SCALE
            if mode == "fox":
                u = u - ck_ref[0][r:r + 1, :]
            else:
                u = u + sel_bias
            if causal_edge:
                u = jnp.where(vis, u, NEG)
            us.append(u)
            maxes.append(jnp.max(u, axis=-1, keepdims=True))
        row_max = jnp.concatenate(maxes, axis=0)
        if mode == "fox":
            c_t = jnp.concatenate([cq_ref[0][:, r:r + 1] for r in range(REP)], axis=0)
            row_max = row_max + c_t
        m_prev = m_ref[...]
        m_new = jnp.maximum(m_prev, row_max)
        alpha = jnp.exp(m_prev - m_new)
        shift = (c_t - m_new) if mode == "fox" else -m_new
        ps, sums = [], []
        for r in range(REP):
            p = jnp.exp(us[r] + shift[r * tq:(r + 1) * tq])
            sums.append(jnp.sum(p, axis=-1, keepdims=True))
            ps.append(p.astype(jnp.bfloat16))
        l_ref[...] = alpha * l_ref[...] + jnp.concatenate(sums, axis=0)
        acc_ref[...] = alpha * acc_ref[...] + _dot(jnp.concatenate(ps, axis=0), v_ref[...])
        m_ref[...] = m_new

    @pl.when(kk < last)
    def _():
        step(False)

    @pl.when(kk == last)
    def _():
        step(True)

    @pl.when(kk == nk - 1)
    def _():
        o_ref[...] = _unstack_heads(acc_ref[...] / l_ref[...], tq)


def _flash_prompt(mode, hb, b, t, q_col, k_col, v_col, sel=None, cq=None, ck=None):
    tq = _pick(t, (256, 128))
    tk = _pick(t, (512, 256, 128))
    nq = t // tq
    nkt = t // tk
    nk = nkt
    qb, kb, vb = q_col // GROUP_COLS, k_col // HEAD_DIM, v_col // HEAD_DIM

    def kidx(i, k):
        return jnp.minimum(k, ((i + 1) * tq - 1) // tk)

    in_specs = [
        pl.BlockSpec((tq, GROUP_COLS), lambda bi, g, i, k: (bi * nq + i, qb + g)),
        pl.BlockSpec((tk, HEAD_DIM), lambda bi, g, i, k: (bi * nkt + kidx(i, k), kb + g)),
        pl.BlockSpec((tk, HEAD_DIM), lambda bi, g, i, k: (bi * nkt + kidx(i, k), vb + g)),
    ]
    args = [hb, hb, hb]
    if mode == "slc":
        in_specs.append(pl.BlockSpec((1, tq, sel.shape[2]), lambda bi, g, i, k: (g, bi * nq + i, 0)))
        args.append(sel)
    if mode == "fox":
        in_specs.append(pl.BlockSpec((1, tq, REP), lambda bi, g, i, k: (g, bi * nq + i, 0)))
        in_specs.append(pl.BlockSpec((1, REP, tk), lambda bi, g, i, k: (g, 0, bi * nkt + kidx(i, k))))
        args += [cq, ck]
    kern = functools.partial(_flash_kernel, mode=mode, tq=tq, tk=tk, nk=nk)
    return pl.pallas_call(
        kern,
        grid=(b, KV_HEADS, nq, nk),
        in_specs=in_specs,
        out_specs=pl.BlockSpec((tq, GROUP_COLS), lambda bi, g, i, k: (bi * nq + i, g)),
        out_shape=jax.ShapeDtypeStruct((b * t, KV_HEADS * GROUP_COLS), jnp.float32),
        scratch_shapes=[pltpu.VMEM((REP * tq, 1), jnp.float32), pltpu.VMEM((REP * tq, 1), jnp.float32),
                        pltpu.VMEM((REP * tq, HEAD_DIM), jnp.float32)],
        compiler_params=_cparams(("parallel", "parallel", "parallel", "arbitrary")),
        name="flash_" + mode,
    )(*args)


def _win_prompt_kernel(q_ref, *refs, tq, n_kb):
    k_refs, v_refs, o_ref = refs[:n_kb], refs[n_kb:2 * n_kb], refs[2 * n_kb]
    qi = pl.program_id(2)
    span = n_kb * tq
    k = jnp.concatenate([r[...] for r in k_refs], axis=0)
    v = jnp.concatenate([r[...] for r in v_refs], axis=0)
    s = _dot_nt(_stack_heads(q_ref[...]), k) * ATTN_SCALE
    t = qi * tq + lax.broadcasted_iota(jnp.int32, (tq, span), 0)
    kp = (qi - (n_kb - 1)) * tq + lax.broadcasted_iota(jnp.int32, (tq, span), 1)
    mask = (kp >= 0) & (kp <= t) & (t - kp < WINDOW)
    s = s + jnp.concatenate([jnp.where(mask, 0.0, NEG)] * REP, axis=0)
    p = jnp.exp(s - jnp.max(s, axis=-1, keepdims=True))
    o = _dot(p.astype(jnp.bfloat16), v) / jnp.sum(p, axis=-1, keepdims=True)
    o_ref[...] = _unstack_heads(o, tq)


def _win_prompt(hb, b, t, q_col, k_col, v_col):
    tq = _pick(t, (256, 128))
    assert WINDOW % tq == 0
    n_kb = WINDOW // tq + 1
    nq = t // tq
    qb, kb, vb = q_col // GROUP_COLS, k_col // HEAD_DIM, v_col // HEAD_DIM

    def kv_spec(col_blk, j):
        return pl.BlockSpec((tq, HEAD_DIM),
                            lambda bi, g, i: (bi * nq + jnp.maximum(i - (n_kb - 1) + j, 0), col_blk + g))

    in_specs = ([pl.BlockSpec((tq, GROUP_COLS), lambda bi, g, i: (bi * nq + i, qb + g))]
                + [kv_spec(kb, j) for j in range(n_kb)] + [kv_spec(vb, j) for j in range(n_kb)])
    kern = functools.partial(_win_prompt_kernel, tq=tq, n_kb=n_kb)
    return pl.pallas_call(
        kern,
        grid=(b, KV_HEADS, nq),
        in_specs=in_specs,
        out_specs=pl.BlockSpec((tq, GROUP_COLS), lambda bi, g, i: (bi * nq + i, g)),
        out_shape=jax.ShapeDtypeStruct((b * t, KV_HEADS * GROUP_COLS), jnp.float32),
        compiler_params=_cparams(("parallel", "parallel", "parallel")),
        name="win_prompt",
    )(*([hb] * (1 + 2 * n_kb)))


def _rows_of(g, tn):
    return slice(g * REP * tn, (g + 1) * REP * tn)


def _sample_init(m_ref, l_ref, acc_ref):
    m_ref[...] = jnp.full_like(m_ref, NEG)
    l_ref[...] = jnp.zeros_like(l_ref)
    acc_ref[...] = jnp.zeros_like(acc_ref)


def _sample_finish(o_ref, l_ref, acc_ref, tn):
    res = acc_ref[...] / l_ref[...]
    for g in range(KV_HEADS):
        o_ref[:, g * GROUP_COLS:(g + 1) * GROUP_COLS] = _unstack_heads(res[_rows_of(g, tn)], tn)


def _online_update_groups(scores, adds, values, m_ref, l_ref, acc_ref, tn):
    s = jnp.concatenate(scores, axis=0) * ATTN_SCALE + jnp.concatenate(adds, axis=0)
    m_prev = m_ref[...]
    m_new = jnp.maximum(m_prev, jnp.max(s, axis=-1, keepdims=True))
    alpha = jnp.exp(m_prev - m_new)
    p = jnp.exp(s - m_new)
    l_ref[...] = alpha * l_ref[...] + jnp.sum(p, axis=-1, keepdims=True)
    pb = p.astype(jnp.bfloat16)
    pv = jnp.concatenate([_dot(pb[_rows_of(g, tn)], values[g]) for g in range(KV_HEADS)], axis=0)
    acc_ref[...] = alpha * acc_ref[...] + pv
    m_ref[...] = m_new


def _kv_of(page_refs, g):
    def pick(ref, slab):
        return ref[0, pl.ds(slab, ref.shape[1] // ROW_SLABS, stride=ROW_SLABS), :]
    k = jnp.concatenate([pick(p, g) for p in page_refs], axis=0)
    v = jnp.concatenate([pick(p, KV_HEADS + g) for p in page_refs], axis=0)
    return k.astype(jnp.bfloat16), v.astype(jnp.bfloat16)


def _new_rows_mask(tn, page):
    tok = lax.broadcasted_iota(jnp.int32, (tn, page), 0)
    r = lax.broadcasted_iota(jnp.int32, (tn, page), 1)
    return r <= tok


def _paged_attn_kernel(pt_ref, *refs, mode, n_steps, tn):
    pages = refs[:PAGES_PER_STEP]
    if mode == "slc":
        q_ref, new_ref, sel_ref, selnew_ref, o_ref, m_ref, l_ref, acc_ref = refs[PAGES_PER_STEP:]
    else:
        q_ref, new_ref, cq_ref, ck_ref, cknew_ref, o_ref, m_ref, l_ref, acc_ref = refs[PAGES_PER_STEP:]
    s_id = pl.program_id(1)
    page = pages[0].shape[1] // ROW_SLABS
    nkeys = PAGES_PER_STEP * page

    @pl.when(s_id == 0)
    def _():
        _sample_init(m_ref, l_ref, acc_ref)

    def bias_rows(g, ck, add):
        return jnp.concatenate(
            [(cq_ref[0][:, g * REP + r:g * REP + r + 1] - ck[g * REP + r:g * REP + r + 1, :]) + add
             for r in range(REP)], axis=0)

    if mode == "slc":
        key_blk = _div_pow2(lax.broadcasted_iota(jnp.int32, (tn, nkeys), 1), SLC_BLOCK)
    scores, values, adds = [], [], []
    for g in range(KV_HEADS):
        q = _stack_heads(q_ref[:, g * GROUP_COLS:(g + 1) * GROUP_COLS])
        k, v = _kv_of(pages, g)
        scores.append(_dot_nt(q, k))
        values.append(v)
        if mode == "slc":
            flags = sel_ref[0, 0, g]
            picked = jnp.zeros((tn, nkeys), jnp.float32)
            for jb in range(flags.shape[1]):
                picked = jnp.where(key_blk == jb, flags[:, jb:jb + 1], picked)
            adds += [jnp.where(picked > 0.5, 0.0, NEG)] * REP
        else:
            adds.append(bias_rows(g, ck_ref[0], 0.0))
    _online_update_groups(scores, adds, values, m_ref, l_ref, acc_ref, tn)

    @pl.when(s_id == n_steps - 1)
    def _():
        causal = _new_rows_mask(tn, page)
        scores, values, adds = [], [], []
        for g in range(KV_HEADS):
            q = _stack_heads(q_ref[:, g * GROUP_COLS:(g + 1) * GROUP_COLS])
            k, v = _kv_of([new_ref], g)
            scores.append(_dot_nt(q, k))
            values.append(v)
            if mode == "slc":
                adds += [jnp.where(causal & (selnew_ref[0, 0, g][:, 0:1] > 0.5), 0.0, NEG)] * REP
            else:
                adds.append(bias_rows(g, cknew_ref[0], jnp.where(causal, 0.0, NEG)))
        _online_update_groups(scores, adds, values, m_ref, l_ref, acc_ref, tn)
        _sample_finish(o_ref, l_ref, acc_ref, tn)


def _paged_attn(mode, page_table, pool, new_pad, q_s, tn, sel5=None, cq=None, ck=None):
    bsz, n_pages = page_table.shape
    _, slab_rows, cols = pool.shape
    page = slab_rows // ROW_SLABS
    n_steps = n_pages // PAGES_PER_STEP
    nkeys = PAGES_PER_STEP * page

    def page_spec(k):
        return pl.BlockSpec((1, slab_rows, cols), lambda b, s, pt: (pt[b, s * PAGES_PER_STEP + k], 0, 0))

    in_specs = [page_spec(k) for k in range(PAGES_PER_STEP)] + [
        pl.BlockSpec((tn, NSA_HEADS * HEAD_DIM), lambda b, s, pt: (b, 0)),
        pl.BlockSpec((1, slab_rows, cols), lambda b, s, pt: (b, 0, 0)),
    ]
    args = [pool] * PAGES_PER_STEP + [q_s, new_pad]
    if mode == "slc":
        nb = sel5.shape[4]
        in_specs += [pl.BlockSpec((1, 1, KV_HEADS, tn, nb), lambda b, s, pt: (b, s, 0, 0, 0)),
                     pl.BlockSpec((1, 1, KV_HEADS, tn, nb), lambda b, s, pt: (b, n_steps, 0, 0, 0))]
        args += [sel5, sel5]
    else:
        nh = cq.shape[2]
        in_specs += [pl.BlockSpec((1, tn, nh), lambda b, s, pt: (b, 0, 0)),
                     pl.BlockSpec((1, nh, nkeys), lambda b, s, pt: (b, 0, s)),
                     pl.BlockSpec((1, nh, page), lambda b, s, pt: (b, 0, n_pages))]
        args += [cq, ck, ck]
    rows = KV_HEADS * REP * tn
    kern = functools.partial(_paged_attn_kernel, mode=mode, n_steps=n_steps, tn=tn)
    return pl.pallas_call(
        kern,
        grid_spec=pltpu.PrefetchScalarGridSpec(
            num_scalar_prefetch=1,
            grid=(bsz, n_steps),
            in_specs=in_specs,
            out_specs=pl.BlockSpec((tn, NSA_HEADS * HEAD_DIM), lambda b, s, pt: (b, 0)),
            scratch_shapes=[pltpu.VMEM((rows, 1), jnp.float32), pltpu.VMEM((rows, 1), jnp.float32),
                            pltpu.VMEM((rows, HEAD_DIM), jnp.float32)],
        ),
        out_shape=jax.ShapeDtypeStruct((bsz * tn, NSA_HEADS * HEAD_DIM), jnp.float32),
        compiler_params=_cparams(("parallel", "arbitrary")),
        name="paged_" + mode,
    )(page_table, *args)


def _win_sample_kernel(q_ref, win_ref, new_ref, o_ref, m_ref, l_ref, acc_ref, *, tn, past):
    _sample_init(m_ref, l_ref, acc_ref)
    win_len = win_ref.shape[1] // ROW_SLABS
    page = new_ref.shape[1] // ROW_SLABS
    pos = past + lax.broadcasted_iota(jnp.int32, (tn, win_len), 0)
    kpos = past - win_len + lax.broadcasted_iota(jnp.int32, (tn, win_len), 1)
    wmask = (kpos <= pos) & (pos - kpos < WINDOW)
    add_w = [jnp.where(wmask, 0.0, NEG)] * (REP * KV_HEADS)
    add_n = [jnp.where(_new_rows_mask(tn, page), 0.0, NEG)] * (REP * KV_HEADS)
    for src, adds in ((new_ref, add_n), (win_ref, add_w)):
        scores, values = [], []
        for g in range(KV_HEADS):
            q = _stack_heads(q_ref[:, g * GROUP_COLS:(g + 1) * GROUP_COLS])
            k, v = _kv_of([src], g)
            scores.append(_dot_nt(q, k))
            values.append(v)
        _online_update_groups(scores, adds, values, m_ref, l_ref, acc_ref, tn)
    _sample_finish(o_ref, l_ref, acc_ref, tn)


def _win_sample(win_buf, new_pad, q_s, tn, past):
    bsz, win_rows, cols = win_buf.shape
    new_rows = new_pad.shape[1]
    rows = KV_HEADS * REP * tn
    kern = functools.partial(_win_sample_kernel, tn=tn, past=past)
    return pl.pallas_call(
        kern,
        grid=(bsz,),
        in_specs=[pl.BlockSpec((tn, NSA_HEADS * HEAD_DIM), lambda b: (b, 0)),
                  pl.BlockSpec((1, win_rows, cols), lambda b: (b, 0, 0)),
                  pl.BlockSpec((1, new_rows, cols), lambda b: (b, 0, 0))],
        out_specs=pl.BlockSpec((tn, NSA_HEADS * HEAD_DIM), lambda b: (b, 0)),
        out_shape=jax.ShapeDtypeStruct((bsz * tn, NSA_HEADS * HEAD_DIM), jnp.float32),
        scratch_shapes=[pltpu.VMEM((rows, 1), jnp.float32), pltpu.VMEM((rows, 1), jnp.float32),
                        pltpu.VMEM((rows, HEAD_DIM), jnp.float32)],
        compiler_params=_cparams(("parallel",)),
        name="win_sample",
    )(q_s, win_buf, new_pad)


def _merge_kernel(cmp_ref, slc_ref, win_ref, fox_ref, gate_ref, gn_ref, gf_ref, o_ref):
    gates = gate_ref[...]
    parts = []
    for h in range(NSA_HEADS):
        cols = slice(h * HEAD_DIM, (h + 1) * HEAD_DIM)
        c = N_BRANCH * h
        parts.append(gates[:, c:c + 1] * cmp_ref[:, cols] + gates[:, c + 1:c + 2] * slc_ref[:, cols]
                     + gates[:, c + 2:c + 3] * win_ref[:, cols])
    nsa = jnp.concatenate(parts, axis=1)
    nsa = nsa * lax.rsqrt(jnp.mean(nsa * nsa, axis=-1, keepdims=True) + RMS_EPS) * gn_ref[...]
    fox = fox_ref[...]
    fox = fox * lax.rsqrt(jnp.mean(fox * fox, axis=-1, keepdims=True) + RMS_EPS) * gf_ref[...]
    o_ref[...] = jnp.concatenate([nsa, fox], axis=1).astype(o_ref.dtype)


def _merge(o_cmp, o_slc, o_win, o_fox, slab, g_nsa, g_fox):
    n, w = o_cmp.shape
    tm = _pick(n, (256, 128, 64, 32, 16, 8))
    row = lambda i: (i, 0)
    fix = lambda i: (0, 0)
    return pl.pallas_call(
        _merge_kernel,
        grid=(n // tm,),
        in_specs=[pl.BlockSpec((tm, w), row)] * 4 + [pl.BlockSpec((tm, LANES), row), pl.BlockSpec((1, w), fix),
                                                      pl.BlockSpec((1, w), fix)],
        out_specs=pl.BlockSpec((tm, 2 * w), row),
        out_shape=jax.ShapeDtypeStruct((n, 2 * w), jnp.bfloat16),
        compiler_params=_cparams(("parallel",)),
        name="merge_heads",
    )(o_cmp, o_slc, o_win, o_fox, slab, g_nsa.reshape(1, w), g_fox.reshape(1, w))


def _outproj_kernel(a_ref, w_ref, x_ref, o_ref):
    o_ref[...] = x_ref[...] + _dot(a_ref[...], w_ref[...])


def _outproj(a, w, x):
    n, k = a.shape
    d = w.shape[1]
    tm = _pick(n, (768, 512, 384, 256, 128, 64, 32, 16, 8))
    tn = 512
    return pl.pallas_call(
        _outproj_kernel,
        grid=(n // tm, d // tn),
        in_specs=[pl.BlockSpec((tm, k), lambda i, j: (i, 0)), pl.BlockSpec((k, tn), lambda i, j: (0, j)),
                  pl.BlockSpec((tm, tn), lambda i, j: (i, j))],
        out_specs=pl.BlockSpec((tm, tn), lambda i, j: (i, j)),
        out_shape=jax.ShapeDtypeStruct((n, d), jnp.float32),
        compiler_params=_cparams(("parallel", "arbitrary")),
        name="outproj",
    )(a, w, x)


def _pack_bf16_pair(lo, hi):
    def bits(x):
        return lax.bitcast_convert_type(x.astype(jnp.bfloat16).astype(jnp.float32), jnp.uint32)
    return (bits(hi) & jnp.uint32(0xFFFF0000)) | lax.shift_right_logical(bits(lo), jnp.uint32(16))


def _unpack_bf16_pair(word):
    lo = lax.bitcast_convert_type(lax.shift_left(word, jnp.uint32(16)), jnp.float32)
    hi = lax.bitcast_convert_type(word & jnp.uint32(0xFFFF0000), jnp.float32)
    return lo, hi


def _router_kernel(h_ref, g_ref, w_ref, b_ref, xn_ref, eid_ref, gate_ref):
    x = h_ref[...]
    xn = x * lax.rsqrt(jnp.mean(x * x, axis=-1, keepdims=True) + RMS_EPS) * g_ref[...]
    half = xn.shape[1] // 2
    xn_ref[...] = _pack_bf16_pair(xn[:, :half], xn[:, half:])
    lg = jnp.dot(xn, w_ref[...], precision=lax.Precision.HIGHEST, preferred_element_type=jnp.float32) + b_ref[...]
    lane = lax.broadcasted_iota(jnp.int32, lg.shape, 1)
    lane_f = lane.astype(jnp.float32)
    ninf = -jnp.inf
    is_grp = lane < N_GROUPS
    gl = jnp.where(is_grp, lg, ninf)
    gmax = jnp.max(gl, axis=-1, keepdims=True)
    gsel = jnp.min(jnp.where(gl == gmax, lane_f, float(LANES)), axis=-1, keepdims=True)
    p_sel = 1.0 / jnp.sum(jnp.where(is_grp, jnp.exp(lg - gmax), 0.0), axis=-1, keepdims=True)
    e_lane = lane - N_GROUPS
    e_grp = _div_pow2(e_lane, EXPERTS_PER_GROUP).astype(jnp.float32)
    in_grp = (e_lane >= 0) & (e_lane < N_EXPERTS) & (e_grp == gsel)
    el = jnp.where(in_grp, lg, ninf)
    v1 = jnp.max(el, axis=-1, keepdims=True)
    i1 = jnp.min(jnp.where(el == v1, lane_f, float(LANES)), axis=-1, keepdims=True)
    el2 = jnp.where(lane_f == i1, ninf, el)
    v2 = jnp.max(el2, axis=-1, keepdims=True)
    i2 = jnp.min(jnp.where(el2 == v2, lane_f, float(LANES)), axis=-1, keepdims=True)
    e2 = jnp.exp(v2 - v1)
    den = 1.0 + e2
    g1 = p_sel * (1.0 / den)
    g2 = p_sel * (e2 / den)
    eid = jnp.where(lane == 0, i1 - N_GROUPS, jnp.where(lane == 1, i2 - N_GROUPS, 0.0))
    eid_ref[...] = eid.astype(jnp.int32)
    gate_ref[...] = jnp.where(lane == 0, g1, jnp.where(lane == 1, g2, 0.0))


def _router(h1, g_ffn, w_r, b_r):
    n, d = h1.shape
    tm = _pick(n, (256, 128, 64, 32, 16, 8))
    row = lambda i: (i, 0)
    fix = lambda i: (0, 0)
    return pl.pallas_call(
        _router_kernel,
        grid=(n // tm,),
        in_specs=[pl.BlockSpec((tm, d), row), pl.BlockSpec((1, d), fix), pl.BlockSpec((d, LANES), fix),
                  pl.BlockSpec((1, LANES), fix)],
        out_specs=[pl.BlockSpec((tm, d // 2), row), pl.BlockSpec((tm, LANES), row), pl.BlockSpec((tm, LANES), row)],
        out_shape=[jax.ShapeDtypeStruct((n, d // 2), jnp.uint32), jax.ShapeDtypeStruct((n, LANES), jnp.int32),
                   jax.ShapeDtypeStruct((n, LANES), jnp.float32)],
        compiler_params=_cparams(("parallel",)),
        name="router",
    )(h1, g_ffn.reshape(1, d), w_r, b_r)


SC_CORES = 2
SC_SUBCORES = 16
SC_WORKERS = SC_CORES * SC_SUBCORES
SC_GATHER_ROWS = 16


def _gather_rows(idx, table):
    n = idx.shape[0]
    _, d = table.shape
    assert n % (SC_WORKERS * SC_GATHER_ROWS) == 0, n
    per_worker = n // SC_WORKERS
    n_chunks = per_worker // SC_GATHER_ROWS
    mesh = plsc.VectorSubcoreMesh(core_axis_name="c", subcore_axis_name="s")

    @functools.partial(
        pl.kernel, mesh=mesh,
        out_type=jax.ShapeDtypeStruct((n, d), table.dtype),
        scratch_types=[pltpu.VMEM((SC_GATHER_ROWS,), jnp.int32),
                       pltpu.VMEM((SC_GATHER_ROWS, d), table.dtype),
                       pltpu.SemaphoreType.DMA],
    )
    def gather(table_hbm, idx_hbm, out_hbm, idx_v, rows_v, sem):
        wid = lax.axis_index("s") * SC_CORES + lax.axis_index("c")
        base = wid * per_worker

        @pl.loop(0, n_chunks)
        def _(i):
            off = pl.multiple_of(base + i * SC_GATHER_ROWS, SC_GATHER_ROWS)
            pltpu.sync_copy(idx_hbm.at[pl.ds(off, SC_GATHER_ROWS)], idx_v)
            pltpu.async_copy(table_hbm.at[idx_v], rows_v, sem).wait()
            pltpu.sync_copy(rows_v, out_hbm.at[pl.ds(off, SC_GATHER_ROWS)])

    return gather(table, idx)


MOE_TM = 256
MOE_FT = 256
MOE_NT = 1024


def _gateup_kernel(tb_ref, e_ref, f_ref, first_ref, valid_ref, x_ref, wg_ref, wu_ref, h_ref, wgb_ref, wub_ref):
    s = pl.program_id(0)

    @pl.when(first_ref[s] == 1)
    def _():
        wgb_ref[...] = wg_ref[0].astype(jnp.bfloat16)
        wub_ref[...] = wu_ref[0].astype(jnp.bfloat16)

    @pl.when(valid_ref[s] == 1)
    def _():
        x = jnp.concatenate(_unpack_bf16_pair(x_ref[...]), axis=1).astype(jnp.bfloat16)
        a = _dot(x, wgb_ref[...])
        u = _dot(x, wub_ref[...])
        h_ref[...] = (a * (1.0 / (1.0 + jnp.exp(-a))) * u).astype(h_ref.dtype)


def _down_kernel(tb_ref, e_ref, f_ref, first_ref, valid_ref, h_ref, wd_ref, y_ref, wdb_ref):
    s = pl.program_id(0)

    @pl.when(first_ref[s] == 1)
    def _():
        wdb_ref[...] = wd_ref[0].astype(jnp.bfloat16)

    @pl.when(valid_ref[s] == 1)
    def _():
        y = _dot(h_ref[...], wdb_ref[...])
        half = y.shape[1] // 2
        y_ref[...] = _pack_bf16_pair(y[:, :half], y[:, half:])


def _work_list(block_e, start_blk, n_blk, total_blocks, n_tiles, n_tb_max):
    n_steps = n_tb_max * n_tiles
    s = jnp.minimum(jnp.arange(n_steps, dtype=jnp.int32), total_blocks * n_tiles - 1)
    valid = (jnp.arange(n_steps, dtype=jnp.int32) < total_blocks * n_tiles).astype(jnp.int32)
    e = block_e[s // n_tiles]
    local = s - n_tiles * start_blk[e]
    nb = jnp.maximum(n_blk[e], 1)
    tile = local // nb
    within = local % nb
    tb = start_blk[e] + within
    first = ((within == 0) & (valid == 1)).astype(jnp.int32)
    return tb.astype(jnp.int32), e.astype(jnp.int32), tile.astype(jnp.int32), first, valid


def _experts(xs, w_gate, w_up, w_down, block_e, start_blk, n_blk, total_blocks):
    a_pad, d = xs.shape[0], 2 * xs.shape[1]
    n_tb_max = a_pad // MOE_TM
    d_exp = w_gate.shape[2]
    wl1 = _work_list(block_e, start_blk, n_blk, total_blocks, d_exp // MOE_FT, n_tb_max)
    hidden = pl.pallas_call(
        _gateup_kernel,
        grid_spec=pltpu.PrefetchScalarGridSpec(
            num_scalar_prefetch=5,
            grid=(wl1[0].shape[0],),
            in_specs=[
                pl.BlockSpec((MOE_TM, d // 2), lambda s, tb, e, f, fi, va: (tb[s], 0)),
                pl.BlockSpec((1, d, MOE_FT), lambda s, tb, e, f, fi, va: (e[s], 0, f[s])),
                pl.BlockSpec((1, d, MOE_FT), lambda s, tb, e, f, fi, va: (e[s], 0, f[s])),
            ],
            out_specs=pl.BlockSpec((MOE_TM, MOE_FT), lambda s, tb, e, f, fi, va: (tb[s], f[s])),
            scratch_shapes=[pltpu.VMEM((d, MOE_FT), jnp.bfloat16), pltpu.VMEM((d, MOE_FT), jnp.bfloat16)],
        ),
        out_shape=jax.ShapeDtypeStruct((a_pad, d_exp), jnp.bfloat16),
        compiler_params=_cparams(("arbitrary",)),
        name="expert_gate_up",
    )(*wl1, xs, w_gate, w_up)
    wl2 = _work_list(block_e, start_blk, n_blk, total_blocks, d // MOE_NT, n_tb_max)
    return pl.pallas_call(
        _down_kernel,
        grid_spec=pltpu.PrefetchScalarGridSpec(
            num_scalar_prefetch=5,
            grid=(wl2[0].shape[0],),
            in_specs=[
                pl.BlockSpec((MOE_TM, d_exp), lambda s, tb, e, f, fi, va: (tb[s], 0)),
                pl.BlockSpec((1, d_exp, MOE_NT), lambda s, tb, e, f, fi, va: (e[s], 0, f[s])),
            ],
            out_specs=pl.BlockSpec((MOE_TM, MOE_NT // 2), lambda s, tb, e, f, fi, va: (tb[s], f[s])),
            scratch_shapes=[pltpu.VMEM((d_exp, MOE_NT), jnp.bfloat16)],
        ),
        out_shape=jax.ShapeDtypeStruct((a_pad, d // 2), jnp.uint32),
        compiler_params=_cparams(("arbitrary",)),
        name="expert_down",
    )(*wl2, hidden, w_down)


def _combine_kernel(h_ref, y0_ref, y1_ref, gate_ref, o_ref):
    g = gate_ref[...]
    half = MOE_NT // 2
    for t in range(o_ref.shape[1] // MOE_NT):
        w = slice(t * half, (t + 1) * half)
        lo0, hi0 = _unpack_bf16_pair(y0_ref[0, :, w])
        lo1, hi1 = _unpack_bf16_pair(y1_ref[0, :, w])
        c = t * MOE_NT
        o_ref[:, c:c + half] = h_ref[:, c:c + half] + (g[:, 0:1] * lo0 + g[:, 1:2] * lo1)
        o_ref[:, c + half:c + MOE_NT] = h_ref[:, c + half:c + MOE_NT] + (g[:, 0:1] * hi0 + g[:, 1:2] * hi1)


def _combine(h1, y2, gate_slab):
    n, d = h1.shape
    tm = _pick(n, (256, 128, 64, 32, 16, 8))
    return pl.pallas_call(
        _combine_kernel,
        grid=(n // tm,),
        in_specs=[pl.BlockSpec((tm, d), lambda i: (i, 0)), pl.BlockSpec((1, tm, d // 2), lambda i: (0, i, 0)),
                  pl.BlockSpec((1, tm, d // 2), lambda i: (1, i, 0)), pl.BlockSpec((tm, LANES), lambda i: (i, 0))],
        out_specs=pl.BlockSpec((tm, d), lambda i: (i, 0)),
        out_shape=jax.ShapeDtypeStruct((n, d), jnp.float32),
        compiler_params=_cparams(("parallel",)),
        name="moe_combine",
    )(h1, y2, y2, gate_slab)


def _moe(h1, g_ffn, w_rg, b_rg, w_re, b_re, w_gate, w_up, w_down):
    n, d = h1.shape
    w_r = jnp.zeros((d, LANES), jnp.float32).at[:, :N_GROUPS].set(w_rg).at[:, N_GROUPS:N_GROUPS + N_EXPERTS].set(w_re)
    b_r = jnp.zeros((1, LANES), jnp.float32).at[0, :N_GROUPS].set(b_rg).at[0, N_GROUPS:N_GROUPS + N_EXPERTS].set(b_re)
    xn, eid_slab, gate_slab = _router(h1, g_ffn, w_r, b_r)
    flat = eid_slab[:, :EXPERT_TOPK].reshape(n * EXPERT_TOPK)
    onehot = (flat[:, None] == jnp.arange(N_EXPERTS, dtype=jnp.int32)[None, :]).astype(jnp.int32)
    rank = jnp.take_along_axis(jnp.cumsum(onehot, axis=0) - onehot, flat[:, None], axis=1)[:, 0]
    counts = jnp.sum(onehot, axis=0)
    n_blk = (counts + MOE_TM - 1) // MOE_TM
    start_blk = jnp.cumsum(n_blk) - n_blk
    total_blocks = jnp.sum(n_blk)
    slot = start_blk[flat] * MOE_TM + rank
    n_tb_max = -(-(n * EXPERT_TOPK) // MOE_TM) + N_EXPERTS
    a_pad = n_tb_max * MOE_TM
    src_tok = jnp.zeros((a_pad,), jnp.int32).at[slot].set(jnp.arange(n * EXPERT_TOPK, dtype=jnp.int32) // EXPERT_TOPK)
    end_blk = jnp.cumsum(n_blk)
    owner = jnp.sum((end_blk[None, :] <= jnp.arange(n_tb_max, dtype=jnp.int32)[:, None]).astype(jnp.int32), axis=1)
    block_e = jnp.minimum(owner, N_EXPERTS - 1).astype(jnp.int32)
    xs = _gather_rows(src_tok, xn)
    yb = _experts(xs, w_gate, w_up, w_down, block_e, start_blk.astype(jnp.int32), n_blk.astype(jnp.int32),
                  total_blocks.astype(jnp.int32))
    back = slot.reshape(n, EXPERT_TOPK).T.reshape(n * EXPERT_TOPK).astype(jnp.int32)
    y2 = _gather_rows(back, yb).reshape(EXPERT_TOPK, n, d // 2)
    return _combine(h1, y2, gate_slab)


def _rope_tables(pos):
    half = HEAD_DIM // 2
    inv = jnp.exp(-math.log(ROPE_THETA) * jnp.arange(half, dtype=jnp.float32) * (2.0 / HEAD_DIM))
    ang = pos.astype(jnp.float32)[:, None] * inv[None, :]
    cos, sin = jnp.cos(ang), jnp.sin(ang)
    return jnp.concatenate([cos, cos], axis=1), jnp.concatenate([-sin, sin], axis=1)


def _repack_weights(w_in, b_gate, b_forget, g_nsa_q, g_nsa_k, g_fox_q, g_fox_k):
    o1 = NSA_HEADS * HEAD_DIM
    o2 = o1 + N_BRANCH * KV_COLS
    o3 = o2 + N_GATE
    o4 = o3 + FOX_HEADS * HEAD_DIM
    o5 = o4 + KV_COLS
    d = w_in.shape[0]
    w_main = jnp.concatenate([w_in[:, :o2], w_in[:, o3:o5]], axis=1).astype(jnp.bfloat16)
    n_f = w_in.shape[1] - o5
    w_small = jnp.zeros((d, LANES), jnp.float32).at[:, :N_GATE].set(w_in[:, o2:o3])
    w_small = w_small.at[:, COL_LOGF:COL_LOGF + n_f].set(w_in[:, o5:]).astype(jnp.bfloat16)
    b_small = jnp.zeros((1, LANES), jnp.float32).at[0, :N_GATE].set(b_gate.reshape(N_GATE))
    b_small = b_small.at[0, COL_LOGF:COL_LOGF + n_f].set(b_forget)
    ones = jnp.ones((KV_HEADS * HEAD_DIM,), jnp.float32)
    gains = [jnp.tile(g_nsa_q, NSA_HEADS)]
    for br in range(N_BRANCH):
        gains += [jnp.tile(g_nsa_k[br], KV_HEADS), ones]
    gains += [jnp.tile(g_fox_q, FOX_HEADS), jnp.tile(g_fox_k, KV_HEADS), ones]
    return w_main, w_small, b_small, jnp.concatenate(gains).reshape(1, MAIN_COLS)


def _cmp_weights(w_k, w_v, pe_k, pe_v):
    def cat(w):
        return jnp.concatenate([w[:CMP_STRIDE], w[CMP_STRIDE:]], axis=2)
    return jnp.stack([cat(w_k), cat(w_v)]).astype(jnp.bfloat16), jnp.stack([pe_k, pe_v])


def _pad_rows(x, rows):
    return jnp.pad(x, ((0, 0), (0, rows - x.shape[1]), (0, 0)))


def kernel(x_prompt, x_sample, cache_nsa_cmp_kv, cache_nsa_slc_kv, cache_nsa_win_kv, cache_fox_kv, cache_fox_logf, page_table, g_attn_norm, w_in, b_nsa_gate, b_fox_forget, g_nsa_q, g_nsa_k, g_fox_q, g_fox_k, w_cmp_k, w_cmp_v, pe_cmp_k, pe_cmp_v, g_out_nsa, g_out_fox, w_out, g_ffn_norm, w_router_grp, b_router_grp, w_router_exp, b_router_exp, w_exp_gate, w_exp_up, w_exp_down):
    depth = w_in.shape[0]
    assert depth == 1, "single-layer step"
    bp, t, d = x_prompt.shape
    bs, tn, _ = x_sample.shape
    n_pages = page_table.shape[1]
    page = cache_fox_kv.shape[2]
    past = n_pages * page
    n_p, n_s = bp * t, bs * tn
    l = 0

    x_all = jnp.concatenate([x_prompt.reshape(n_p, d), x_sample.reshape(n_s, d)], axis=0)
    pos_all = jnp.concatenate([jnp.tile(jnp.arange(t, dtype=jnp.int32), bp),
                               jnp.tile(past + jnp.arange(tn, dtype=jnp.int32), bs)])
    cos_f, sin_s = _rope_tables(pos_all)
    w_main, w_small, b_small, gain_cols = _repack_weights(
        w_in[l], b_nsa_gate[l], b_fox_forget[l], g_nsa_q[l], g_nsa_k[l], g_fox_q[l], g_fox_k[l])

    xn = _rmsnorm(x_all, g_attn_norm[l], jnp.bfloat16)
    hf, hb = _inproj(xn, w_main, gain_cols, cos_f, sin_s)
    slab = _small_proj(xn, w_small, b_small)

    col_cmp, col_slc, col_win = (COL_KV_NSA + br * KV_COLS for br in range(N_BRANCH))
    w_cat, pe_cat = _cmp_weights(w_cmp_k[l], w_cmp_v[l], pe_cmp_k[l], pe_cmp_v[l])
    v_off = KV_HEADS * HEAD_DIM

    tq = _pick(t, (256, 128))
    kc_p = _compress_prompt(hf, bp, t, col_cmp, w_cat, pe_cat)
    n_slc_p = max(-(-t // SLC_BLOCK), SLC_TOPK)
    nsp_p = -(-n_slc_p // LANES) * LANES if n_slc_p > 64 else 64
    o_cmp_p, sel_p = _cmp_select(hb, 0, bp, t // tq, tq, kc_p, 0, t // CMP_STRIDE - CMP_RATIO + 1, n_slc_p, nsp_p,
                                 jnp.bfloat16)
    o_slc_p = _flash_prompt("slc", hb, bp, t, COL_Q_NSA, col_slc, col_slc + v_off, sel=sel_p)
    o_win_p = _win_prompt(hb, bp, t, COL_Q_NSA, col_win, col_win + v_off)
    c_p = _cumsum_prompt(slab[:n_p], bp, t)[:, COL_LOGF:COL_LOGF + FOX_HEADS]
    cq_p = c_p.reshape(n_p, KV_HEADS, REP).transpose(1, 0, 2)
    ck_p = c_p.T.reshape(KV_HEADS, REP, n_p)
    o_fox_p = _flash_prompt("fox", hb, bp, t, COL_Q_FOX, COL_KV_FOX, COL_KV_FOX + v_off, cq=cq_p, ck=ck_p)
    a_p = _merge(o_cmp_p, o_slc_p, o_win_p, o_fox_p, slab[:n_p], g_out_nsa[l], g_out_fox[l])

    hf_s, slab_s = hf[n_p:], slab[n_p:]
    q_nsa_s = hf_s[:, COL_Q_NSA:COL_Q_NSA + NSA_HEADS * HEAD_DIM]
    q_fox_s = hf_s[:, COL_Q_FOX:COL_Q_FOX + FOX_HEADS * HEAD_DIM]

    def new_rows(col):
        rows = _pad_rows(hf_s[:, col:col + KV_COLS].reshape(bs, tn, KV_COLS), page)
        return rows.reshape(bs, page * ROW_SLABS, HEAD_DIM)

    pool = lambda c: c[l].reshape(c.shape[1], page * ROW_SLABS, HEAD_DIM)
    kc_s = _compress_sample(page_table, pool(cache_nsa_cmp_kv), new_rows(col_cmp), w_cat, pe_cat)
    t_ctx = past + tn
    n_cmp_s = -(-t_ctx // CMP_STRIDE) - CMP_RATIO + 1
    n_slc_s = max(-(-t_ctx // SLC_BLOCK), SLC_TOPK)
    blocks_per_step = PAGES_PER_STEP * page // SLC_BLOCK
    nsp_s = -(-(n_slc_s + 1) // LANES) * LANES
    nsp_s = -(-nsp_s // blocks_per_step) * blocks_per_step
    o_cmp_s, sel_s = _cmp_select(q_nsa_s, 0, bs, 1, tn, kc_s, past, n_cmp_s, n_slc_s, nsp_s, jnp.float32)
    sel5 = sel_s.reshape(KV_HEADS, bs, tn, nsp_s // blocks_per_step, blocks_per_step).transpose(1, 3, 0, 2, 4)
    o_slc_s = _paged_attn("slc", page_table, pool(cache_nsa_slc_kv), new_rows(col_slc), q_nsa_s, tn, sel5=sel5)
    win_buf = cache_nsa_win_kv[l].reshape(bs, -1, HEAD_DIM)
    o_win_s = _win_sample(win_buf, new_rows(col_win), q_nsa_s, tn, past)
    logf_s = slab_s[:, COL_LOGF:COL_LOGF + FOX_HEADS].reshape(bs, tn, FOX_HEADS)
    ck_s = _cumsum_sample(page_table, cache_fox_logf[l].transpose(0, 2, 1),
                          _pad_rows(logf_s, page).transpose(0, 2, 1))
    cq_s = ck_s[:, :, past:past + tn].transpose(0, 2, 1)
    o_fox_s = _paged_attn("fox", page_table, pool(cache_fox_kv), new_rows(COL_KV_FOX), q_fox_s, tn, cq=cq_s, ck=ck_s)
    a_s = _merge(o_cmp_s, o_slc_s, o_win_s, o_fox_s, slab_s, g_out_nsa[l], g_out_fox[l])

    h1 = _outproj(jnp.concatenate([a_p, a_s], axis=0), w_out[l].astype(jnp.bfloat16), x_all)
    y = _moe(h1, g_ffn_norm[l], w_router_grp[l], b_router_grp[l], w_router_exp[l], b_router_exp[l],
             w_exp_gate[l], w_exp_up[l], w_exp_down[l])

    def kv_out(rows, col, bsz, tt):
        return rows[:, col:col + KV_COLS].reshape(1, bsz, tt, 2, KV_HEADS, HEAD_DIM)

    hf_p = hf[:n_p]
    win_p = kv_out(hf_p, col_win, bp, t)[:, :, t - min(WINDOW, t):]
    win_new = kv_out(hf_s, col_win, bs, tn)
    win_s = jnp.concatenate([cache_nsa_win_kv[l:l + 1][:, :, tn:], win_new], axis=2)
    logf_p = slab[:n_p, COL_LOGF:COL_LOGF + FOX_HEADS].reshape(1, bp, t, FOX_HEADS)
    return (y[:n_p].reshape(bp, t, d), y[n_p:].reshape(bs, tn, d),
            kv_out(hf_p, col_cmp, bp, t), kv_out(hf_s, col_cmp, bs, tn),
            kv_out(hf_p, col_slc, bp, t), kv_out(hf_s, col_slc, bs, tn),
            win_p, win_s,
            kv_out(hf_p, COL_KV_FOX, bp, t), kv_out(hf_s, COL_KV_FOX, bs, tn),
            logf_p, logf_s.reshape(1, bs, tn, FOX_HEADS))
```

```python
import functools
import math

import jax
import jax.numpy as jnp
from jax import lax
from jax.experimental import pallas as pl
from jax.experimental.pallas import tpu as pltpu
from jax.experimental.pallas import tpu_sc as plsc

HEAD_DIM = 128
NSA_HEADS = 16
FOX_HEADS = 16
KV_HEADS = 4
REP = NSA_HEADS // KV_HEADS
N_BRANCH = 3
CMP_BLOCK = 32
CMP_STRIDE = 16
CMP_RATIO = CMP_BLOCK // CMP_STRIDE
SLC_BLOCK = 64
SLC_TOPK = 16
WINDOW = 512
ROPE_THETA = 10000.0
RMS_EPS = 1e-6
N_GROUPS = 4
EXPERTS_PER_GROUP = 8
N_EXPERTS = N_GROUPS * EXPERTS_PER_GROUP
EXPERT_TOPK = 2
ATTN_SCALE = HEAD_DIM ** -0.5

LANES = 128
GROUP_COLS = REP * HEAD_DIM
KV_COLS = 2 * KV_HEADS * HEAD_DIM
ROW_SLABS = 2 * KV_HEADS
NEG = -1e30
VMEM_LIMIT = 48 * 1024 * 1024
PAGES_PER_STEP = 8

COL_Q_NSA = 0
COL_KV_NSA = NSA_HEADS * HEAD_DIM
COL_Q_FOX = COL_KV_NSA + N_BRANCH * KV_COLS
COL_KV_FOX = COL_Q_FOX + FOX_HEADS * HEAD_DIM
MAIN_COLS = COL_KV_FOX + KV_COLS
N_GATE = N_BRANCH * NSA_HEADS
COL_LOGF = N_GATE


def _pick(n, cands):
    for c in cands:
        if n % c == 0:
            return c
    raise ValueError(f"no tile in {cands} divides {n}")


def _cparams(sem, vmem=VMEM_LIMIT):
    return pltpu.CompilerParams(dimension_semantics=sem, vmem_limit_bytes=vmem)


def _split3(x):
    hi = x.astype(jnp.bfloat16)
    r1 = x - hi.astype(jnp.float32)
    mid = r1.astype(jnp.bfloat16)
    lo = (r1 - mid.astype(jnp.float32)).astype(jnp.bfloat16)
    return hi, mid, lo


def _dot(a, b):
    return jnp.dot(a, b, preferred_element_type=jnp.float32)


def _dot_nt(a, b):
    return lax.dot_general(a, b, (((1,), (1,)), ((), ())), preferred_element_type=jnp.float32)


def _rms_kernel(x_ref, g_ref, o_ref):
    x = x_ref[...]
    ms = jnp.mean(x * x, axis=-1, keepdims=True)
    o_ref[...] = (x * lax.rsqrt(ms + RMS_EPS) * g_ref[...]).astype(o_ref.dtype)


def _rmsnorm(x, g, out_dtype):
    n, d = x.shape
    tm = _pick(n, (256, 128, 64, 32, 16, 8))
    return pl.pallas_call(
        _rms_kernel,
        grid=(n // tm,),
        in_specs=[pl.BlockSpec((tm, d), lambda i: (i, 0)), pl.BlockSpec((1, d), lambda i: (0, 0))],
        out_specs=pl.BlockSpec((tm, d), lambda i: (i, 0)),
        out_shape=jax.ShapeDtypeStruct((n, d), out_dtype),
        compiler_params=_cparams(("parallel",)),
        name="rmsnorm",
    )(x, g.reshape(1, d))


N_COLBLK = MAIN_COLS // GROUP_COLS


def _colblock_kinds():
    kinds = []
    kinds += ["rope"] * (NSA_HEADS // REP)
    for _ in range(N_BRANCH):
        kinds += ["rope", "id"]
    kinds += ["norm"] * (FOX_HEADS // REP)
    kinds += ["norm", "id"]
    assert len(kinds) == N_COLBLK
    return kinds


def _any_eq(j, vals):
    return functools.reduce(jnp.logical_or, [j == v for v in vals])


def _inproj_kernel(x_ref, w_ref, gain_ref, cos_ref, sin_ref, of_ref, ob_ref):
    j = pl.program_id(1)
    kinds = _colblock_kinds()
    id_blocks = [i for i, k in enumerate(kinds) if k == "id"]
    rope_blocks = [i for i, k in enumerate(kinds) if k == "rope"]
    norm_blocks = [i for i, k in enumerate(kinds) if k == "norm"]
    acc = _dot(x_ref[...], w_ref[...])

    def write(y):
        of_ref[...] = y
        ob_ref[...] = y.astype(ob_ref.dtype)

    def normed(rope):
        outs = []
        for s in range(GROUP_COLS // HEAD_DIM):
            h = acc[:, s * HEAD_DIM:(s + 1) * HEAD_DIM]
            g = gain_ref[:, s * HEAD_DIM:(s + 1) * HEAD_DIM]
            y = h * lax.rsqrt(jnp.mean(h * h, axis=-1, keepdims=True) + RMS_EPS) * g
            if rope:
                y = y * cos_ref[...] + pltpu.roll(y, HEAD_DIM // 2, 1) * sin_ref[...]
            outs.append(y)
        return jnp.concatenate(outs, axis=1)

    @pl.when(_any_eq(j, id_blocks))
    def _():
        write(acc)

    @pl.when(_any_eq(j, norm_blocks))
    def _():
        write(normed(False))

    @pl.when(_any_eq(j, rope_blocks))
    def _():
        write(normed(True))


def _inproj(xn, w_main, gain_cols, cos_f, sin_s):
    n, d = xn.shape
    tm = _pick(n, (768, 512, 384, 256, 128, 64, 32, 16, 8))
    tn = GROUP_COLS
    return pl.pallas_call(
        _inproj_kernel,
        grid=(n // tm, MAIN_COLS // tn),
        in_specs=[
            pl.BlockSpec((tm, d), lambda i, j: (i, 0)),
            pl.BlockSpec((d, tn), lambda i, j: (0, j)),
            pl.BlockSpec((1, tn), lambda i, j: (0, j)),
            pl.BlockSpec((tm, HEAD_DIM), lambda i, j: (i, 0)),
            pl.BlockSpec((tm, HEAD_DIM), lambda i, j: (i, 0)),
        ],
        out_specs=[pl.BlockSpec((tm, tn), lambda i, j: (i, j)), pl.BlockSpec((tm, tn), lambda i, j: (i, j))],
        out_shape=[jax.ShapeDtypeStruct((n, MAIN_COLS), jnp.float32),
                   jax.ShapeDtypeStruct((n, MAIN_COLS), jnp.bfloat16)],
        compiler_params=_cparams(("parallel", "arbitrary")),
        name="inproj",
    )(xn, w_main, gain_cols, cos_f, sin_s)


def _small_kernel(x_ref, w_ref, b_ref, o_ref):
    z = _dot(x_ref[...], w_ref[...]) + b_ref[...]
    lane = lax.broadcasted_iota(jnp.int32, z.shape, 1)
    sig = 1.0 / (1.0 + jnp.exp(-z))
    logsig = jnp.minimum(z, 0.0) - jnp.log1p(jnp.exp(-jnp.abs(z)))
    o_ref[...] = jnp.where(lane < N_GATE, sig, logsig)


def _small_proj(xn, w_small, b_small):
    n, d = xn.shape
    tm = _pick(n, (768, 512, 384, 256, 128, 64, 32, 16, 8))
    return pl.pallas_call(
        _small_kernel,
        grid=(n // tm,),
        in_specs=[pl.BlockSpec((tm, d), lambda i: (i, 0)), pl.BlockSpec((d, LANES), lambda i: (0, 0)),
                  pl.BlockSpec((1, LANES), lambda i: (0, 0))],
        out_specs=pl.BlockSpec((tm, LANES), lambda i: (i, 0)),
        out_shape=jax.ShapeDtypeStruct((n, LANES), jnp.float32),
        compiler_params=_cparams(("parallel",)),
        name="small_proj",
    )(xn, w_small, b_small)


def _tri_lower(n):
    r = lax.broadcasted_iota(jnp.int32, (n, n), 0)
    c = lax.broadcasted_iota(jnp.int32, (n, n), 1)
    return (c <= r).astype(jnp.bfloat16)


def _block_cumsum(x, tri):
    hi, mid, lo = _split3(x)
    return _dot(tri, hi) + _dot(tri, mid) + _dot(tri, lo)


def _cumsum_prompt_kernel(x_ref, o_ref, carry_ref):
    @pl.when(pl.program_id(1) == 0)
    def _():
        carry_ref[...] = jnp.zeros_like(carry_ref)

    blk = x_ref.shape[0]
    c = _block_cumsum(x_ref[...], _tri_lower(blk)) + carry_ref[...]
    o_ref[...] = c
    carry_ref[...] = c[blk - 1:blk, :]


def _cumsum_prompt(slab, b, t):
    blk = _pick(t, (512, 256, 128))
    nb = t // blk
    return pl.pallas_call(
        _cumsum_prompt_kernel,
        grid=(b, nb),
        in_specs=[pl.BlockSpec((blk, LANES), lambda i, j: (i * nb + j, 0))],
        out_specs=pl.BlockSpec((blk, LANES), lambda i, j: (i * nb + j, 0)),
        out_shape=jax.ShapeDtypeStruct((b * t, LANES), jnp.float32),
        scratch_shapes=[pltpu.VMEM((1, LANES), jnp.float32)],
        compiler_params=_cparams(("parallel", "arbitrary")),
        name="cumsum_prompt",
    )(slab)


def _cumsum_sample_kernel(pt_ref, *refs, n_steps):
    pages = refs[:PAGES_PER_STEP]
    new_ref, o_ref, carry_ref = refs[PAGES_PER_STEP:]
    s = pl.program_id(1)
    _, h, page = pages[0].shape

    @pl.when(s == 0)
    def _():
        carry_ref[...] = jnp.zeros_like(carry_ref)

    def local(x):
        r = lax.broadcasted_iota(jnp.int32, (page, page), 0)
        c = lax.broadcasted_iota(jnp.int32, (page, page), 1)
        tri = (r <= c).astype(jnp.bfloat16)
        hi, mid, lo = _split3(x)
        return _dot(hi, tri) + _dot(mid, tri) + _dot(lo, tri)

    @pl.when(s < n_steps)
    def _():
        loc = local(jnp.concatenate([p[0] for p in pages], axis=0))
        carry = carry_ref[...]
        for p in range(PAGES_PER_STEP):
            blk = loc[p * h:(p + 1) * h] + carry
            o_ref[0, :, p * page:(p + 1) * page] = blk
            carry = blk[:, page - 1:page]
        carry_ref[...] = carry

    @pl.when(s == n_steps)
    def _():
        o_ref[0, :, 0:page] = local(new_ref[0]) + carry_ref[...]
        o_ref[0, :, page:] = jnp.zeros((h, (PAGES_PER_STEP - 1) * page), jnp.float32)


def _cumsum_sample(page_table, pool_t, new_t):
    bsz, n_pages = page_table.shape
    _, h, page = pool_t.shape
    n_steps = n_pages // PAGES_PER_STEP
    width = PAGES_PER_STEP * page

    def page_spec(k):
        return pl.BlockSpec(
            (1, h, page), lambda b, s, pt: (pt[b, jnp.minimum(s, n_steps - 1) * PAGES_PER_STEP + k], 0, 0))

    kern = functools.partial(_cumsum_sample_kernel, n_steps=n_steps)
    return pl.pallas_call(
        kern,
        grid_spec=pltpu.PrefetchScalarGridSpec(
            num_scalar_prefetch=1,
            grid=(bsz, n_steps + 1),
            in_specs=[page_spec(k) for k in range(PAGES_PER_STEP)] + [
                pl.BlockSpec((1, h, page), lambda b, s, pt: (b, 0, 0))],
            out_specs=pl.BlockSpec((1, h, width), lambda b, s, pt: (b, 0, s)),
            scratch_shapes=[pltpu.VMEM((h, 1), jnp.float32)],
        ),
        out_shape=jax.ShapeDtypeStruct((bsz, h, (n_steps + 1) * width), jnp.float32),
        compiler_params=_cparams(("parallel", "arbitrary")),
        name="cumsum_sample",
    )(page_table, *([pool_t] * PAGES_PER_STEP), new_t)


def _pe_term(pe_ref, w_ref):
    acc = jnp.zeros((16, HEAD_DIM), jnp.float32)
    for j in range(CMP_STRIDE):
        lo = jnp.broadcast_to(pe_ref[j:j + 1, :], (16, HEAD_DIM)).astype(jnp.bfloat16)
        hi = jnp.broadcast_to(pe_ref[CMP_STRIDE + j:CMP_STRIDE + j + 1, :], (16, HEAD_DIM)).astype(jnp.bfloat16)
        w = w_ref[j]
        acc = acc + _dot(lo, w[:, :HEAD_DIM]) + _dot(hi, w[:, HEAD_DIM:])
    return acc[0:1]


def _compress_prompt_kernel(x_ref, w_ref, pe_ref, o_ref, *, n_chunks):
    acc = jnp.zeros((n_chunks, 2 * HEAD_DIM), jnp.float32)
    for j in range(CMP_STRIDE):
        xj = x_ref[pl.ds(j, n_chunks, stride=CMP_STRIDE), :].astype(jnp.bfloat16)
        acc = acc + _dot(xj, w_ref[0, j])
    nxt = pltpu.roll(acc[:, HEAD_DIM:], n_chunks - 1, 0)
    o_ref[0, 0, 0] = (acc[:, :HEAD_DIM] + nxt + _pe_term(pe_ref.at[0], w_ref.at[0])).astype(o_ref.dtype)


def _compress_prompt(hf, b, t, col0, w_cat, pe):
    n_chunks = t // CMP_STRIDE
    rows_blk = col0 // HEAD_DIM
    kern = functools.partial(_compress_prompt_kernel, n_chunks=n_chunks)
    return pl.pallas_call(
        kern,
        grid=(b, 2, KV_HEADS),
        in_specs=[
            pl.BlockSpec((t, HEAD_DIM), lambda i, kv, g: (i, rows_blk + kv * KV_HEADS + g)),
            pl.BlockSpec((1, CMP_STRIDE, HEAD_DIM, 2 * HEAD_DIM), lambda i, kv, g: (kv, 0, 0, 0)),
            pl.BlockSpec((1, CMP_BLOCK, HEAD_DIM), lambda i, kv, g: (kv, 0, 0)),
        ],
        out_specs=pl.BlockSpec((1, 1, 1, n_chunks, HEAD_DIM), lambda i, kv, g: (i, kv, g, 0, 0)),
        out_shape=jax.ShapeDtypeStruct((b, 2, KV_HEADS, n_chunks, HEAD_DIM), jnp.bfloat16),
        compiler_params=_cparams(("parallel", "parallel", "parallel")),
        name="compress_prompt",
    )(hf, w_cat, pe)


def _compress_sample_kernel(pt_ref, *refs, n_steps):
    pages = refs[:PAGES_PER_STEP]
    next_ref, new_ref, w_ref, pe_ref, o_ref = refs[PAGES_PER_STEP:]
    s = pl.program_id(1)
    chunk_rows = CMP_STRIDE * ROW_SLABS
    cpp = pages[0].shape[1] // chunk_rows
    n_out = PAGES_PER_STEP * cpp * ROW_SLABS
    is_last = s == n_steps - 1

    def chunk_lhs(ref, c):
        base = c * chunk_rows
        return jnp.concatenate([ref[0, base + j * ROW_SLABS:base + (j + 1) * ROW_SLABS, :]
                                for j in range(CMP_STRIDE)], axis=1)

    lhs = [chunk_lhs(p, c) for p in pages for c in range(cpp)]
    lhs.append(jnp.where(is_last, chunk_lhs(new_ref, 0), chunk_lhs(next_ref, 0)))
    out = _dot(jnp.concatenate(lhs, axis=0).astype(jnp.bfloat16), w_ref[...])
    pe = _dot(pe_ref[...].astype(jnp.bfloat16), w_ref[...])
    own, nxt = out[:n_out], out[ROW_SLABS:n_out + ROW_SLABS]
    d = HEAD_DIM
    is_k = (lax.broadcasted_iota(jnp.int32, (n_out, d), 0) & (ROW_SLABS - 1)) < KV_HEADS
    pe_k = pe[0:1, 0:d] + pe[1:2, d:2 * d]
    pe_v = pe[2:3, 2 * d:3 * d] + pe[3:4, 3 * d:4 * d]
    o_ref[0] = jnp.where(is_k, own[:, 0:d] + nxt[:, d:2 * d] + pe_k, own[:, 2 * d:3 * d] + nxt[:, 3 * d:4 * d] + pe_v)


def _compress_sample(page_table, pool, new_pad, w_cat, pe):
    bsz, n_pages = page_table.shape
    _, page, cols = pool.shape
    assert n_pages % PAGES_PER_STEP == 0
    n_steps = n_pages // PAGES_PER_STEP
    chunk_rows = CMP_STRIDE * ROW_SLABS
    cps = PAGES_PER_STEP * page // chunk_rows
    w_flat = jnp.concatenate([w_cat[0].reshape(CMP_STRIDE * HEAD_DIM, 2 * HEAD_DIM),
                              w_cat[1].reshape(CMP_STRIDE * HEAD_DIM, 2 * HEAD_DIM)], axis=1)
    pe_lhs = jnp.pad(pe.reshape(2 * CMP_RATIO, CMP_STRIDE * HEAD_DIM), ((0, ROW_SLABS - 2 * CMP_RATIO), (0, 0)))

    def page_spec(k):
        return pl.BlockSpec((1, page, cols), lambda b, s, pt: (pt[b, s * PAGES_PER_STEP + k], 0, 0))

    next_spec = pl.BlockSpec(
        (1, chunk_rows, cols), lambda b, s, pt: (pt[b, jnp.minimum((s + 1) * PAGES_PER_STEP, n_pages - 1)], 0, 0))
    kern = functools.partial(_compress_sample_kernel, n_steps=n_steps)
    out = pl.pallas_call(
        kern,
        grid_spec=pltpu.PrefetchScalarGridSpec(
            num_scalar_prefetch=1,
            grid=(bsz, n_steps),
            in_specs=[page_spec(k) for k in range(PAGES_PER_STEP)] + [
                next_spec,
                pl.BlockSpec((1, chunk_rows, cols), lambda b, s, pt: (b, 0, 0)),
                pl.BlockSpec(w_flat.shape, lambda b, s, pt: (0, 0)),
                pl.BlockSpec(pe_lhs.shape, lambda b, s, pt: (0, 0)),
            ],
            out_specs=pl.BlockSpec((1, cps * ROW_SLABS, HEAD_DIM), lambda b, s, pt: (b, s, 0)),
        ),
        out_shape=jax.ShapeDtypeStruct((bsz, n_steps * cps * ROW_SLABS, HEAD_DIM), jnp.float32),
        compiler_params=_cparams(("parallel", "arbitrary")),
        name="compress_sample",
    )(page_table, *([pool] * (PAGES_PER_STEP + 1)), new_pad, w_flat, pe_lhs)
    out = out.reshape(bsz, n_steps * cps, 2, KV_HEADS, HEAD_DIM).transpose(0, 2, 3, 1, 4)
    return out.astype(jnp.bfloat16)


def _stack_heads(q):
    return jnp.concatenate([q[:, r * HEAD_DIM:(r + 1) * HEAD_DIM] for r in range(REP)], axis=0).astype(jnp.bfloat16)


def _div_pow2(x, c):
    assert c & (c - 1) == 0
    return lax.shift_right_arithmetic(x, jnp.int32(c.bit_length() - 1))


def _unstack_heads(o, tq):
    return jnp.concatenate([o[r * tq:(r + 1) * tq] for r in range(REP)], axis=1)


RANK_GROUP = 8


def _cmp_select_kernel(q_ref, kc_ref, vc_ref, ov_ref, o_ref, sel_ref, rank_ref, *, tq, pos_base, n_cmp_valid, n_slc):
    qi = pl.program_id(2)
    nc = kc_ref.shape[3]
    nsp = ov_ref.shape[1]
    q = _stack_heads(q_ref[...])
    s = _dot_nt(q, kc_ref[0, 0, 0]) * ATTN_SCALE
    assert tq & (tq - 1) == 0
    t1 = pos_base + qi * tq + (lax.broadcasted_iota(jnp.int32, (REP * tq, nc), 0) & (tq - 1))
    n1 = lax.broadcasted_iota(jnp.int32, (REP * tq, nc), 1)
    vis = (n1 * CMP_STRIDE + CMP_BLOCK - 1 <= t1) & (n1 < n_cmp_valid)
    s = jnp.where(vis, s, -jnp.inf)
    m = jnp.max(s, axis=-1, keepdims=True)
    m = jnp.where(m == -jnp.inf, 0.0, m)
    p = jnp.exp(s - m)
    p = p / jnp.maximum(jnp.sum(p, axis=-1, keepdims=True), jnp.finfo(jnp.float32).tiny)
    o = _dot(p.astype(jnp.bfloat16), vc_ref[0, 0, 0])
    o_ref[...] = _unstack_heads(o, tq)

    psum = p[0:tq]
    for r in range(1, REP):
        psum = psum + p[r * tq:(r + 1) * tq]
    hi, mid, lo = _split3(psum)
    ov = ov_ref[...]
    imp = _dot(hi, ov) + _dot(mid, ov) + _dot(lo, ov)
    t = pos_base + qi * tq + lax.broadcasted_iota(jnp.int32, (tq, nsp), 0)
    jj = lax.broadcasted_iota(jnp.int32, (tq, nsp), 1)
    cur = _div_pow2(t, SLC_BLOCK)
    forced = (jj == 0) | (jj == cur) | (jj == cur - 1)
    causal = (jj * SLC_BLOCK <= t) & (jj < n_slc)
    imp = jnp.where(forced, jnp.inf, imp)
    imp = jnp.where(causal, imp, -jnp.inf)
    rank_ref[...] = jnp.zeros_like(rank_ref)
    t_last = pos_base + qi * tq + tq - 1
    for i0 in range(0, n_slc, RANK_GROUP):
        @pl.when(i0 * SLC_BLOCK <= t_last)
        def _():
            part = jnp.zeros((tq, nsp), jnp.float32)
            for i in range(i0, min(i0 + RANK_GROUP, n_slc)):
                col = imp[:, i:i + 1]
                beats = (col > imp) | ((col == imp) & (jj > i))
                part = part + beats.astype(jnp.float32)
            rank_ref[...] += part
    sel = (rank_ref[...] < SLC_TOPK) & (imp > -jnp.inf)
    sel_ref[0] = sel.astype(sel_ref.dtype)


def _overlap_matrix(nc, nsp):
    cs = jnp.arange(nc) * CMP_STRIDE
    ss = jnp.arange(nsp) * SLC_BLOCK
    ov = (cs[:, None] < ss[None, :] + SLC_BLOCK) & (cs[:, None] + CMP_BLOCK > ss[None, :])
    return ov.astype(jnp.bfloat16)


def _cmp_select(q_arr, row_blk0, bsz, nq, tq, kc, pos_base, n_cmp_valid, n_slc, nsp, sel_dtype):
    nc = kc.shape[3]
    rows = bsz * nq * tq
    kern = functools.partial(_cmp_select_kernel, tq=tq, pos_base=pos_base, n_cmp_valid=n_cmp_valid, n_slc=n_slc)
    return pl.pallas_call(
        kern,
        grid=(bsz, KV_HEADS, nq),
        in_specs=[
            pl.BlockSpec((tq, GROUP_COLS), lambda b, g, i: (row_blk0 + b * nq + i, g)),
            pl.BlockSpec((1, 1, 1, nc, HEAD_DIM), lambda b, g, i: (b, 0, g, 0, 0)),
            pl.BlockSpec((1, 1, 1, nc, HEAD_DIM), lambda b, g, i: (b, 1, g, 0, 0)),
            pl.BlockSpec((nc, nsp), lambda b, g, i: (0, 0)),
        ],
        out_specs=[
            pl.BlockSpec((tq, GROUP_COLS), lambda b, g, i: (b * nq + i, g)),
            pl.BlockSpec((1, tq, nsp), lambda b, g, i: (g, b * nq + i, 0)),
        ],
        out_shape=[jax.ShapeDtypeStruct((rows, NSA_HEADS * HEAD_DIM), jnp.float32),
                   jax.ShapeDtypeStruct((KV_HEADS, rows, nsp), sel_dtype)],
        scratch_shapes=[pltpu.VMEM((tq, nsp), jnp.float32)],
        compiler_params=_cparams(("parallel", "parallel", "parallel")),
        name="cmp_select",
    )(q_arr, kc, kc, _overlap_matrix(nc, nsp))


def _online_update(s, v, m_ref, l_ref, acc_ref, rows=slice(None)):
    m_prev = m_ref[rows]
    m_new = jnp.maximum(m_prev, jnp.max(s, axis=-1, keepdims=True))
    alpha = jnp.exp(m_prev - m_new)
    p = jnp.exp(s - m_new)
    l_ref[rows] = alpha * l_ref[rows] + jnp.sum(p, axis=-1, keepdims=True)
    acc_ref[rows] = alpha * acc_ref[rows] + _dot(p.astype(jnp.bfloat16), v)
    m_ref[rows] = m_new


def _flash_kernel(qi_ref, kk_ref, *refs, mode, tq, tk):
    if mode == "slc":
        q_ref, k_ref, v_ref, sel_ref, o_ref, m_ref, l_ref, acc_ref = refs
    else:
        q_ref, k_ref, v_ref, cq_ref, ck_ref, o_ref, m_ref, l_ref, acc_ref = refs
    step = pl.program_id(2)
    qi = qi_ref[step]
    kk = kk_ref[step]

    @pl.when(kk == 0)
    def _():
        m_ref[...] = jnp.full_like(m_ref, NEG)
        l_ref[...] = jnp.zeros_like(l_ref)
        acc_ref[...] = jnp.zeros_like(acc_ref)

    q = _stack_heads(q_ref[...])
    s = _dot_nt(q, k_ref[...]) * ATTN_SCALE
    t = qi * tq + lax.broadcasted_iota(jnp.int32, (tq, tk), 0)
    kp = kk * tk + lax.broadcasted_iota(jnp.int32, (tq, tk), 1)
    mask = kp <= t
    if mode == "slc":
        nsb = sel_ref.shape[2]
        jb = lax.broadcasted_iota(jnp.int32, (nsb, tk), 0)
        kb = kk * (tk // SLC_BLOCK) + _div_pow2(lax.broadcasted_iota(jnp.int32, (nsb, tk), 1), SLC_BLOCK)
        expand = (jb == kb).astype(jnp.bfloat16)
        mask = mask & (_dot(sel_ref[0], expand) > 0.5)
    add = jnp.where(mask, 0.0, NEG)
    if mode == "fox":
        add = jnp.concatenate([(cq_ref[0][:, r:r + 1] - ck_ref[0][r:r + 1, :]) + add for r in range(REP)], axis=0)
    else:
        add = jnp.concatenate([add] * REP, axis=0)
    _online_update(s + add, v_ref[...], m_ref, l_ref, acc_ref)

    @pl.when(kk == ((qi + 1) * tq - 1) // tk)
    def _():
        o_ref[...] = _unstack_heads(acc_ref[...] / l_ref[...], tq)


def _flash_prompt(mode, hb, b, t, q_col, k_col, v_col, sel=None, cq=None, ck=None):
    tq = _pick(t, (256, 128))
    tk = _pick(t, (512, 256, 128))
    nq = t // tq
    nkt = t // tk
    qb, kb, vb = q_col // GROUP_COLS, k_col // HEAD_DIM, v_col // HEAD_DIM
    pairs = [(i, k) for i in range(nq) for k in range(((i + 1) * tq - 1) // tk + 1)]
    qi_of = jnp.asarray([p[0] for p in pairs], jnp.int32)
    kk_of = jnp.asarray([p[1] for p in pairs], jnp.int32)

    in_specs = [
        pl.BlockSpec((tq, GROUP_COLS), lambda bi, g, s, qi, kk: (bi * nq + qi[s], qb + g)),
        pl.BlockSpec((tk, HEAD_DIM), lambda bi, g, s, qi, kk: (bi * nkt + kk[s], kb + g)),
        pl.BlockSpec((tk, HEAD_DIM), lambda bi, g, s, qi, kk: (bi * nkt + kk[s], vb + g)),
    ]
    args = [hb, hb, hb]
    if mode == "slc":
        in_specs.append(pl.BlockSpec((1, tq, sel.shape[2]), lambda bi, g, s, qi, kk: (g, bi * nq + qi[s], 0)))
        args.append(sel)
    if mode == "fox":
        in_specs.append(pl.BlockSpec((1, tq, REP), lambda bi, g, s, qi, kk: (g, bi * nq + qi[s], 0)))
        in_specs.append(pl.BlockSpec((1, REP, tk), lambda bi, g, s, qi, kk: (g, 0, bi * nkt + kk[s])))
        args += [cq, ck]
    kern = functools.partial(_flash_kernel, mode=mode, tq=tq, tk=tk)
    return pl.pallas_call(
        kern,
        grid_spec=pltpu.PrefetchScalarGridSpec(
            num_scalar_prefetch=2,
            grid=(b, KV_HEADS, len(pairs)),
            in_specs=in_specs,
            out_specs=pl.BlockSpec((tq, GROUP_COLS), lambda bi, g, s, qi, kk: (bi * nq + qi[s], g)),
            scratch_shapes=[pltpu.VMEM((REP * tq, 1), jnp.float32), pltpu.VMEM((REP * tq, 1), jnp.float32),
                            pltpu.VMEM((REP * tq, HEAD_DIM), jnp.float32)],
        ),
        out_shape=jax.ShapeDtypeStruct((b * t, KV_HEADS * GROUP_COLS), jnp.float32),
        compiler_params=_cparams(("parallel", "parallel", "arbitrary")),
        name="flash_" + mode,
    )(qi_of, kk_of, *args)


def _win_prompt_kernel(q_ref, *refs, tq, n_kb):
    k_refs, v_refs, o_ref = refs[:n_kb], refs[n_kb:2 * n_kb], refs[2 * n_kb]
    qi = pl.program_id(2)
    span = n_kb * tq
    k = jnp.concatenate([r[...] for r in k_refs], axis=0)
    v = jnp.concatenate([r[...] for r in v_refs], axis=0)
    s = _dot_nt(_stack_heads(q_ref[...]), k) * ATTN_SCALE
    t = qi * tq + lax.broadcasted_iota(jnp.int32, (tq, span), 0)
    kp = (qi - (n_kb - 1)) * tq + lax.broadcasted_iota(jnp.int32, (tq, span), 1)
    mask = (kp >= 0) & (kp <= t) & (t - kp < WINDOW)
    s = s + jnp.concatenate([jnp.where(mask, 0.0, NEG)] * REP, axis=0)
    p = jnp.exp(s - jnp.max(s, axis=-1, keepdims=True))
    o = _dot(p.astype(jnp.bfloat16), v) / jnp.sum(p, axis=-1, keepdims=True)
    o_ref[...] = _unstack_heads(o, tq)


def _win_prompt(hb, b, t, q_col, k_col, v_col):
    tq = _pick(t, (256, 128))
    assert WINDOW % tq == 0
    n_kb = WINDOW // tq + 1
    nq = t // tq
    qb, kb, vb = q_col // GROUP_COLS, k_col // HEAD_DIM, v_col // HEAD_DIM

    def kv_spec(col_blk, j):
        return pl.BlockSpec((tq, HEAD_DIM),
                            lambda bi, g, i: (bi * nq + jnp.maximum(i - (n_kb - 1) + j, 0), col_blk + g))

    in_specs = ([pl.BlockSpec((tq, GROUP_COLS), lambda bi, g, i: (bi * nq + i, qb + g))]
                + [kv_spec(kb, j) for j in range(n_kb)] + [kv_spec(vb, j) for j in range(n_kb)])
    kern = functools.partial(_win_prompt_kernel, tq=tq, n_kb=n_kb)
    return pl.pallas_call(
        kern,
        grid=(b, KV_HEADS, nq),
        in_specs=in_specs,
        out_specs=pl.BlockSpec((tq, GROUP_COLS), lambda bi, g, i: (bi * nq + i, g)),
        out_shape=jax.ShapeDtypeStruct((b * t, KV_HEADS * GROUP_COLS), jnp.float32),
        compiler_params=_cparams(("parallel", "parallel", "parallel")),
        name="win_prompt",
    )(*([hb] * (1 + 2 * n_kb)))


def _rows_of(g, tn):
    return slice(g * REP * tn, (g + 1) * REP * tn)


def _sample_init(m_ref, l_ref, acc_ref):
    m_ref[...] = jnp.full_like(m_ref, NEG)
    l_ref[...] = jnp.zeros_like(l_ref)
    acc_ref[...] = jnp.zeros_like(acc_ref)


def _sample_finish(o_ref, l_ref, acc_ref, tn):
    res = acc_ref[...] / l_ref[...]
    for g in range(KV_HEADS):
        o_ref[:, g * GROUP_COLS:(g + 1) * GROUP_COLS] = _unstack_heads(res[_rows_of(g, tn)], tn)


def _online_update_groups(scores, adds, values, m_ref, l_ref, acc_ref, tn):
    s = jnp.concatenate(scores, axis=0) * ATTN_SCALE + jnp.concatenate(adds, axis=0)
    m_prev = m_ref[...]
    m_new = jnp.maximum(m_prev, jnp.max(s, axis=-1, keepdims=True))
    alpha = jnp.exp(m_prev - m_new)
    p = jnp.exp(s - m_new)
    l_ref[...] = alpha * l_ref[...] + jnp.sum(p, axis=-1, keepdims=True)
    pb = p.astype(jnp.bfloat16)
    pv = jnp.concatenate([_dot(pb[_rows_of(g, tn)], values[g]) for g in range(KV_HEADS)], axis=0)
    acc_ref[...] = alpha * acc_ref[...] + pv
    m_ref[...] = m_new


def _kv_of(page_refs, g):
    def pick(ref, slab):
        return ref[0, pl.ds(slab, ref.shape[1] // ROW_SLABS, stride=ROW_SLABS), :]
    k = jnp.concatenate([pick(p, g) for p in page_refs], axis=0)
    v = jnp.concatenate([pick(p, KV_HEADS + g) for p in page_refs], axis=0)
    return k.astype(jnp.bfloat16), v.astype(jnp.bfloat16)


def _new_rows_mask(tn, page):
    tok = lax.broadcasted_iota(jnp.int32, (tn, page), 0)
    r = lax.broadcasted_iota(jnp.int32, (tn, page), 1)
    return r <= tok


def _paged_attn_kernel(pt_ref, *refs, mode, n_steps, tn):
    pages = refs[:PAGES_PER_STEP]
    if mode == "slc":
        q_ref, new_ref, sel_ref, selnew_ref, o_ref, m_ref, l_ref, acc_ref = refs[PAGES_PER_STEP:]
    else:
        q_ref, new_ref, cq_ref, ck_ref, cknew_ref, o_ref, m_ref, l_ref, acc_ref = refs[PAGES_PER_STEP:]
    s_id = pl.program_id(1)
    page = pages[0].shape[1] // ROW_SLABS
    nkeys = PAGES_PER_STEP * page

    @pl.when(s_id == 0)
    def _():
        _sample_init(m_ref, l_ref, acc_ref)

    def bias_rows(g, ck, add):
        return jnp.concatenate(
            [(cq_ref[0][:, g * REP + r:g * REP + r + 1] - ck[g * REP + r:g * REP + r + 1, :]) + add
             for r in range(REP)], axis=0)

    if mode == "slc":
        key_blk = _div_pow2(lax.broadcasted_iota(jnp.int32, (tn, nkeys), 1), SLC_BLOCK)
    scores, values, adds = [], [], []
    for g in range(KV_HEADS):
        q = _stack_heads(q_ref[:, g * GROUP_COLS:(g + 1) * GROUP_COLS])
        k, v = _kv_of(pages, g)
        scores.append(_dot_nt(q, k))
        values.append(v)
        if mode == "slc":
            flags = sel_ref[0, 0, g]
            picked = jnp.zeros((tn, nkeys), jnp.float32)
            for jb in range(flags.shape[1]):
                picked = jnp.where(key_blk == jb, flags[:, jb:jb + 1], picked)
            adds += [jnp.where(picked > 0.5, 0.0, NEG)] * REP
        else:
            adds.append(bias_rows(g, ck_ref[0], 0.0))
    _online_update_groups(scores, adds, values, m_ref, l_ref, acc_ref, tn)

    @pl.when(s_id == n_steps - 1)
    def _():
        new_keys = new_ref.shape[1] // ROW_SLABS
        causal = _new_rows_mask(tn, new_keys)
        scores, values, adds = [], [], []
        for g in range(KV_HEADS):
            q = _stack_heads(q_ref[:, g * GROUP_COLS:(g + 1) * GROUP_COLS])
            k, v = _kv_of([new_ref], g)
            scores.append(_dot_nt(q, k))
            values.append(v)
            if mode == "slc":
                adds += [jnp.where(causal & (selnew_ref[0, 0, g][:, 0:1] > 0.5), 0.0, NEG)] * REP
            else:
                adds.append(bias_rows(g, cknew_ref[0][:, :new_keys], jnp.where(causal, 0.0, NEG)))
        _online_update_groups(scores, adds, values, m_ref, l_ref, acc_ref, tn)
        _sample_finish(o_ref, l_ref, acc_ref, tn)


def _paged_attn(mode, page_table, pool, new_pad, q_s, tn, sel5=None, cq=None, ck=None):
    bsz, n_pages = page_table.shape
    _, slab_rows, cols = pool.shape
    page = slab_rows // ROW_SLABS
    n_steps = n_pages // PAGES_PER_STEP
    nkeys = PAGES_PER_STEP * page

    def page_spec(k):
        return pl.BlockSpec((1, slab_rows, cols), lambda b, s, pt: (pt[b, s * PAGES_PER_STEP + k], 0, 0))

    in_specs = [page_spec(k) for k in range(PAGES_PER_STEP)] + [
        pl.BlockSpec((tn, NSA_HEADS * HEAD_DIM), lambda b, s, pt: (b, 0)),
        pl.BlockSpec((1, new_pad.shape[1], cols), lambda b, s, pt: (b, 0, 0)),
    ]
    args = [pool] * PAGES_PER_STEP + [q_s, new_pad]
    if mode == "slc":
        nb = sel5.shape[4]
        in_specs += [pl.BlockSpec((1, 1, KV_HEADS, tn, nb), lambda b, s, pt: (b, s, 0, 0, 0)),
                     pl.BlockSpec((1, 1, KV_HEADS, tn, nb), lambda b, s, pt: (b, n_steps, 0, 0, 0))]
        args += [sel5, sel5]
    else:
        nh = cq.shape[2]
        in_specs += [pl.BlockSpec((1, tn, nh), lambda b, s, pt: (b, 0, 0)),
                     pl.BlockSpec((1, nh, nkeys), lambda b, s, pt: (b, 0, s)),
                     pl.BlockSpec((1, nh, page), lambda b, s, pt: (b, 0, n_pages))]
        args += [cq, ck, ck]
    rows = KV_HEADS * REP * tn
    kern = functools.partial(_paged_attn_kernel, mode=mode, n_steps=n_steps, tn=tn)
    return pl.pallas_call(
        kern,
        grid_spec=pltpu.PrefetchScalarGridSpec(
            num_scalar_prefetch=1,
            grid=(bsz, n_steps),
            in_specs=in_specs,
            out_specs=pl.BlockSpec((tn, NSA_HEADS * HEAD_DIM), lambda b, s, pt: (b, 0)),
            scratch_shapes=[pltpu.VMEM((rows, 1), jnp.float32), pltpu.VMEM((rows, 1), jnp.float32),
                            pltpu.VMEM((rows, HEAD_DIM), jnp.float32)],
        ),
        out_shape=jax.ShapeDtypeStruct((bsz * tn, NSA_HEADS * HEAD_DIM), jnp.float32),
        compiler_params=_cparams(("parallel", "arbitrary")),
        name="paged_" + mode,
    )(page_table, *args)


def _win_sample_kernel(q_ref, win_ref, new_ref, o_ref, m_ref, l_ref, acc_ref, *, tn, past):
    _sample_init(m_ref, l_ref, acc_ref)
    win_len = win_ref.shape[1] // ROW_SLABS
    page = new_ref.shape[1] // ROW_SLABS
    pos = past + lax.broadcasted_iota(jnp.int32, (tn, win_len), 0)
    kpos = past - win_len + lax.broadcasted_iota(jnp.int32, (tn, win_len), 1)
    wmask = (kpos <= pos) & (pos - kpos < WINDOW)
    add_w = [jnp.where(wmask, 0.0, NEG)] * (REP * KV_HEADS)
    add_n = [jnp.where(_new_rows_mask(tn, page), 0.0, NEG)] * (REP * KV_HEADS)
    for src, adds in ((new_ref, add_n), (win_ref, add_w)):
        scores, values = [], []
        for g in range(KV_HEADS):
            q = _stack_heads(q_ref[:, g * GROUP_COLS:(g + 1) * GROUP_COLS])
            k, v = _kv_of([src], g)
            scores.append(_dot_nt(q, k))
            values.append(v)
        _online_update_groups(scores, adds, values, m_ref, l_ref, acc_ref, tn)
    _sample_finish(o_ref, l_ref, acc_ref, tn)


def _win_sample(win_buf, new_pad, q_s, tn, past):
    bsz, win_rows, cols = win_buf.shape
    new_rows = new_pad.shape[1]
    rows = KV_HEADS * REP * tn
    kern = functools.partial(_win_sample_kernel, tn=tn, past=past)
    return pl.pallas_call(
        kern,
        grid=(bsz,),
        in_specs=[pl.BlockSpec((tn, NSA_HEADS * HEAD_DIM), lambda b: (b, 0)),
                  pl.BlockSpec((1, win_rows, cols), lambda b: (b, 0, 0)),
                  pl.BlockSpec((1, new_rows, cols), lambda b: (b, 0, 0))],
        out_specs=pl.BlockSpec((tn, NSA_HEADS * HEAD_DIM), lambda b: (b, 0)),
        out_shape=jax.ShapeDtypeStruct((bsz * tn, NSA_HEADS * HEAD_DIM), jnp.float32),
        scratch_shapes=[pltpu.VMEM((rows, 1), jnp.float32), pltpu.VMEM((rows, 1), jnp.float32),
                        pltpu.VMEM((rows, HEAD_DIM), jnp.float32)],
        compiler_params=_cparams(("parallel",)),
        name="win_sample",
    )(q_s, win_buf, new_pad)


def _merge_kernel(cmp_ref, slc_ref, win_ref, fox_ref, gate_ref, gn_ref, gf_ref, o_ref):
    gates = gate_ref[...]
    parts = []
    for h in range(NSA_HEADS):
        cols = slice(h * HEAD_DIM, (h + 1) * HEAD_DIM)
        c = N_BRANCH * h
        parts.append(gates[:, c:c + 1] * cmp_ref[:, cols] + gates[:, c + 1:c + 2] * slc_ref[:, cols]
                     + gates[:, c + 2:c + 3] * win_ref[:, cols])
    nsa = jnp.concatenate(parts, axis=1)
    nsa = nsa * lax.rsqrt(jnp.mean(nsa * nsa, axis=-1, keepdims=True) + RMS_EPS) * gn_ref[...]
    fox = fox_ref[...]
    fox = fox * lax.rsqrt(jnp.mean(fox * fox, axis=-1, keepdims=True) + RMS_EPS) * gf_ref[...]
    o_ref[...] = jnp.concatenate([nsa, fox], axis=1).astype(o_ref.dtype)


def _merge(o_cmp, o_slc, o_win, o_fox, slab, g_nsa, g_fox):
    n, w = o_cmp.shape
    tm = _pick(n, (256, 128, 64, 32, 16, 8))
    row = lambda i: (i, 0)
    fix = lambda i: (0, 0)
    return pl.pallas_call(
        _merge_kernel,
        grid=(n // tm,),
        in_specs=[pl.BlockSpec((tm, w), row)] * 4 + [pl.BlockSpec((tm, LANES), row), pl.BlockSpec((1, w), fix),
                                                      pl.BlockSpec((1, w), fix)],
        out_specs=pl.BlockSpec((tm, 2 * w), row),
        out_shape=jax.ShapeDtypeStruct((n, 2 * w), jnp.bfloat16),
        compiler_params=_cparams(("parallel",)),
        name="merge_heads",
    )(o_cmp, o_slc, o_win, o_fox, slab, g_nsa.reshape(1, w), g_fox.reshape(1, w))


def _outproj_kernel(a_ref, w_ref, x_ref, o_ref):
    o_ref[...] = x_ref[...] + _dot(a_ref[...], w_ref[...])


def _outproj(a, w, x):
    n, k = a.shape
    d = w.shape[1]
    tm = _pick(n, (768, 512, 384, 256, 128, 64, 32, 16, 8))
    tn = 512
    return pl.pallas_call(
        _outproj_kernel,
        grid=(n // tm, d // tn),
        in_specs=[pl.BlockSpec((tm, k), lambda i, j: (i, 0)), pl.BlockSpec((k, tn), lambda i, j: (0, j)),
                  pl.BlockSpec((tm, tn), lambda i, j: (i, j))],
        out_specs=pl.BlockSpec((tm, tn), lambda i, j: (i, j)),
        out_shape=jax.ShapeDtypeStruct((n, d), jnp.float32),
        compiler_params=_cparams(("parallel", "arbitrary")),
        name="outproj",
    )(a, w, x)


def _pack_bf16_pair(lo, hi):
    def bits(x):
        return lax.bitcast_convert_type(x.astype(jnp.bfloat16).astype(jnp.float32), jnp.uint32)
    return (bits(hi) & jnp.uint32(0xFFFF0000)) | lax.shift_right_logical(bits(lo), jnp.uint32(16))


def _unpack_bf16_pair(word):
    lo = lax.bitcast_convert_type(lax.shift_left(word, jnp.uint32(16)), jnp.float32)
    hi = lax.bitcast_convert_type(word & jnp.uint32(0xFFFF0000), jnp.float32)
    return lo, hi


def _router_kernel(h_ref, g_ref, w_ref, b_ref, xn_ref, eid_ref, gate_ref):
    x = h_ref[...]
    xn = x * lax.rsqrt(jnp.mean(x * x, axis=-1, keepdims=True) + RMS_EPS) * g_ref[...]
    half = xn.shape[1] // 2
    xn_ref[...] = _pack_bf16_pair(xn[:, :half], xn[:, half:])
    lg = jnp.dot(xn, w_ref[...], precision=lax.Precision.HIGHEST, preferred_element_type=jnp.float32) + b_ref[...]
    lane = lax.broadcasted_iota(jnp.int32, lg.shape, 1)
    lane_f = lane.astype(jnp.float32)
    ninf = -jnp.inf
    is_grp = lane < N_GROUPS
    gl = jnp.where(is_grp, lg, ninf)
    gmax = jnp.max(gl, axis=-1, keepdims=True)
    gsel = jnp.min(jnp.where(gl == gmax, lane_f, float(LANES)), axis=-1, keepdims=True)
    p_sel = 1.0 / jnp.sum(jnp.where(is_grp, jnp.exp(lg - gmax), 0.0), axis=-1, keepdims=True)
    e_lane = lane - N_GROUPS
    e_grp = _div_pow2(e_lane, EXPERTS_PER_GROUP).astype(jnp.float32)
    in_grp = (e_lane >= 0) & (e_lane < N_EXPERTS) & (e_grp == gsel)
    el = jnp.where(in_grp, lg, ninf)
    v1 = jnp.max(el, axis=-1, keepdims=True)
    i1 = jnp.min(jnp.where(el == v1, lane_f, float(LANES)), axis=-1, keepdims=True)
    el2 = jnp.where(lane_f == i1, ninf, el)
    v2 = jnp.max(el2, axis=-1, keepdims=True)
    i2 = jnp.min(jnp.where(el2 == v2, lane_f, float(LANES)), axis=-1, keepdims=True)
    e2 = jnp.exp(v2 - v1)
    den = 1.0 + e2
    g1 = p_sel * (1.0 / den)
    g2 = p_sel * (e2 / den)
    eid = jnp.where(lane == 0, i1 - N_GROUPS, jnp.where(lane == 1, i2 - N_GROUPS, 0.0))
    eid_ref[...] = eid.astype(jnp.int32)
    gate_ref[...] = jnp.where(lane == 0, g1, jnp.where(lane == 1, g2, 0.0))


def _router(h1, g_ffn, w_r, b_r):
    n, d = h1.shape
    tm = _pick(n, (256, 128, 64, 32, 16, 8))
    row = lambda i: (i, 0)
    fix = lambda i: (0, 0)
    return pl.pallas_call(
        _router_kernel,
        grid=(n // tm,),
        in_specs=[pl.BlockSpec((tm, d), row), pl.BlockSpec((1, d), fix), pl.BlockSpec((d, LANES), fix),
                  pl.BlockSpec((1, LANES), fix)],
        out_specs=[pl.BlockSpec((tm, d // 2), row), pl.BlockSpec((tm, LANES), row), pl.BlockSpec((tm, LANES), row)],
        out_shape=[jax.ShapeDtypeStruct((n, d // 2), jnp.uint32), jax.ShapeDtypeStruct((n, LANES), jnp.int32),
                   jax.ShapeDtypeStruct((n, LANES), jnp.float32)],
        compiler_params=_cparams(("parallel",)),
        name="router",
    )(h1, g_ffn.reshape(1, d), w_r, b_r)


SC_CORES = 2
SC_SUBCORES = 16
SC_WORKERS = SC_CORES * SC_SUBCORES
SC_GATHER_ROWS = 16


def _gather_rows(idx, table):
    n = idx.shape[0]
    _, d = table.shape
    assert n % (SC_WORKERS * SC_GATHER_ROWS) == 0, n
    per_worker = n // SC_WORKERS
    n_chunks = per_worker // SC_GATHER_ROWS
    mesh = plsc.VectorSubcoreMesh(core_axis_name="c", subcore_axis_name="s")

    @functools.partial(
        pl.kernel, mesh=mesh,
        out_type=jax.ShapeDtypeStruct((n, d), table.dtype),
        scratch_types=[pltpu.VMEM((SC_GATHER_ROWS,), jnp.int32),
                       pltpu.VMEM((SC_GATHER_ROWS, d), table.dtype),
                       pltpu.SemaphoreType.DMA],
    )
    def gather(table_hbm, idx_hbm, out_hbm, idx_v, rows_v, sem):
        wid = lax.axis_index("s") * SC_CORES + lax.axis_index("c")
        base = wid * per_worker

        @pl.loop(0, n_chunks)
        def _(i):
            off = pl.multiple_of(base + i * SC_GATHER_ROWS, SC_GATHER_ROWS)
            pltpu.sync_copy(idx_hbm.at[pl.ds(off, SC_GATHER_ROWS)], idx_v)
            pltpu.async_copy(table_hbm.at[idx_v], rows_v, sem).wait()
            pltpu.sync_copy(rows_v, out_hbm.at[pl.ds(off, SC_GATHER_ROWS)])

    return gather(table, idx)


MOE_TM = 256
MOE_FT = 256
MOE_NT = 1024


def _gateup_kernel(tb_ref, e_ref, f_ref, first_ref, valid_ref, x_ref, wg_ref, wu_ref, h_ref, wgb_ref, wub_ref):
    s = pl.program_id(0)

    @pl.when(first_ref[s] == 1)
    def _():
        wgb_ref[...] = wg_ref[0].astype(jnp.bfloat16)
        wub_ref[...] = wu_ref[0].astype(jnp.bfloat16)

    @pl.when(valid_ref[s] == 1)
    def _():
        x = jnp.concatenate(_unpack_bf16_pair(x_ref[...]), axis=1).astype(jnp.bfloat16)
        a = _dot(x, wgb_ref[...])
        u = _dot(x, wub_ref[...])
        h_ref[...] = (a * (1.0 / (1.0 + jnp.exp(-a))) * u).astype(h_ref.dtype)


def _down_kernel(tb_ref, e_ref, f_ref, first_ref, valid_ref, h_ref, wd_ref, y_ref, wdb_ref):
    s = pl.program_id(0)

    @pl.when(first_ref[s] == 1)
    def _():
        wdb_ref[...] = wd_ref[0].astype(jnp.bfloat16)

    @pl.when(valid_ref[s] == 1)
    def _():
        y = _dot(h_ref[...], wdb_ref[...])
        half = y.shape[1] // 2
        y_ref[...] = _pack_bf16_pair(y[:, :half], y[:, half:])


def _work_list(block_e, start_blk, n_blk, total_blocks, n_tiles, n_tb_max):
    n_steps = n_tb_max * n_tiles
    s = jnp.minimum(jnp.arange(n_steps, dtype=jnp.int32), total_blocks * n_tiles - 1)
    valid = (jnp.arange(n_steps, dtype=jnp.int32) < total_blocks * n_tiles).astype(jnp.int32)
    e = block_e[s // n_tiles]
    local = s - n_tiles * start_blk[e]
    nb = jnp.maximum(n_blk[e], 1)
    tile = local // nb
    within = local % nb
    tb = start_blk[e] + within
    first = ((within == 0) & (valid == 1)).astype(jnp.int32)
    return tb.astype(jnp.int32), e.astype(jnp.int32), tile.astype(jnp.int32), first, valid


def _experts(xs, w_gate, w_up, w_down, block_e, start_blk, n_blk, total_blocks):
    a_pad, d = xs.shape[0], 2 * xs.shape[1]
    n_tb_max = a_pad // MOE_TM
    d_exp = w_gate.shape[2]
    wl1 = _work_list(block_e, start_blk, n_blk, total_blocks, d_exp // MOE_FT, n_tb_max)
    hidden = pl.pallas_call(
        _gateup_kernel,
        grid_spec=pltpu.PrefetchScalarGridSpec(
            num_scalar_prefetch=5,
            grid=(wl1[0].shape[0],),
            in_specs=[
                pl.BlockSpec((MOE_TM, d // 2), lambda s, tb, e, f, fi, va: (tb[s], 0)),
                pl.BlockSpec((1, d, MOE_FT), lambda s, tb, e, f, fi, va: (e[s], 0, f[s])),
                pl.BlockSpec((1, d, MOE_FT), lambda s, tb, e, f, fi, va: (e[s], 0, f[s])),
            ],
            out_specs=pl.BlockSpec((MOE_TM, MOE_FT), lambda s, tb, e, f, fi, va: (tb[s], f[s])),
            scratch_shapes=[pltpu.VMEM((d, MOE_FT), jnp.bfloat16), pltpu.VMEM((d, MOE_FT), jnp.bfloat16)],
        ),
        out_shape=jax.ShapeDtypeStruct((a_pad, d_exp), jnp.bfloat16),
        compiler_params=_cparams(("arbitrary",)),
        name="expert_gate_up",
    )(*wl1, xs, w_gate, w_up)
    wl2 = _work_list(block_e, start_blk, n_blk, total_blocks, d // MOE_NT, n_tb_max)
    return pl.pallas_call(
        _down_kernel,
        grid_spec=pltpu.PrefetchScalarGridSpec(
            num_scalar_prefetch=5,
            grid=(wl2[0].shape[0],),
            in_specs=[
                pl.BlockSpec((MOE_TM, d_exp), lambda s, tb, e, f, fi, va: (tb[s], 0)),
                pl.BlockSpec((1, d_exp, MOE_NT), lambda s, tb, e, f, fi, va: (e[s], 0, f[s])),
            ],
            out_specs=pl.BlockSpec((MOE_TM, MOE_NT // 2), lambda s, tb, e, f, fi, va: (tb[s], f[s])),
            scratch_shapes=[pltpu.VMEM((d_exp, MOE_NT), jnp.bfloat16)],
        ),
        out_shape=jax.ShapeDtypeStruct((a_pad, d // 2), jnp.uint32),
        compiler_params=_cparams(("arbitrary",)),
        name="expert_down",
    )(*wl2, hidden, w_down)


def _combine_kernel(h_ref, y0_ref, y1_ref, gate_ref, op_ref, os_ref, *, prompt_tiles):
    def write(o_ref):
        g = gate_ref[...]
        half = MOE_NT // 2
        for t in range(o_ref.shape[1] // MOE_NT):
            w = slice(t * half, (t + 1) * half)
            lo0, hi0 = _unpack_bf16_pair(y0_ref[0, :, w])
            lo1, hi1 = _unpack_bf16_pair(y1_ref[0, :, w])
            c = t * MOE_NT
            o_ref[:, c:c + half] = h_ref[:, c:c + half] + (g[:, 0:1] * lo0 + g[:, 1:2] * lo1)
            o_ref[:, c + half:c + MOE_NT] = h_ref[:, c + half:c + MOE_NT] + (g[:, 0:1] * hi0 + g[:, 1:2] * hi1)

    @pl.when(pl.program_id(0) < prompt_tiles)
    def _():
        write(op_ref)

    @pl.when(pl.program_id(0) >= prompt_tiles)
    def _():
        write(os_ref)


def _combine(h1, y2, gate_slab, n_p):
    n, d = h1.shape
    tm = _pick(math.gcd(n_p, n - n_p), (256, 128, 64, 32, 16, 8))
    pt = n_p // tm
    kern = functools.partial(_combine_kernel, prompt_tiles=pt)
    return pl.pallas_call(
        kern,
        grid=(n // tm,),
        in_specs=[pl.BlockSpec((tm, d), lambda i: (i, 0)), pl.BlockSpec((1, tm, d // 2), lambda i: (0, i, 0)),
                  pl.BlockSpec((1, tm, d // 2), lambda i: (1, i, 0)), pl.BlockSpec((tm, LANES), lambda i: (i, 0))],
        out_specs=[pl.BlockSpec((tm, d), lambda i: (jnp.minimum(i, pt - 1), 0)),
                   pl.BlockSpec((tm, d), lambda i: (jnp.maximum(i - pt, 0), 0))],
        out_shape=[jax.ShapeDtypeStruct((n_p, d), jnp.float32), jax.ShapeDtypeStruct((n - n_p, d), jnp.float32)],
        compiler_params=_cparams(("arbitrary",)),
        name="moe_combine",
    )(h1, y2, y2, gate_slab)


def _moe(h1, n_p, g_ffn, w_rg, b_rg, w_re, b_re, w_gate, w_up, w_down):
    n, d = h1.shape
    w_r = jnp.zeros((d, LANES), jnp.float32).at[:, :N_GROUPS].set(w_rg).at[:, N_GROUPS:N_GROUPS + N_EXPERTS].set(w_re)
    b_r = jnp.zeros((1, LANES), jnp.float32).at[0, :N_GROUPS].set(b_rg).at[0, N_GROUPS:N_GROUPS + N_EXPERTS].set(b_re)
    xn, eid_slab, gate_slab = _router(h1, g_ffn, w_r, b_r)
    flat = eid_slab[:, :EXPERT_TOPK].reshape(n * EXPERT_TOPK)
    onehot = (flat[:, None] == jnp.arange(N_EXPERTS, dtype=jnp.int32)[None, :]).astype(jnp.int32)
    rank = jnp.take_along_axis(jnp.cumsum(onehot, axis=0) - onehot, flat[:, None], axis=1)[:, 0]
    counts = jnp.sum(onehot, axis=0)
    n_blk = (counts + MOE_TM - 1) // MOE_TM
    start_blk = jnp.cumsum(n_blk) - n_blk
    total_blocks = jnp.sum(n_blk)
    slot = start_blk[flat] * MOE_TM + rank
    n_tb_max = -(-(n * EXPERT_TOPK) // MOE_TM) + N_EXPERTS
    a_pad = n_tb_max * MOE_TM
    src_tok = jnp.zeros((a_pad,), jnp.int32).at[slot].set(jnp.arange(n * EXPERT_TOPK, dtype=jnp.int32) // EXPERT_TOPK)
    end_blk = jnp.cumsum(n_blk)
    owner = jnp.sum((end_blk[None, :] <= jnp.arange(n_tb_max, dtype=jnp.int32)[:, None]).astype(jnp.int32), axis=1)
    block_e = jnp.minimum(owner, N_EXPERTS - 1).astype(jnp.int32)
    xs = _gather_rows(src_tok, xn)
    yb = _experts(xs, w_gate, w_up, w_down, block_e, start_blk.astype(jnp.int32), n_blk.astype(jnp.int32),
                  total_blocks.astype(jnp.int32))
    back = slot.reshape(n, EXPERT_TOPK).T.reshape(n * EXPERT_TOPK).astype(jnp.int32)
    y2 = _gather_rows(back, yb).reshape(EXPERT_TOPK, n, d // 2)
    return _combine(h1, y2, gate_slab, n_p)


def _rope_tables(pos):
    half = HEAD_DIM // 2
    inv = jnp.exp(-math.log(ROPE_THETA) * jnp.arange(half, dtype=jnp.float32) * (2.0 / HEAD_DIM))
    ang = pos.astype(jnp.float32)[:, None] * inv[None, :]
    cos, sin = jnp.cos(ang), jnp.sin(ang)
    return jnp.concatenate([cos, cos], axis=1), jnp.concatenate([-sin, sin], axis=1)


def _repack_weights(w_in, b_gate, b_forget, g_nsa_q, g_nsa_k, g_fox_q, g_fox_k):
    o1 = NSA_HEADS * HEAD_DIM
    o2 = o1 + N_BRANCH * KV_COLS
    o3 = o2 + N_GATE
    o4 = o3 + FOX_HEADS * HEAD_DIM
    o5 = o4 + KV_COLS
    d = w_in.shape[0]
    w_main = jnp.concatenate([w_in[:, :o2], w_in[:, o3:o5]], axis=1).astype(jnp.bfloat16)
    n_f = w_in.shape[1] - o5
    w_small = jnp.zeros((d, LANES), jnp.float32).at[:, :N_GATE].set(w_in[:, o2:o3])
    w_small = w_small.at[:, COL_LOGF:COL_LOGF + n_f].set(w_in[:, o5:]).astype(jnp.bfloat16)
    b_small = jnp.zeros((1, LANES), jnp.float32).at[0, :N_GATE].set(b_gate.reshape(N_GATE))
    b_small = b_small.at[0, COL_LOGF:COL_LOGF + n_f].set(b_forget)
    ones = jnp.ones((KV_HEADS * HEAD_DIM,), jnp.float32)
    gains = [jnp.tile(g_nsa_q, NSA_HEADS)]
    for br in range(N_BRANCH):
        gains += [jnp.tile(g_nsa_k[br], KV_HEADS), ones]
    gains += [jnp.tile(g_fox_q, FOX_HEADS), jnp.tile(g_fox_k, KV_HEADS), ones]
    return w_main, w_small, b_small, jnp.concatenate(gains).reshape(1, MAIN_COLS)


def _cmp_weights(w_k, w_v, pe_k, pe_v):
    def cat(w):
        return jnp.concatenate([w[:CMP_STRIDE], w[CMP_STRIDE:]], axis=2)
    return jnp.stack([cat(w_k), cat(w_v)]).astype(jnp.bfloat16), jnp.stack([pe_k, pe_v])


def _pad_rows(x, rows):
    return jnp.pad(x, ((0, 0), (0, rows - x.shape[1]), (0, 0)))


def kernel(x_prompt, x_sample, cache_nsa_cmp_kv, cache_nsa_slc_kv, cache_nsa_win_kv, cache_fox_kv, cache_fox_logf, page_table, g_attn_norm, w_in, b_nsa_gate, b_fox_forget, g_nsa_q, g_nsa_k, g_fox_q, g_fox_k, w_cmp_k, w_cmp_v, pe_cmp_k, pe_cmp_v, g_out_nsa, g_out_fox, w_out, g_ffn_norm, w_router_grp, b_router_grp, w_router_exp, b_router_exp, w_exp_gate, w_exp_up, w_exp_down):
    depth = w_in.shape[0]
    assert depth == 1, "single-layer step"
    bp, t, d = x_prompt.shape
    bs, tn, _ = x_sample.shape
    n_pages = page_table.shape[1]
    page = cache_fox_kv.shape[2]
    past = n_pages * page
    n_p, n_s = bp * t, bs * tn
    l = 0

    x_all = jnp.concatenate([x_prompt.reshape(n_p, d), x_sample.reshape(n_s, d)], axis=0)
    pos_all = jnp.concatenate([jnp.tile(jnp.arange(t, dtype=jnp.int32), bp),
                               jnp.tile(past + jnp.arange(tn, dtype=jnp.int32), bs)])
    cos_f, sin_s = _rope_tables(pos_all)
    w_main, w_small, b_small, gain_cols = _repack_weights(
        w_in[l], b_nsa_gate[l], b_fox_forget[l], g_nsa_q[l], g_nsa_k[l], g_fox_q[l], g_fox_k[l])

    xn = _rmsnorm(x_all, g_attn_norm[l], jnp.bfloat16)
    hf, hb = _inproj(xn, w_main, gain_cols, cos_f, sin_s)
    slab = _small_proj(xn, w_small, b_small)

    col_cmp, col_slc, col_win = (COL_KV_NSA + br * KV_COLS for br in range(N_BRANCH))
    w_cat, pe_cat = _cmp_weights(w_cmp_k[l], w_cmp_v[l], pe_cmp_k[l], pe_cmp_v[l])
    v_off = KV_HEADS * HEAD_DIM

    tq = _pick(t, (256, 128))
    kc_p = _compress_prompt(hf, bp, t, col_cmp, w_cat, pe_cat)
    n_slc_p = max(-(-t // SLC_BLOCK), SLC_TOPK)
    nsp_p = -(-n_slc_p // LANES) * LANES if n_slc_p > 64 else 64
    o_cmp_p, sel_p = _cmp_select(hb, 0, bp, t // tq, tq, kc_p, 0, t // CMP_STRIDE - CMP_RATIO + 1, n_slc_p, nsp_p,
                                 jnp.bfloat16)
    o_slc_p = _flash_prompt("slc", hb, bp, t, COL_Q_NSA, col_slc, col_slc + v_off, sel=sel_p)
    o_win_p = _win_prompt(hb, bp, t, COL_Q_NSA, col_win, col_win + v_off)
    c_p = _cumsum_prompt(slab[:n_p], bp, t)[:, COL_LOGF:COL_LOGF + FOX_HEADS]
    cq_p = c_p.reshape(n_p, KV_HEADS, REP).transpose(1, 0, 2)
    ck_p = c_p.T.reshape(KV_HEADS, REP, n_p)
    o_fox_p = _flash_prompt("fox", hb, bp, t, COL_Q_FOX, COL_KV_FOX, COL_KV_FOX + v_off, cq=cq_p, ck=ck_p)
    a_p = _merge(o_cmp_p, o_slc_p, o_win_p, o_fox_p, slab[:n_p], g_out_nsa[l], g_out_fox[l])

    hf_s, slab_s = hf[n_p:], slab[n_p:]
    q_nsa_s = hf_s[:, COL_Q_NSA:COL_Q_NSA + NSA_HEADS * HEAD_DIM]
    q_fox_s = hf_s[:, COL_Q_FOX:COL_Q_FOX + FOX_HEADS * HEAD_DIM]

    new_keys = -(-tn // CMP_STRIDE) * CMP_STRIDE

    def new_rows(col):
        rows = _pad_rows(hf_s[:, col:col + KV_COLS].reshape(bs, tn, KV_COLS), new_keys)
        return rows.reshape(bs, new_keys * ROW_SLABS, HEAD_DIM)

    pool = lambda c: c[l].reshape(c.shape[1], page * ROW_SLABS, HEAD_DIM)
    kc_s = _compress_sample(page_table, pool(cache_nsa_cmp_kv), new_rows(col_cmp), w_cat, pe_cat)
    t_ctx = past + tn
    n_cmp_s = -(-t_ctx // CMP_STRIDE) - CMP_RATIO + 1
    n_slc_s = max(-(-t_ctx // SLC_BLOCK), SLC_TOPK)
    blocks_per_step = PAGES_PER_STEP * page // SLC_BLOCK
    nsp_s = -(-(n_slc_s + 1) // LANES) * LANES
    nsp_s = -(-nsp_s // blocks_per_step) * blocks_per_step
    o_cmp_s, sel_s = _cmp_select(q_nsa_s, 0, bs, 1, tn, kc_s, past, n_cmp_s, n_slc_s, nsp_s, jnp.float32)
    sel5 = sel_s.reshape(KV_HEADS, bs, tn, nsp_s // blocks_per_step, blocks_per_step).transpose(1, 3, 0, 2, 4)
    o_slc_s = _paged_attn("slc", page_table, pool(cache_nsa_slc_kv), new_rows(col_slc), q_nsa_s, tn, sel5=sel5)
    win_buf = cache_nsa_win_kv[l].reshape(bs, -1, HEAD_DIM)
    o_win_s = _win_sample(win_buf, new_rows(col_win), q_nsa_s, tn, past)
    logf_s = slab_s[:, COL_LOGF:COL_LOGF + FOX_HEADS].reshape(bs, tn, FOX_HEADS)
    ck_s = _cumsum_sample(page_table, cache_fox_logf[l].transpose(0, 2, 1),
                          _pad_rows(logf_s, page).transpose(0, 2, 1))
    cq_s = ck_s[:, :, past:past + tn].transpose(0, 2, 1)
    o_fox_s = _paged_attn("fox", page_table, pool(cache_fox_kv), new_rows(COL_KV_FOX), q_fox_s, tn, cq=cq_s, ck=ck_s)
    a_s = _merge(o_cmp_s, o_slc_s, o_win_s, o_fox_s, slab_s, g_out_nsa[l], g_out_fox[l])

    h1 = _outproj(jnp.concatenate([a_p, a_s], axis=0), w_out[l].astype(jnp.bfloat16), x_all)
    y_p, y_s = _moe(h1, n_p, g_ffn_norm[l], w_router_grp[l], b_router_grp[l], w_router_exp[l], b_router_exp[l],
                    w_exp_gate[l], w_exp_up[l], w_exp_down[l])

    def kv_out(rows, col, bsz, tt):
        return rows[:, col:col + KV_COLS].reshape(1, bsz, tt, 2, KV_HEADS, HEAD_DIM)

    hf_p = hf[:n_p]
    win_p = kv_out(hf_p, col_win, bp, t)[:, :, t - min(WINDOW, t):]
    win_new = kv_out(hf_s, col_win, bs, tn)
    win_s = jnp.concatenate([cache_nsa_win_kv[l:l + 1][:, :, tn:], win_new], axis=2)
    logf_p = slab[:n_p, COL_LOGF:COL_LOGF + FOX_HEADS].reshape(1, bp, t, FOX_HEADS)
    return (y_p.reshape(bp, t, d), y_s.reshape(bs, tn, d),
            kv_out(hf_p, col_cmp, bp, t), kv_out(hf_s, col_cmp, bs, tn),
            kv_out(hf_p, col_slc, bp, t), kv_out(hf_s, col_slc, bs, tn),
            win_p, win_s,
            kv_out(hf_p, COL_KV_FOX, bp, t), kv_out(hf_s, COL_KV_FOX, bs, tn),
            logf_p, logf_s.reshape(1, bs, tn, FOX_HEADS))
```

```python
import functools
import math

import jax
import jax.numpy as jnp
from jax import lax
from jax.experimental import pallas as pl
from jax.experimental.pallas import tpu as pltpu
from jax.experimental.pallas import tpu_sc as plsc

HEAD_DIM = 128
NSA_HEADS = 16
FOX_HEADS = 16
KV_HEADS = 4
REP = NSA_HEADS // KV_HEADS
N_BRANCH = 3
CMP_BLOCK = 32
CMP_STRIDE = 16
CMP_RATIO = CMP_BLOCK // CMP_STRIDE
SLC_BLOCK = 64
SLC_TOPK = 16
WINDOW = 512
ROPE_THETA = 10000.0
RMS_EPS = 1e-6
N_GROUPS = 4
EXPERTS_PER_GROUP = 8
N_EXPERTS = N_GROUPS * EXPERTS_PER_GROUP
EXPERT_TOPK = 2
ATTN_SCALE = HEAD_DIM ** -0.5

LANES = 128
GROUP_COLS = REP * HEAD_DIM
KV_COLS = 2 * KV_HEADS * HEAD_DIM
ROW_SLABS = 2 * KV_HEADS
NEG = -1e30
VMEM_LIMIT = 48 * 1024 * 1024
PAGES_PER_STEP = 8

COL_Q_NSA = 0
COL_KV_NSA = NSA_HEADS * HEAD_DIM
COL_Q_FOX = COL_KV_NSA + N_BRANCH * KV_COLS
COL_KV_FOX = COL_Q_FOX + FOX_HEADS * HEAD_DIM
MAIN_COLS = COL_KV_FOX + KV_COLS
N_GATE = N_BRANCH * NSA_HEADS
COL_LOGF = N_GATE


def _pick(n, cands):
    for c in cands:
        if n % c == 0:
            return c
    raise ValueError(f"no tile in {cands} divides {n}")


def _cparams(sem, vmem=VMEM_LIMIT):
    return pltpu.CompilerParams(dimension_semantics=sem, vmem_limit_bytes=vmem)


def _split3(x):
    hi = x.astype(jnp.bfloat16)
    r1 = x - hi.astype(jnp.float32)
    mid = r1.astype(jnp.bfloat16)
    lo = (r1 - mid.astype(jnp.float32)).astype(jnp.bfloat16)
    return hi, mid, lo


def _dot(a, b):
    return jnp.dot(a, b, preferred_element_type=jnp.float32)


def _dot_nt(a, b):
    return lax.dot_general(a, b, (((1,), (1,)), ((), ())), preferred_element_type=jnp.float32)


def _rms_kernel(x_ref, g_ref, o_ref):
    x = x_ref[...]
    ms = jnp.mean(x * x, axis=-1, keepdims=True)
    o_ref[...] = (x * lax.rsqrt(ms + RMS_EPS) * g_ref[...]).astype(o_ref.dtype)


def _rmsnorm(x, g, out_dtype):
    n, d = x.shape
    tm = _pick(n, (256, 128, 64, 32, 16, 8))
    return pl.pallas_call(
        _rms_kernel,
        grid=(n // tm,),
        in_specs=[pl.BlockSpec((tm, d), lambda i: (i, 0)), pl.BlockSpec((1, d), lambda i: (0, 0))],
        out_specs=pl.BlockSpec((tm, d), lambda i: (i, 0)),
        out_shape=jax.ShapeDtypeStruct((n, d), out_dtype),
        compiler_params=_cparams(("parallel",)),
        name="rmsnorm",
    )(x, g.reshape(1, d))


N_COLBLK = MAIN_COLS // GROUP_COLS


def _colblock_kinds():
    kinds = []
    kinds += ["rope"] * (NSA_HEADS // REP)
    for _ in range(N_BRANCH):
        kinds += ["rope", "id"]
    kinds += ["norm"] * (FOX_HEADS // REP)
    kinds += ["norm", "id"]
    assert len(kinds) == N_COLBLK
    return kinds


def _any_eq(j, vals):
    return functools.reduce(jnp.logical_or, [j == v for v in vals])


def _inproj_kernel(x_ref, w_ref, gain_ref, cos_ref, sin_ref, of_ref, ob_ref):
    j = pl.program_id(1)
    kinds = _colblock_kinds()
    id_blocks = [i for i, k in enumerate(kinds) if k == "id"]
    rope_blocks = [i for i, k in enumerate(kinds) if k == "rope"]
    norm_blocks = [i for i, k in enumerate(kinds) if k == "norm"]
    acc = _dot(x_ref[...], w_ref[...])

    def write(y):
        of_ref[...] = y
        ob_ref[...] = y.astype(ob_ref.dtype)

    def normed(rope):
        outs = []
        for s in range(GROUP_COLS // HEAD_DIM):
            h = acc[:, s * HEAD_DIM:(s + 1) * HEAD_DIM]
            g = gain_ref[:, s * HEAD_DIM:(s + 1) * HEAD_DIM]
            y = h * lax.rsqrt(jnp.mean(h * h, axis=-1, keepdims=True) + RMS_EPS) * g
            if rope:
                y = y * cos_ref[...] + pltpu.roll(y, HEAD_DIM // 2, 1) * sin_ref[...]
            outs.append(y)
        return jnp.concatenate(outs, axis=1)

    @pl.when(_any_eq(j, id_blocks))
    def _():
        write(acc)

    @pl.when(_any_eq(j, norm_blocks))
    def _():
        write(normed(False))

    @pl.when(_any_eq(j, rope_blocks))
    def _():
        write(normed(True))


def _inproj(xn, w_main, gain_cols, cos_f, sin_s):
    n, d = xn.shape
    tm = _pick(n, (768, 512, 384, 256, 128, 64, 32, 16, 8))
    tn = GROUP_COLS
    return pl.pallas_call(
        _inproj_kernel,
        grid=(n // tm, MAIN_COLS // tn),
        in_specs=[
            pl.BlockSpec((tm, d), lambda i, j: (i, 0)),
            pl.BlockSpec((d, tn), lambda i, j: (0, j)),
            pl.BlockSpec((1, tn), lambda i, j: (0, j)),
            pl.BlockSpec((tm, HEAD_DIM), lambda i, j: (i, 0)),
            pl.BlockSpec((tm, HEAD_DIM), lambda i, j: (i, 0)),
        ],
        out_specs=[pl.BlockSpec((tm, tn), lambda i, j: (i, j)), pl.BlockSpec((tm, tn), lambda i, j: (i, j))],
        out_shape=[jax.ShapeDtypeStruct((n, MAIN_COLS), jnp.float32),
                   jax.ShapeDtypeStruct((n, MAIN_COLS), jnp.bfloat16)],
        compiler_params=_cparams(("parallel", "arbitrary")),
        name="inproj",
    )(xn, w_main, gain_cols, cos_f, sin_s)


def _small_kernel(x_ref, w_ref, b_ref, o_ref):
    z = _dot(x_ref[...], w_ref[...]) + b_ref[...]
    lane = lax.broadcasted_iota(jnp.int32, z.shape, 1)
    sig = 1.0 / (1.0 + jnp.exp(-z))
    logsig = jnp.minimum(z, 0.0) - jnp.log1p(jnp.exp(-jnp.abs(z)))
    o_ref[...] = jnp.where(lane < N_GATE, sig, logsig)


def _small_proj(xn, w_small, b_small):
    n, d = xn.shape
    tm = _pick(n, (768, 512, 384, 256, 128, 64, 32, 16, 8))
    return pl.pallas_call(
        _small_kernel,
        grid=(n // tm,),
        in_specs=[pl.BlockSpec((tm, d), lambda i: (i, 0)), pl.BlockSpec((d, LANES), lambda i: (0, 0)),
                  pl.BlockSpec((1, LANES), lambda i: (0, 0))],
        out_specs=pl.BlockSpec((tm, LANES), lambda i: (i, 0)),
        out_shape=jax.ShapeDtypeStruct((n, LANES), jnp.float32),
        compiler_params=_cparams(("parallel",)),
        name="small_proj",
    )(xn, w_small, b_small)


def _tri_lower(n):
    r = lax.broadcasted_iota(jnp.int32, (n, n), 0)
    c = lax.broadcasted_iota(jnp.int32, (n, n), 1)
    return (c <= r).astype(jnp.bfloat16)


def _block_cumsum(x, tri):
    hi, mid, lo = _split3(x)
    return _dot(tri, hi) + _dot(tri, mid) + _dot(tri, lo)


def _cumsum_prompt_kernel(x_ref, o_ref, carry_ref):
    @pl.when(pl.program_id(1) == 0)
    def _():
        carry_ref[...] = jnp.zeros_like(carry_ref)

    blk = x_ref.shape[0]
    c = _block_cumsum(x_ref[...], _tri_lower(blk)) + carry_ref[...]
    o_ref[...] = c
    carry_ref[...] = c[blk - 1:blk, :]


def _cumsum_prompt(slab, b, t):
    blk = _pick(t, (512, 256, 128))
    nb = t // blk
    return pl.pallas_call(
        _cumsum_prompt_kernel,
        grid=(b, nb),
        in_specs=[pl.BlockSpec((blk, LANES), lambda i, j: (i * nb + j, 0))],
        out_specs=pl.BlockSpec((blk, LANES), lambda i, j: (i * nb + j, 0)),
        out_shape=jax.ShapeDtypeStruct((b * t, LANES), jnp.float32),
        scratch_shapes=[pltpu.VMEM((1, LANES), jnp.float32)],
        compiler_params=_cparams(("parallel", "arbitrary")),
        name="cumsum_prompt",
    )(slab)


CUMSUM_PAGES_PER_STEP = 16


def _cumsum_sample_kernel(pt_ref, *refs, n_steps, pps):
    pages = refs[:pps]
    new_ref, o_ref, carry_ref = refs[pps:]
    s = pl.program_id(1)
    _, h, page = pages[0].shape

    @pl.when(s == 0)
    def _():
        carry_ref[...] = jnp.zeros_like(carry_ref)

    def local(x):
        r = lax.broadcasted_iota(jnp.int32, (page, page), 0)
        c = lax.broadcasted_iota(jnp.int32, (page, page), 1)
        tri = (r <= c).astype(jnp.bfloat16)
        hi, mid, lo = _split3(x)
        return _dot(hi, tri) + _dot(mid, tri) + _dot(lo, tri)

    @pl.when(s < n_steps)
    def _():
        loc = local(jnp.concatenate([p[0] for p in pages], axis=0))
        carry = carry_ref[...]
        for p in range(pps):
            blk = loc[p * h:(p + 1) * h] + carry
            o_ref[0, :, p * page:(p + 1) * page] = blk
            carry = blk[:, page - 1:page]
        carry_ref[...] = carry

    @pl.when(s == n_steps)
    def _():
        o_ref[0, :, 0:page] = local(new_ref[0]) + carry_ref[...]
        o_ref[0, :, page:] = jnp.zeros((h, (pps - 1) * page), jnp.float32)


def _cumsum_sample(page_table, pool_t, new_t):
    bsz, n_pages = page_table.shape
    _, h, page = pool_t.shape
    pps = CUMSUM_PAGES_PER_STEP if n_pages % CUMSUM_PAGES_PER_STEP == 0 else PAGES_PER_STEP
    n_steps = n_pages // pps
    width = pps * page

    def page_spec(k):
        return pl.BlockSpec(
            (1, h, page), lambda b, s, pt: (pt[b, jnp.minimum(s, n_steps - 1) * pps + k], 0, 0))

    kern = functools.partial(_cumsum_sample_kernel, n_steps=n_steps, pps=pps)
    return pl.pallas_call(
        kern,
        grid_spec=pltpu.PrefetchScalarGridSpec(
            num_scalar_prefetch=1,
            grid=(bsz, n_steps + 1),
            in_specs=[page_spec(k) for k in range(pps)] + [
                pl.BlockSpec((1, h, page), lambda b, s, pt: (b, 0, 0))],
            out_specs=pl.BlockSpec((1, h, width), lambda b, s, pt: (b, 0, s)),
            scratch_shapes=[pltpu.VMEM((h, 1), jnp.float32)],
        ),
        out_shape=jax.ShapeDtypeStruct((bsz, h, (n_steps + 1) * width), jnp.float32),
        compiler_params=_cparams(("parallel", "arbitrary")),
        name="cumsum_sample",
    )(page_table, *([pool_t] * pps), new_t)


def _pe_term(pe_ref, w_ref):
    acc = jnp.zeros((16, HEAD_DIM), jnp.float32)
    for j in range(CMP_STRIDE):
        lo = jnp.broadcast_to(pe_ref[j:j + 1, :], (16, HEAD_DIM)).astype(jnp.bfloat16)
        hi = jnp.broadcast_to(pe_ref[CMP_STRIDE + j:CMP_STRIDE + j + 1, :], (16, HEAD_DIM)).astype(jnp.bfloat16)
        w = w_ref[j]
        acc = acc + _dot(lo, w[:, :HEAD_DIM]) + _dot(hi, w[:, HEAD_DIM:])
    return acc[0:1]


def _compress_prompt_kernel(x_ref, w_ref, pe_ref, o_ref, *, n_chunks):
    acc = jnp.zeros((n_chunks, 2 * HEAD_DIM), jnp.float32)
    for j in range(CMP_STRIDE):
        xj = x_ref[pl.ds(j, n_chunks, stride=CMP_STRIDE), :].astype(jnp.bfloat16)
        acc = acc + _dot(xj, w_ref[0, j])
    nxt = pltpu.roll(acc[:, HEAD_DIM:], n_chunks - 1, 0)
    o_ref[0, 0, 0] = (acc[:, :HEAD_DIM] + nxt + _pe_term(pe_ref.at[0], w_ref.at[0])).astype(o_ref.dtype)


def _compress_prompt(hf, b, t, col0, w_cat, pe):
    n_chunks = t // CMP_STRIDE
    rows_blk = col0 // HEAD_DIM
    kern = functools.partial(_compress_prompt_kernel, n_chunks=n_chunks)
    return pl.pallas_call(
        kern,
        grid=(b, 2, KV_HEADS),
        in_specs=[
            pl.BlockSpec((t, HEAD_DIM), lambda i, kv, g: (i, rows_blk + kv * KV_HEADS + g)),
            pl.BlockSpec((1, CMP_STRIDE, HEAD_DIM, 2 * HEAD_DIM), lambda i, kv, g: (kv, 0, 0, 0)),
            pl.BlockSpec((1, CMP_BLOCK, HEAD_DIM), lambda i, kv, g: (kv, 0, 0)),
        ],
        out_specs=pl.BlockSpec((1, 1, 1, n_chunks, HEAD_DIM), lambda i, kv, g: (i, kv, g, 0, 0)),
        out_shape=jax.ShapeDtypeStruct((b, 2, KV_HEADS, n_chunks, HEAD_DIM), jnp.bfloat16),
        compiler_params=_cparams(("parallel", "parallel", "parallel")),
        name="compress_prompt",
    )(hf, w_cat, pe)


def _compress_sample_kernel(pt_ref, *refs, n_steps):
    pages = refs[:PAGES_PER_STEP]
    next_ref, new_ref, w_ref, pe_ref, o_ref = refs[PAGES_PER_STEP:]
    s = pl.program_id(1)
    chunk_rows = CMP_STRIDE * ROW_SLABS
    cpp = pages[0].shape[1] // chunk_rows
    n_out = PAGES_PER_STEP * cpp * ROW_SLABS
    is_last = s == n_steps - 1

    def chunk_lhs(ref, c):
        base = c * chunk_rows
        return jnp.concatenate([ref[0, base + j * ROW_SLABS:base + (j + 1) * ROW_SLABS, :]
                                for j in range(CMP_STRIDE)], axis=1)

    lhs = [chunk_lhs(p, c) for p in pages for c in range(cpp)]
    lhs.append(jnp.where(is_last, chunk_lhs(new_ref, 0), chunk_lhs(next_ref, 0)))
    out = _dot(jnp.concatenate(lhs, axis=0).astype(jnp.bfloat16), w_ref[...])
    pe = _dot(pe_ref[...].astype(jnp.bfloat16), w_ref[...])
    own, nxt = out[:n_out], out[ROW_SLABS:n_out + ROW_SLABS]
    d = HEAD_DIM
    is_k = (lax.broadcasted_iota(jnp.int32, (n_out, d), 0) & (ROW_SLABS - 1)) < KV_HEADS
    pe_k = pe[0:1, 0:d] + pe[1:2, d:2 * d]
    pe_v = pe[2:3, 2 * d:3 * d] + pe[3:4, 3 * d:4 * d]
    o_ref[0] = jnp.where(is_k, own[:, 0:d] + nxt[:, d:2 * d] + pe_k, own[:, 2 * d:3 * d] + nxt[:, 3 * d:4 * d] + pe_v)


def _compress_sample(page_table, pool, new_pad, w_cat, pe):
    bsz, n_pages = page_table.shape
    _, page, cols = pool.shape
    assert n_pages % PAGES_PER_STEP == 0
    n_steps = n_pages // PAGES_PER_STEP
    chunk_rows = CMP_STRIDE * ROW_SLABS
    cps = PAGES_PER_STEP * page // chunk_rows
    w_flat = jnp.concatenate([w_cat[0].reshape(CMP_STRIDE * HEAD_DIM, 2 * HEAD_DIM),
                              w_cat[1].reshape(CMP_STRIDE * HEAD_DIM, 2 * HEAD_DIM)], axis=1)
    pe_lhs = jnp.pad(pe.reshape(2 * CMP_RATIO, CMP_STRIDE * HEAD_DIM), ((0, ROW_SLABS - 2 * CMP_RATIO), (0, 0)))

    def page_spec(k):
        return pl.BlockSpec((1, page, cols), lambda b, s, pt: (pt[b, s * PAGES_PER_STEP + k], 0, 0))

    next_spec = pl.BlockSpec(
        (1, chunk_rows, cols), lambda b, s, pt: (pt[b, jnp.minimum((s + 1) * PAGES_PER_STEP, n_pages - 1)], 0, 0))
    kern = functools.partial(_compress_sample_kernel, n_steps=n_steps)
    out = pl.pallas_call(
        kern,
        grid_spec=pltpu.PrefetchScalarGridSpec(
            num_scalar_prefetch=1,
            grid=(bsz, n_steps),
            in_specs=[page_spec(k) for k in range(PAGES_PER_STEP)] + [
                next_spec,
                pl.BlockSpec((1, chunk_rows, cols), lambda b, s, pt: (b, 0, 0)),
                pl.BlockSpec(w_flat.shape, lambda b, s, pt: (0, 0)),
                pl.BlockSpec(pe_lhs.shape, lambda b, s, pt: (0, 0)),
            ],
            out_specs=pl.BlockSpec((1, cps * ROW_SLABS, HEAD_DIM), lambda b, s, pt: (b, s, 0)),
        ),
        out_shape=jax.ShapeDtypeStruct((bsz, n_steps * cps * ROW_SLABS, HEAD_DIM), jnp.float32),
        compiler_params=_cparams(("parallel", "arbitrary")),
        name="compress_sample",
    )(page_table, *([pool] * (PAGES_PER_STEP + 1)), new_pad, w_flat, pe_lhs)
    out = out.reshape(bsz, n_steps * cps, 2, KV_HEADS, HEAD_DIM).transpose(0, 2, 3, 1, 4)
    return out.astype(jnp.bfloat16)


def _stack_heads(q):
    return jnp.concatenate([q[:, r * HEAD_DIM:(r + 1) * HEAD_DIM] for r in range(REP)], axis=0).astype(jnp.bfloat16)


def _div_pow2(x, c):
    assert c & (c - 1) == 0
    return lax.shift_right_arithmetic(x, jnp.int32(c.bit_length() - 1))


def _unstack_heads(o, tq):
    return jnp.concatenate([o[r * tq:(r + 1) * tq] for r in range(REP)], axis=1)


RANK_GROUP = 8


def _cmp_select_kernel(q_ref, kc_ref, vc_ref, ov_ref, o_ref, sel_ref, rank_ref, *, tq, pos_base, n_cmp_valid, n_slc):
    qi = pl.program_id(2)
    nc = kc_ref.shape[3]
    nsp = ov_ref.shape[1]
    q = _stack_heads(q_ref[...])
    s = _dot_nt(q, kc_ref[0, 0, 0]) * ATTN_SCALE
    assert tq & (tq - 1) == 0
    t1 = pos_base + qi * tq + (lax.broadcasted_iota(jnp.int32, (REP * tq, nc), 0) & (tq - 1))
    n1 = lax.broadcasted_iota(jnp.int32, (REP * tq, nc), 1)
    vis = (n1 * CMP_STRIDE + CMP_BLOCK - 1 <= t1) & (n1 < n_cmp_valid)
    s = jnp.where(vis, s, -jnp.inf)
    m = jnp.max(s, axis=-1, keepdims=True)
    m = jnp.where(m == -jnp.inf, 0.0, m)
    p = jnp.exp(s - m)
    p = p / jnp.maximum(jnp.sum(p, axis=-1, keepdims=True), jnp.finfo(jnp.float32).tiny)
    o = _dot(p.astype(jnp.bfloat16), vc_ref[0, 0, 0])
    o_ref[...] = _unstack_heads(o, tq)

    psum = p[0:tq]
    for r in range(1, REP):
        psum = psum + p[r * tq:(r + 1) * tq]
    hi, mid, lo = _split3(psum)
    ov = ov_ref[...]
    imp = _dot(hi, ov) + _dot(mid, ov) + _dot(lo, ov)
    t = pos_base + qi * tq + lax.broadcasted_iota(jnp.int32, (tq, nsp), 0)
    jj = lax.broadcasted_iota(jnp.int32, (tq, nsp), 1)
    cur = _div_pow2(t, SLC_BLOCK)
    forced = (jj == 0) | (jj == cur) | (jj == cur - 1)
    causal = (jj * SLC_BLOCK <= t) & (jj < n_slc)
    imp = jnp.where(forced, jnp.inf, imp)
    imp = jnp.where(causal, imp, -jnp.inf)
    rank_ref[...] = jnp.zeros_like(rank_ref)
    t_last = pos_base + qi * tq + tq - 1
    for i0 in range(0, n_slc, RANK_GROUP):
        @pl.when(i0 * SLC_BLOCK <= t_last)
        def _():
            part = jnp.zeros((tq, nsp), jnp.float32)
            for i in range(i0, min(i0 + RANK_GROUP, n_slc)):
                col = imp[:, i:i + 1]
                beats = (col > imp) | ((col == imp) & (jj > i))
                part = part + beats.astype(jnp.float32)
            rank_ref[...] += part
    sel = (rank_ref[...] < SLC_TOPK) & (imp > -jnp.inf)
    sel_ref[0] = sel.astype(sel_ref.dtype)


def _overlap_matrix(nc, nsp):
    cs = jnp.arange(nc) * CMP_STRIDE
    ss = jnp.arange(nsp) * SLC_BLOCK
    ov = (cs[:, None] < ss[None, :] + SLC_BLOCK) & (cs[:, None] + CMP_BLOCK > ss[None, :])
    return ov.astype(jnp.bfloat16)


def _cmp_select(q_arr, row_blk0, bsz, nq, tq, kc, pos_base, n_cmp_valid, n_slc, nsp, sel_dtype):
    nc = kc.shape[3]
    rows = bsz * nq * tq
    kern = functools.partial(_cmp_select_kernel, tq=tq, pos_base=pos_base, n_cmp_valid=n_cmp_valid, n_slc=n_slc)
    return pl.pallas_call(
        kern,
        grid=(bsz, KV_HEADS, nq),
        in_specs=[
            pl.BlockSpec((tq, GROUP_COLS), lambda b, g, i: (row_blk0 + b * nq + i, g)),
            pl.BlockSpec((1, 1, 1, nc, HEAD_DIM), lambda b, g, i: (b, 0, g, 0, 0)),
            pl.BlockSpec((1, 1, 1, nc, HEAD_DIM), lambda b, g, i: (b, 1, g, 0, 0)),
            pl.BlockSpec((nc, nsp), lambda b, g, i: (0, 0)),
        ],
        out_specs=[
            pl.BlockSpec((tq, GROUP_COLS), lambda b, g, i: (b * nq + i, g)),
            pl.BlockSpec((1, tq, nsp), lambda b, g, i: (g, b * nq + i, 0)),
        ],
        out_shape=[jax.ShapeDtypeStruct((rows, NSA_HEADS * HEAD_DIM), jnp.float32),
                   jax.ShapeDtypeStruct((KV_HEADS, rows, nsp), sel_dtype)],
        scratch_shapes=[pltpu.VMEM((tq, nsp), jnp.float32)],
        compiler_params=_cparams(("parallel", "parallel", "parallel")),
        name="cmp_select",
    )(q_arr, kc, kc, _overlap_matrix(nc, nsp))


def _online_update(s, v, m_ref, l_ref, acc_ref, rows=slice(None)):
    m_prev = m_ref[rows]
    m_new = jnp.maximum(m_prev, jnp.max(s, axis=-1, keepdims=True))
    alpha = jnp.exp(m_prev - m_new)
    p = jnp.exp(s - m_new)
    l_ref[rows] = alpha * l_ref[rows] + jnp.sum(p, axis=-1, keepdims=True)
    acc_ref[rows] = alpha * acc_ref[rows] + _dot(p.astype(jnp.bfloat16), v)
    m_ref[rows] = m_new


def _flash_kernel(qi_ref, kk_ref, *refs, mode, tq, tk):
    if mode == "slc":
        q_ref, k_ref, v_ref, sel_ref, o_ref, m_ref, l_ref, acc_ref = refs
    else:
        q_ref, k_ref, v_ref, cq_ref, ck_ref, o_ref, m_ref, l_ref, acc_ref = refs
    step = pl.program_id(2)
    qi = qi_ref[step]
    kk = kk_ref[step]

    @pl.when(kk == 0)
    def _():
        m_ref[...] = jnp.full_like(m_ref, NEG)
        l_ref[...] = jnp.zeros_like(l_ref)
        acc_ref[...] = jnp.zeros_like(acc_ref)

    q = _stack_heads(q_ref[...])
    s = _dot_nt(q, k_ref[...]) * ATTN_SCALE
    t = qi * tq + lax.broadcasted_iota(jnp.int32, (tq, tk), 0)
    kp = kk * tk + lax.broadcasted_iota(jnp.int32, (tq, tk), 1)
    mask = kp <= t
    if mode == "slc":
        nsb = sel_ref.shape[2]
        jb = lax.broadcasted_iota(jnp.int32, (nsb, tk), 0)
        kb = kk * (tk // SLC_BLOCK) + _div_pow2(lax.broadcasted_iota(jnp.int32, (nsb, tk), 1), SLC_BLOCK)
        expand = (jb == kb).astype(jnp.bfloat16)
        mask = mask & (_dot(sel_ref[0], expand) > 0.5)
    add = jnp.where(mask, 0.0, NEG)
    if mode == "fox":
        add = jnp.concatenate([(cq_ref[0][:, r:r + 1] - ck_ref[0][r:r + 1, :]) + add for r in range(REP)], axis=0)
    else:
        add = jnp.concatenate([add] * REP, axis=0)
    _online_update(s + add, v_ref[...], m_ref, l_ref, acc_ref)

    @pl.when(kk == ((qi + 1) * tq - 1) // tk)
    def _():
        o_ref[...] = _unstack_heads(acc_ref[...] / l_ref[...], tq)


def _flash_prompt(mode, hb, b, t, q_col, k_col, v_col, sel=None, cq=None, ck=None):
    tq = _pick(t, (256, 128))
    tk = _pick(t, (512, 256, 128))
    nq = t // tq
    nkt = t // tk
    qb, kb, vb = q_col // GROUP_COLS, k_col // HEAD_DIM, v_col // HEAD_DIM
    pairs = [(i, k) for i in range(nq) for k in range(((i + 1) * tq - 1) // tk + 1)]
    qi_of = jnp.asarray([p[0] for p in pairs], jnp.int32)
    kk_of = jnp.asarray([p[1] for p in pairs], jnp.int32)

    in_specs = [
        pl.BlockSpec((tq, GROUP_COLS), lambda bi, g, s, qi, kk: (bi * nq + qi[s], qb + g)),
        pl.BlockSpec((tk, HEAD_DIM), lambda bi, g, s, qi, kk: (bi * nkt + kk[s], kb + g)),
        pl.BlockSpec((tk, HEAD_DIM), lambda bi, g, s, qi, kk: (bi * nkt + kk[s], vb + g)),
    ]
    args = [hb, hb, hb]
    if mode == "slc":
        in_specs.append(pl.BlockSpec((1, tq, sel.shape[2]), lambda bi, g, s, qi, kk: (g, bi * nq + qi[s], 0)))
        args.append(sel)
    if mode == "fox":
        in_specs.append(pl.BlockSpec((1, tq, REP), lambda bi, g, s, qi, kk: (g, bi * nq + qi[s], 0)))
        in_specs.append(pl.BlockSpec((1, REP, tk), lambda bi, g, s, qi, kk: (g, 0, bi * nkt + kk[s])))
        args += [cq, ck]
    kern = functools.partial(_flash_kernel, mode=mode, tq=tq, tk=tk)
    return pl.pallas_call(
        kern,
        grid_spec=pltpu.PrefetchScalarGridSpec(
            num_scalar_prefetch=2,
            grid=(b, KV_HEADS, len(pairs)),
            in_specs=in_specs,
            out_specs=pl.BlockSpec((tq, GROUP_COLS), lambda bi, g, s, qi, kk: (bi * nq + qi[s], g)),
            scratch_shapes=[pltpu.VMEM((REP * tq, 1), jnp.float32), pltpu.VMEM((REP * tq, 1), jnp.float32),
                            pltpu.VMEM((REP * tq, HEAD_DIM), jnp.float32)],
        ),
        out_shape=jax.ShapeDtypeStruct((b * t, KV_HEADS * GROUP_COLS), jnp.float32),
        compiler_params=_cparams(("parallel", "parallel", "arbitrary")),
        name="flash_" + mode,
    )(qi_of, kk_of, *args)


def _win_prompt_kernel(q_ref, *refs, tq, n_kb):
    k_refs, v_refs, o_ref = refs[:n_kb], refs[n_kb:2 * n_kb], refs[2 * n_kb]
    qi = pl.program_id(2)
    span = n_kb * tq
    k = jnp.concatenate([r[...] for r in k_refs], axis=0)
    v = jnp.concatenate([r[...] for r in v_refs], axis=0)
    s = _dot_nt(_stack_heads(q_ref[...]), k) * ATTN_SCALE
    t = qi * tq + lax.broadcasted_iota(jnp.int32, (tq, span), 0)
    kp = (qi - (n_kb - 1)) * tq + lax.broadcasted_iota(jnp.int32, (tq, span), 1)
    mask = (kp >= 0) & (kp <= t) & (t - kp < WINDOW)
    s = s + jnp.concatenate([jnp.where(mask, 0.0, NEG)] * REP, axis=0)
    p = jnp.exp(s - jnp.max(s, axis=-1, keepdims=True))
    o = _dot(p.astype(jnp.bfloat16), v) / jnp.sum(p, axis=-1, keepdims=True)
    o_ref[...] = _unstack_heads(o, tq)


def _win_prompt(hb, b, t, q_col, k_col, v_col):
    tq = _pick(t, (256, 128))
    assert WINDOW % tq == 0
    n_kb = WINDOW // tq + 1
    nq = t // tq
    qb, kb, vb = q_col // GROUP_COLS, k_col // HEAD_DIM, v_col // HEAD_DIM

    def kv_spec(col_blk, j):
        return pl.BlockSpec((tq, HEAD_DIM),
                            lambda bi, g, i: (bi * nq + jnp.maximum(i - (n_kb - 1) + j, 0), col_blk + g))

    in_specs = ([pl.BlockSpec((tq, GROUP_COLS), lambda bi, g, i: (bi * nq + i, qb + g))]
                + [kv_spec(kb, j) for j in range(n_kb)] + [kv_spec(vb, j) for j in range(n_kb)])
    kern = functools.partial(_win_prompt_kernel, tq=tq, n_kb=n_kb)
    return pl.pallas_call(
        kern,
        grid=(b, KV_HEADS, nq),
        in_specs=in_specs,
        out_specs=pl.BlockSpec((tq, GROUP_COLS), lambda bi, g, i: (bi * nq + i, g)),
        out_shape=jax.ShapeDtypeStruct((b * t, KV_HEADS * GROUP_COLS), jnp.float32),
        compiler_params=_cparams(("parallel", "parallel", "parallel")),
        name="win_prompt",
    )(*([hb] * (1 + 2 * n_kb)))


def _rows_of(g, tn):
    return slice(g * REP * tn, (g + 1) * REP * tn)


def _sample_init(m_ref, l_ref, acc_ref):
    m_ref[...] = jnp.full_like(m_ref, NEG)
    l_ref[...] = jnp.zeros_like(l_ref)
    acc_ref[...] = jnp.zeros_like(acc_ref)


def _sample_finish(o_ref, l_ref, acc_ref, tn):
    res = acc_ref[...] / l_ref[...]
    for g in range(KV_HEADS):
        o_ref[:, g * GROUP_COLS:(g + 1) * GROUP_COLS] = _unstack_heads(res[_rows_of(g, tn)], tn)


def _online_update_groups(scores, adds, values, m_ref, l_ref, acc_ref, tn):
    s = jnp.concatenate(scores, axis=0) * ATTN_SCALE + jnp.concatenate(adds, axis=0)
    m_prev = m_ref[...]
    m_new = jnp.maximum(m_prev, jnp.max(s, axis=-1, keepdims=True))
    alpha = jnp.exp(m_prev - m_new)
    p = jnp.exp(s - m_new)
    l_ref[...] = alpha * l_ref[...] + jnp.sum(p, axis=-1, keepdims=True)
    pb = p.astype(jnp.bfloat16)
    pv = jnp.concatenate([_dot(pb[_rows_of(g, tn)], values[g]) for g in range(KV_HEADS)], axis=0)
    acc_ref[...] = alpha * acc_ref[...] + pv
    m_ref[...] = m_new


def _kv_of(page_refs, g):
    def pick(ref, slab):
        return ref[0, pl.ds(slab, ref.shape[1] // ROW_SLABS, stride=ROW_SLABS), :]
    k = jnp.concatenate([pick(p, g) for p in page_refs], axis=0)
    v = jnp.concatenate([pick(p, KV_HEADS + g) for p in page_refs], axis=0)
    return k.astype(jnp.bfloat16), v.astype(jnp.bfloat16)


def _new_rows_mask(tn, page):
    tok = lax.broadcasted_iota(jnp.int32, (tn, page), 0)
    r = lax.broadcasted_iota(jnp.int32, (tn, page), 1)
    return r <= tok


def _paged_attn_kernel(pt_ref, *refs, mode, n_steps, tn):
    pages = refs[:PAGES_PER_STEP]
    if mode == "slc":
        q_ref, new_ref, sel_ref, selnew_ref, o_ref, m_ref, l_ref, acc_ref = refs[PAGES_PER_STEP:]
    else:
        q_ref, new_ref, cq_ref, ck_ref, cknew_ref, o_ref, m_ref, l_ref, acc_ref = refs[PAGES_PER_STEP:]
    s_id = pl.program_id(1)
    page = pages[0].shape[1] // ROW_SLABS
    nkeys = PAGES_PER_STEP * page

    @pl.when(s_id == 0)
    def _():
        _sample_init(m_ref, l_ref, acc_ref)

    def bias_rows(g, ck, add):
        return jnp.concatenate(
            [(cq_ref[0][:, g * REP + r:g * REP + r + 1] - ck[g * REP + r:g * REP + r + 1, :]) + add
             for r in range(REP)], axis=0)

    if mode == "slc":
        key_blk = _div_pow2(lax.broadcasted_iota(jnp.int32, (tn, nkeys), 1), SLC_BLOCK)
    scores, values, adds = [], [], []
    for g in range(KV_HEADS):
        q = _stack_heads(q_ref[:, g * GROUP_COLS:(g + 1) * GROUP_COLS])
        k, v = _kv_of(pages, g)
        scores.append(_dot_nt(q, k))
        values.append(v)
        if mode == "slc":
            flags = sel_ref[0, 0, g]
            picked = jnp.zeros((tn, nkeys), jnp.float32)
            for jb in range(flags.shape[1]):
                picked = jnp.where(key_blk == jb, flags[:, jb:jb + 1], picked)
            adds += [jnp.where(picked > 0.5, 0.0, NEG)] * REP
        else:
            adds.append(bias_rows(g, ck_ref[0], 0.0))
    _online_update_groups(scores, adds, values, m_ref, l_ref, acc_ref, tn)

    @pl.when(s_id == n_steps - 1)
    def _():
        new_keys = new_ref.shape[1] // ROW_SLABS
        causal = _new_rows_mask(tn, new_keys)
        scores, values, adds = [], [], []
        for g in range(KV_HEADS):
            q = _stack_heads(q_ref[:, g * GROUP_COLS:(g + 1) * GROUP_COLS])
            k, v = _kv_of([new_ref], g)
            scores.append(_dot_nt(q, k))
            values.append(v)
            if mode == "slc":
                adds += [jnp.where(causal & (selnew_ref[0, 0, g][:, 0:1] > 0.5), 0.0, NEG)] * REP
            else:
                adds.append(bias_rows(g, cknew_ref[0][:, :new_keys], jnp.where(causal, 0.0, NEG)))
        _online_update_groups(scores, adds, values, m_ref, l_ref, acc_ref, tn)
        _sample_finish(o_ref, l_ref, acc_ref, tn)


def _paged_attn(mode, page_table, pool, new_pad, q_s, tn, sel5=None, cq=None, ck=None):
    bsz, n_pages = page_table.shape
    _, slab_rows, cols = pool.shape
    page = slab_rows // ROW_SLABS
    n_steps = n_pages // PAGES_PER_STEP
    nkeys = PAGES_PER_STEP * page

    def page_spec(k):
        return pl.BlockSpec((1, slab_rows, cols), lambda b, s, pt: (pt[b, s * PAGES_PER_STEP + k], 0, 0))

    in_specs = [page_spec(k) for k in range(PAGES_PER_STEP)] + [
        pl.BlockSpec((tn, NSA_HEADS * HEAD_DIM), lambda b, s, pt: (b, 0)),
        pl.BlockSpec((1, new_pad.shape[1], cols), lambda b, s, pt: (b, 0, 0)),
    ]
    args = [pool] * PAGES_PER_STEP + [q_s, new_pad]
    if mode == "slc":
        nb = sel5.shape[4]
        in_specs += [pl.BlockSpec((1, 1, KV_HEADS, tn, nb), lambda b, s, pt: (b, s, 0, 0, 0)),
                     pl.BlockSpec((1, 1, KV_HEADS, tn, nb), lambda b, s, pt: (b, n_steps, 0, 0, 0))]
        args += [sel5, sel5]
    else:
        nh = cq.shape[2]
        in_specs += [pl.BlockSpec((1, tn, nh), lambda b, s, pt: (b, 0, 0)),
                     pl.BlockSpec((1, nh, nkeys), lambda b, s, pt: (b, 0, s)),
                     pl.BlockSpec((1, nh, page), lambda b, s, pt: (b, 0, n_pages))]
        args += [cq, ck, ck]
    rows = KV_HEADS * REP * tn
    kern = functools.partial(_paged_attn_kernel, mode=mode, n_steps=n_steps, tn=tn)
    return pl.pallas_call(
        kern,
        grid_spec=pltpu.PrefetchScalarGridSpec(
            num_scalar_prefetch=1,
            grid=(bsz, n_steps),
            in_specs=in_specs,
            out_specs=pl.BlockSpec((tn, NSA_HEADS * HEAD_DIM), lambda b, s, pt: (b, 0)),
            scratch_shapes=[pltpu.VMEM((rows, 1), jnp.float32), pltpu.VMEM((rows, 1), jnp.float32),
                            pltpu.VMEM((rows, HEAD_DIM), jnp.float32)],
        ),
        out_shape=jax.ShapeDtypeStruct((bsz * tn, NSA_HEADS * HEAD_DIM), jnp.float32),
        compiler_params=_cparams(("parallel", "arbitrary")),
        name="paged_" + mode,
    )(page_table, *args)


def _win_sample_kernel(q_ref, win_ref, new_ref, o_ref, m_ref, l_ref, acc_ref, *, tn, past):
    _sample_init(m_ref, l_ref, acc_ref)
    win_len = win_ref.shape[1] // ROW_SLABS
    page = new_ref.shape[1] // ROW_SLABS
    pos = past + lax.broadcasted_iota(jnp.int32, (tn, win_len), 0)
    kpos = past - win_len + lax.broadcasted_iota(jnp.int32, (tn, win_len), 1)
    wmask = (kpos <= pos) & (pos - kpos < WINDOW)
    add_w = [jnp.where(wmask, 0.0, NEG)] * (REP * KV_HEADS)
    add_n = [jnp.where(_new_rows_mask(tn, page), 0.0, NEG)] * (REP * KV_HEADS)
    for src, adds in ((new_ref, add_n), (win_ref, add_w)):
        scores, values = [], []
        for g in range(KV_HEADS):
            q = _stack_heads(q_ref[:, g * GROUP_COLS:(g + 1) * GROUP_COLS])
            k, v = _kv_of([src], g)
            scores.append(_dot_nt(q, k))
            values.append(v)
        _online_update_groups(scores, adds, values, m_ref, l_ref, acc_ref, tn)
    _sample_finish(o_ref, l_ref, acc_ref, tn)


def _win_sample(win_buf, new_pad, q_s, tn, past):
    bsz, win_rows, cols = win_buf.shape
    new_rows = new_pad.shape[1]
    rows = KV_HEADS * REP * tn
    kern = functools.partial(_win_sample_kernel, tn=tn, past=past)
    return pl.pallas_call(
        kern,
        grid=(bsz,),
        in_specs=[pl.BlockSpec((tn, NSA_HEADS * HEAD_DIM), lambda b: (b, 0)),
                  pl.BlockSpec((1, win_rows, cols), lambda b: (b, 0, 0)),
                  pl.BlockSpec((1, new_rows, cols), lambda b: (b, 0, 0))],
        out_specs=pl.BlockSpec((tn, NSA_HEADS * HEAD_DIM), lambda b: (b, 0)),
        out_shape=jax.ShapeDtypeStruct((bsz * tn, NSA_HEADS * HEAD_DIM), jnp.float32),
        scratch_shapes=[pltpu.VMEM((rows, 1), jnp.float32), pltpu.VMEM((rows, 1), jnp.float32),
                        pltpu.VMEM((rows, HEAD_DIM), jnp.float32)],
        compiler_params=_cparams(("parallel",)),
        name="win_sample",
    )(q_s, win_buf, new_pad)


def _merge_kernel(cmp_ref, slc_ref, win_ref, fox_ref, gate_ref, gn_ref, gf_ref, o_ref):
    gates = gate_ref[...]
    parts = []
    for h in range(NSA_HEADS):
        cols = slice(h * HEAD_DIM, (h + 1) * HEAD_DIM)
        c = N_BRANCH * h
        parts.append(gates[:, c:c + 1] * cmp_ref[:, cols] + gates[:, c + 1:c + 2] * slc_ref[:, cols]
                     + gates[:, c + 2:c + 3] * win_ref[:, cols])
    nsa = jnp.concatenate(parts, axis=1)
    nsa = nsa * lax.rsqrt(jnp.mean(nsa * nsa, axis=-1, keepdims=True) + RMS_EPS) * gn_ref[...]
    fox = fox_ref[...]
    fox = fox * lax.rsqrt(jnp.mean(fox * fox, axis=-1, keepdims=True) + RMS_EPS) * gf_ref[...]
    o_ref[...] = jnp.concatenate([nsa, fox], axis=1).astype(o_ref.dtype)


def _merge(o_cmp, o_slc, o_win, o_fox, slab, g_nsa, g_fox):
    n, w = o_cmp.shape
    tm = _pick(n, (256, 128, 64, 32, 16, 8))
    row = lambda i: (i, 0)
    fix = lambda i: (0, 0)
    return pl.pallas_call(
        _merge_kernel,
        grid=(n // tm,),
        in_specs=[pl.BlockSpec((tm, w), row)] * 4 + [pl.BlockSpec((tm, LANES), row), pl.BlockSpec((1, w), fix),
                                                      pl.BlockSpec((1, w), fix)],
        out_specs=pl.BlockSpec((tm, 2 * w), row),
        out_shape=jax.ShapeDtypeStruct((n, 2 * w), jnp.bfloat16),
        compiler_params=_cparams(("parallel",)),
        name="merge_heads",
    )(o_cmp, o_slc, o_win, o_fox, slab, g_nsa.reshape(1, w), g_fox.reshape(1, w))


def _outproj_kernel(a_ref, w_ref, x_ref, o_ref):
    o_ref[...] = x_ref[...] + _dot(a_ref[...], w_ref[...])


def _outproj(a, w, x):
    n, k = a.shape
    d = w.shape[1]
    tm = _pick(n, (768, 512, 384, 256, 128, 64, 32, 16, 8))
    tn = 512
    return pl.pallas_call(
        _outproj_kernel,
        grid=(n // tm, d // tn),
        in_specs=[pl.BlockSpec((tm, k), lambda i, j: (i, 0)), pl.BlockSpec((k, tn), lambda i, j: (0, j)),
                  pl.BlockSpec((tm, tn), lambda i, j: (i, j))],
        out_specs=pl.BlockSpec((tm, tn), lambda i, j: (i, j)),
        out_shape=jax.ShapeDtypeStruct((n, d), jnp.float32),
        compiler_params=_cparams(("parallel", "arbitrary")),
        name="outproj",
    )(a, w, x)


def _pack_bf16_pair(lo, hi):
    def bits(x):
        return lax.bitcast_convert_type(x.astype(jnp.bfloat16).astype(jnp.float32), jnp.uint32)
    return (bits(hi) & jnp.uint32(0xFFFF0000)) | lax.shift_right_logical(bits(lo), jnp.uint32(16))


def _unpack_bf16_pair(word):
    lo = lax.bitcast_convert_type(lax.shift_left(word, jnp.uint32(16)), jnp.float32)
    hi = lax.bitcast_convert_type(word & jnp.uint32(0xFFFF0000), jnp.float32)
    return lo, hi


def _router_kernel(h_ref, g_ref, w_ref, b_ref, xn_ref, eid_ref, gate_ref):
    x = h_ref[...]
    xn = x * lax.rsqrt(jnp.mean(x * x, axis=-1, keepdims=True) + RMS_EPS) * g_ref[...]
    half = xn.shape[1] // 2
    xn_ref[...] = _pack_bf16_pair(xn[:, :half], xn[:, half:])
    lg = jnp.dot(xn, w_ref[...], precision=lax.Precision.HIGHEST, preferred_element_type=jnp.float32) + b_ref[...]
    lane = lax.broadcasted_iota(jnp.int32, lg.shape, 1)
    lane_f = lane.astype(jnp.float32)
    ninf = -jnp.inf
    is_grp = lane < N_GROUPS
    gl = jnp.where(is_grp, lg, ninf)
    gmax = jnp.max(gl, axis=-1, keepdims=True)
    gsel = jnp.min(jnp.where(gl == gmax, lane_f, float(LANES)), axis=-1, keepdims=True)
    p_sel = 1.0 / jnp.sum(jnp.where(is_grp, jnp.exp(lg - gmax), 0.0), axis=-1, keepdims=True)
    e_lane = lane - N_GROUPS
    e_grp = _div_pow2(e_lane, EXPERTS_PER_GROUP).astype(jnp.float32)
    in_grp = (e_lane >= 0) & (e_lane < N_EXPERTS) & (e_grp == gsel)
    el = jnp.where(in_grp, lg, ninf)
    v1 = jnp.max(el, axis=-1, keepdims=True)
    i1 = jnp.min(jnp.where(el == v1, lane_f, float(LANES)), axis=-1, keepdims=True)
    el2 = jnp.where(lane_f == i1, ninf, el)
    v2 = jnp.max(el2, axis=-1, keepdims=True)
    i2 = jnp.min(jnp.where(el2 == v2, lane_f, float(LANES)), axis=-1, keepdims=True)
    e2 = jnp.exp(v2 - v1)
    den = 1.0 + e2
    g1 = p_sel * (1.0 / den)
    g2 = p_sel * (e2 / den)
    eid = jnp.where(lane == 0, i1 - N_GROUPS, jnp.where(lane == 1, i2 - N_GROUPS, 0.0))
    eid_ref[...] = eid.astype(jnp.int32)
    gate_ref[...] = jnp.where(lane == 0, g1, jnp.where(lane == 1, g2, 0.0))


def _router(h1, g_ffn, w_r, b_r):
    n, d = h1.shape
    tm = _pick(n, (256, 128, 64, 32, 16, 8))
    row = lambda i: (i, 0)
    fix = lambda i: (0, 0)
    return pl.pallas_call(
        _router_kernel,
        grid=(n // tm,),
        in_specs=[pl.BlockSpec((tm, d), row), pl.BlockSpec((1, d), fix), pl.BlockSpec((d, LANES), fix),
                  pl.BlockSpec((1, LANES), fix)],
        out_specs=[pl.BlockSpec((tm, d // 2), row), pl.BlockSpec((tm, LANES), row), pl.BlockSpec((tm, LANES), row)],
        out_shape=[jax.ShapeDtypeStruct((n, d // 2), jnp.uint32), jax.ShapeDtypeStruct((n, LANES), jnp.int32),
                   jax.ShapeDtypeStruct((n, LANES), jnp.float32)],
        compiler_params=_cparams(("parallel",)),
        name="router",
    )(h1, g_ffn.reshape(1, d), w_r, b_r)


SC_CORES = 2
SC_SUBCORES = 16
SC_WORKERS = SC_CORES * SC_SUBCORES
SC_ROW_BUFFER_BYTES = 448 * 1024
SC_ROW_ALIGN = 8


def _gather_rows(idx, table):
    n = idx.shape[0]
    _, d = table.shape
    assert n % (SC_WORKERS * SC_ROW_ALIGN) == 0, n
    per_worker = n // SC_WORKERS
    max_rows = SC_ROW_BUFFER_BYTES // (d * table.dtype.itemsize)
    rows = max(r for r in range(SC_ROW_ALIGN, max_rows + 1, SC_ROW_ALIGN) if per_worker % r == 0)
    n_chunks = per_worker // rows
    mesh = plsc.VectorSubcoreMesh(core_axis_name="c", subcore_axis_name="s")

    @functools.partial(
        pl.kernel, mesh=mesh,
        out_type=jax.ShapeDtypeStruct((n, d), table.dtype),
        scratch_types=[pltpu.VMEM((rows,), jnp.int32),
                       pltpu.VMEM((rows, d), table.dtype),
                       pltpu.SemaphoreType.DMA],
    )
    def gather(table_hbm, idx_hbm, out_hbm, idx_v, rows_v, sem):
        wid = lax.axis_index("s") * SC_CORES + lax.axis_index("c")
        base = wid * per_worker

        @pl.loop(0, n_chunks)
        def _(i):
            off = pl.multiple_of(base + i * rows, SC_ROW_ALIGN)
            pltpu.sync_copy(idx_hbm.at[pl.ds(off, rows)], idx_v)
            pltpu.async_copy(table_hbm.at[idx_v], rows_v, sem).wait()
            pltpu.sync_copy(rows_v, out_hbm.at[pl.ds(off, rows)])

    return gather(table, idx)


MOE_TM = 256
MOE_FT = 256
MOE_NT = 1024


def _gateup_kernel(tb_ref, e_ref, f_ref, first_ref, valid_ref, x_ref, wg_ref, wu_ref, h_ref, wgb_ref, wub_ref):
    s = pl.program_id(0)

    @pl.when(first_ref[s] == 1)
    def _():
        wgb_ref[...] = wg_ref[0].astype(jnp.bfloat16)
        wub_ref[...] = wu_ref[0].astype(jnp.bfloat16)

    @pl.when(valid_ref[s] == 1)
    def _():
        x = jnp.concatenate(_unpack_bf16_pair(x_ref[...]), axis=1).astype(jnp.bfloat16)
        a = _dot(x, wgb_ref[...])
        u = _dot(x, wub_ref[...])
        h_ref[...] = (a * (1.0 / (1.0 + jnp.exp(-a))) * u).astype(h_ref.dtype)


def _down_kernel(tb_ref, e_ref, f_ref, first_ref, valid_ref, h_ref, wd_ref, y_ref, wdb_ref):
    s = pl.program_id(0)

    @pl.when(first_ref[s] == 1)
    def _():
        wdb_ref[...] = wd_ref[0].astype(jnp.bfloat16)

    @pl.when(valid_ref[s] == 1)
    def _():
        y = _dot(h_ref[...], wdb_ref[...])
        half = y.shape[1] // 2
        y_ref[...] = _pack_bf16_pair(y[:, :half], y[:, half:])


def _work_list(block_e, start_blk, n_blk, total_blocks, n_tiles, n_tb_max):
    n_steps = n_tb_max * n_tiles
    s = jnp.minimum(jnp.arange(n_steps, dtype=jnp.int32), total_blocks * n_tiles - 1)
    valid = (jnp.arange(n_steps, dtype=jnp.int32) < total_blocks * n_tiles).astype(jnp.int32)
    e = block_e[s // n_tiles]
    local = s - n_tiles * start_blk[e]
    nb = jnp.maximum(n_blk[e], 1)
    tile = local // nb
    within = local % nb
    tb = start_blk[e] + within
    first = ((within == 0) & (valid == 1)).astype(jnp.int32)
    return tb.astype(jnp.int32), e.astype(jnp.int32), tile.astype(jnp.int32), first, valid


def _experts(xs, w_gate, w_up, w_down, block_e, start_blk, n_blk, total_blocks):
    a_pad, d = xs.shape[0], 2 * xs.shape[1]
    n_tb_max = a_pad // MOE_TM
    d_exp = w_gate.shape[2]
    wl1 = _work_list(block_e, start_blk, n_blk, total_blocks, d_exp // MOE_FT, n_tb_max)
    hidden = pl.pallas_call(
        _gateup_kernel,
        grid_spec=pltpu.PrefetchScalarGridSpec(
            num_scalar_prefetch=5,
            grid=(wl1[0].shape[0],),
            in_specs=[
                pl.BlockSpec((MOE_TM, d // 2), lambda s, tb, e, f, fi, va: (tb[s], 0)),
                pl.BlockSpec((1, d, MOE_FT), lambda s, tb, e, f, fi, va: (e[s], 0, f[s])),
                pl.BlockSpec((1, d, MOE_FT), lambda s, tb, e, f, fi, va: (e[s], 0, f[s])),
            ],
            out_specs=pl.BlockSpec((MOE_TM, MOE_FT), lambda s, tb, e, f, fi, va: (tb[s], f[s])),
            scratch_shapes=[pltpu.VMEM((d, MOE_FT), jnp.bfloat16), pltpu.VMEM((d, MOE_FT), jnp.bfloat16)],
        ),
        out_shape=jax.ShapeDtypeStruct((a_pad, d_exp), jnp.bfloat16),
        compiler_params=_cparams(("arbitrary",)),
        name="expert_gate_up",
    )(*wl1, xs, w_gate, w_up)
    wl2 = _work_list(block_e, start_blk, n_blk, total_blocks, d // MOE_NT, n_tb_max)
    return pl.pallas_call(
        _down_kernel,
        grid_spec=pltpu.PrefetchScalarGridSpec(
            num_scalar_prefetch=5,
            grid=(wl2[0].shape[0],),
            in_specs=[
                pl.BlockSpec((MOE_TM, d_exp), lambda s, tb, e, f, fi, va: (tb[s], 0)),
                pl.BlockSpec((1, d_exp, MOE_NT), lambda s, tb, e, f, fi, va: (e[s], 0, f[s])),
            ],
            out_specs=pl.BlockSpec((MOE_TM, MOE_NT // 2), lambda s, tb, e, f, fi, va: (tb[s], f[s])),
            scratch_shapes=[pltpu.VMEM((d_exp, MOE_NT), jnp.bfloat16)],
        ),
        out_shape=jax.ShapeDtypeStruct((a_pad, d // 2), jnp.uint32),
        compiler_params=_cparams(("arbitrary",)),
        name="expert_down",
    )(*wl2, hidden, w_down)


def _combine_kernel(h_ref, y0_ref, y1_ref, gate_ref, op_ref, os_ref, *, prompt_tiles):
    def write(o_ref):
        g = gate_ref[...]
        half = MOE_NT // 2
        for t in range(o_ref.shape[1] // MOE_NT):
            w = slice(t * half, (t + 1) * half)
            lo0, hi0 = _unpack_bf16_pair(y0_ref[0, :, w])
            lo1, hi1 = _unpack_bf16_pair(y1_ref[0, :, w])
            c = t * MOE_NT
            o_ref[:, c:c + half] = h_ref[:, c:c + half] + (g[:, 0:1] * lo0 + g[:, 1:2] * lo1)
            o_ref[:, c + half:c + MOE_NT] = h_ref[:, c + half:c + MOE_NT] + (g[:, 0:1] * hi0 + g[:, 1:2] * hi1)

    @pl.when(pl.program_id(0) < prompt_tiles)
    def _():
        write(op_ref)

    @pl.when(pl.program_id(0) >= prompt_tiles)
    def _():
        write(os_ref)


def _combine(h1, y2, gate_slab, n_p):
    n, d = h1.shape
    tm = _pick(math.gcd(n_p, n - n_p), (256, 128, 64, 32, 16, 8))
    pt = n_p // tm
    kern = functools.partial(_combine_kernel, prompt_tiles=pt)
    return pl.pallas_call(
        kern,
        grid=(n // tm,),
        in_specs=[pl.BlockSpec((tm, d), lambda i: (i, 0)), pl.BlockSpec((1, tm, d // 2), lambda i: (0, i, 0)),
                  pl.BlockSpec((1, tm, d // 2), lambda i: (1, i, 0)), pl.BlockSpec((tm, LANES), lambda i: (i, 0))],
        out_specs=[pl.BlockSpec((tm, d), lambda i: (jnp.minimum(i, pt - 1), 0)),
                   pl.BlockSpec((tm, d), lambda i: (jnp.maximum(i - pt, 0), 0))],
        out_shape=[jax.ShapeDtypeStruct((n_p, d), jnp.float32), jax.ShapeDtypeStruct((n - n_p, d), jnp.float32)],
        compiler_params=_cparams(("arbitrary",)),
        name="moe_combine",
    )(h1, y2, y2, gate_slab)


def _moe(h1, n_p, g_ffn, w_rg, b_rg, w_re, b_re, w_gate, w_up, w_down):
    n, d = h1.shape
    w_r = jnp.zeros((d, LANES), jnp.float32).at[:, :N_GROUPS].set(w_rg).at[:, N_GROUPS:N_GROUPS + N_EXPERTS].set(w_re)
    b_r = jnp.zeros((1, LANES), jnp.float32).at[0, :N_GROUPS].set(b_rg).at[0, N_GROUPS:N_GROUPS + N_EXPERTS].set(b_re)
    xn, eid_slab, gate_slab = _router(h1, g_ffn, w_r, b_r)
    flat = eid_slab[:, :EXPERT_TOPK].reshape(n * EXPERT_TOPK)
    onehot = (flat[:, None] == jnp.arange(N_EXPERTS, dtype=jnp.int32)[None, :]).astype(jnp.int32)
    rank = jnp.take_along_axis(jnp.cumsum(onehot, axis=0) - onehot, flat[:, None], axis=1)[:, 0]
    counts = jnp.sum(onehot, axis=0)
    n_blk = (counts + MOE_TM - 1) // MOE_TM
    start_blk = jnp.cumsum(n_blk) - n_blk
    total_blocks = jnp.sum(n_blk)
    slot = start_blk[flat] * MOE_TM + rank
    n_tb_max = -(-(n * EXPERT_TOPK) // MOE_TM) + N_EXPERTS
    a_pad = n_tb_max * MOE_TM
    src_tok = jnp.zeros((a_pad,), jnp.int32).at[slot].set(jnp.arange(n * EXPERT_TOPK, dtype=jnp.int32) // EXPERT_TOPK)
    end_blk = jnp.cumsum(n_blk)
    owner = jnp.sum((end_blk[None, :] <= jnp.arange(n_tb_max, dtype=jnp.int32)[:, None]).astype(jnp.int32), axis=1)
    block_e = jnp.minimum(owner, N_EXPERTS - 1).astype(jnp.int32)
    xs = _gather_rows(src_tok, xn)
    yb = _experts(xs, w_gate, w_up, w_down, block_e, start_blk.astype(jnp.int32), n_blk.astype(jnp.int32),
                  total_blocks.astype(jnp.int32))
    back = slot.reshape(n, EXPERT_TOPK).T.reshape(n * EXPERT_TOPK).astype(jnp.int32)
    y2 = _gather_rows(back, yb).reshape(EXPERT_TOPK, n, d // 2)
    return _combine(h1, y2, gate_slab, n_p)


def _rope_tables(pos):
    half = HEAD_DIM // 2
    inv = jnp.exp(-math.log(ROPE_THETA) * jnp.arange(half, dtype=jnp.float32) * (2.0 / HEAD_DIM))
    ang = pos.astype(jnp.float32)[:, None] * inv[None, :]
    cos, sin = jnp.cos(ang), jnp.sin(ang)
    return jnp.concatenate([cos, cos], axis=1), jnp.concatenate([-sin, sin], axis=1)


def _repack_weights(w_in, b_gate, b_forget, g_nsa_q, g_nsa_k, g_fox_q, g_fox_k):
    o1 = NSA_HEADS * HEAD_DIM
    o2 = o1 + N_BRANCH * KV_COLS
    o3 = o2 + N_GATE
    o4 = o3 + FOX_HEADS * HEAD_DIM
    o5 = o4 + KV_COLS
    d = w_in.shape[0]
    w_main = jnp.concatenate([w_in[:, :o2], w_in[:, o3:o5]], axis=1).astype(jnp.bfloat16)
    n_f = w_in.shape[1] - o5
    w_small = jnp.zeros((d, LANES), jnp.float32).at[:, :N_GATE].set(w_in[:, o2:o3])
    w_small = w_small.at[:, COL_LOGF:COL_LOGF + n_f].set(w_in[:, o5:]).astype(jnp.bfloat16)
    b_small = jnp.zeros((1, LANES), jnp.float32).at[0, :N_GATE].set(b_gate.reshape(N_GATE))
    b_small = b_small.at[0, COL_LOGF:COL_LOGF + n_f].set(b_forget)
    ones = jnp.ones((KV_HEADS * HEAD_DIM,), jnp.float32)
    gains = [jnp.tile(g_nsa_q, NSA_HEADS)]
    for br in range(N_BRANCH):
        gains += [jnp.tile(g_nsa_k[br], KV_HEADS), ones]
    gains += [jnp.tile(g_fox_q, FOX_HEADS), jnp.tile(g_fox_k, KV_HEADS), ones]
    return w_main, w_small, b_small, jnp.concatenate(gains).reshape(1, MAIN_COLS)


def _cmp_weights(w_k, w_v, pe_k, pe_v):
    def cat(w):
        return jnp.concatenate([w[:CMP_STRIDE], w[CMP_STRIDE:]], axis=2)
    return jnp.stack([cat(w_k), cat(w_v)]).astype(jnp.bfloat16), jnp.stack([pe_k, pe_v])


def _pad_rows(x, rows):
    return jnp.pad(x, ((0, 0), (0, rows - x.shape[1]), (0, 0)))


def kernel(x_prompt, x_sample, cache_nsa_cmp_kv, cache_nsa_slc_kv, cache_nsa_win_kv, cache_fox_kv, cache_fox_logf, page_table, g_attn_norm, w_in, b_nsa_gate, b_fox_forget, g_nsa_q, g_nsa_k, g_fox_q, g_fox_k, w_cmp_k, w_cmp_v, pe_cmp_k, pe_cmp_v, g_out_nsa, g_out_fox, w_out, g_ffn_norm, w_router_grp, b_router_grp, w_router_exp, b_router_exp, w_exp_gate, w_exp_up, w_exp_down):
    depth = w_in.shape[0]
    assert depth == 1, "single-layer step"
    bp, t, d = x_prompt.shape
    bs, tn, _ = x_sample.shape
    n_pages = page_table.shape[1]
    page = cache_fox_kv.shape[2]
    past = n_pages * page
    n_p, n_s = bp * t, bs * tn
    l = 0

    x_all = jnp.concatenate([x_prompt.reshape(n_p, d), x_sample.reshape(n_s, d)], axis=0)
    pos_all = jnp.concatenate([jnp.tile(jnp.arange(t, dtype=jnp.int32), bp),
                               jnp.tile(past + jnp.arange(tn, dtype=jnp.int32), bs)])
    cos_f, sin_s = _rope_tables(pos_all)
    w_main, w_small, b_small, gain_cols = _repack_weights(
        w_in[l], b_nsa_gate[l], b_fox_forget[l], g_nsa_q[l], g_nsa_k[l], g_fox_q[l], g_fox_k[l])

    xn = _rmsnorm(x_all, g_attn_norm[l], jnp.bfloat16)
    hf, hb = _inproj(xn, w_main, gain_cols, cos_f, sin_s)
    slab = _small_proj(xn, w_small, b_small)

    col_cmp, col_slc, col_win = (COL_KV_NSA + br * KV_COLS for br in range(N_BRANCH))
    w_cat, pe_cat = _cmp_weights(w_cmp_k[l], w_cmp_v[l], pe_cmp_k[l], pe_cmp_v[l])
    v_off = KV_HEADS * HEAD_DIM

    tq = _pick(t, (256, 128))
    kc_p = _compress_prompt(hf, bp, t, col_cmp, w_cat, pe_cat)
    n_slc_p = max(-(-t // SLC_BLOCK), SLC_TOPK)
    nsp_p = -(-n_slc_p // LANES) * LANES if n_slc_p > 64 else 64
    o_cmp_p, sel_p = _cmp_select(hb, 0, bp, t // tq, tq, kc_p, 0, t // CMP_STRIDE - CMP_RATIO + 1, n_slc_p, nsp_p,
                                 jnp.bfloat16)
    o_slc_p = _flash_prompt("slc", hb, bp, t, COL_Q_NSA, col_slc, col_slc + v_off, sel=sel_p)
    o_win_p = _win_prompt(hb, bp, t, COL_Q_NSA, col_win, col_win + v_off)
    c_p = _cumsum_prompt(slab[:n_p], bp, t)[:, COL_LOGF:COL_LOGF + FOX_HEADS]
    cq_p = c_p.reshape(n_p, KV_HEADS, REP).transpose(1, 0, 2)
    ck_p = c_p.T.reshape(KV_HEADS, REP, n_p)
    o_fox_p = _flash_prompt("fox", hb, bp, t, COL_Q_FOX, COL_KV_FOX, COL_KV_FOX + v_off, cq=cq_p, ck=ck_p)
    a_p = _merge(o_cmp_p, o_slc_p, o_win_p, o_fox_p, slab[:n_p], g_out_nsa[l], g_out_fox[l])

    hf_s, slab_s = hf[n_p:], slab[n_p:]
    q_nsa_s = hf_s[:, COL_Q_NSA:COL_Q_NSA + NSA_HEADS * HEAD_DIM]
    q_fox_s = hf_s[:, COL_Q_FOX:COL_Q_FOX + FOX_HEADS * HEAD_DIM]

    new_keys = -(-tn // CMP_STRIDE) * CMP_STRIDE

    def new_rows(col):
        rows = _pad_rows(hf_s[:, col:col + KV_COLS].reshape(bs, tn, KV_COLS), new_keys)
        return rows.reshape(bs, new_keys * ROW_SLABS, HEAD_DIM)

    pool = lambda c: c[l].reshape(c.shape[1], page * ROW_SLABS, HEAD_DIM)
    kc_s = _compress_sample(page_table, pool(cache_nsa_cmp_kv), new_rows(col_cmp), w_cat, pe_cat)
    t_ctx = past + tn
    n_cmp_s = -(-t_ctx // CMP_STRIDE) - CMP_RATIO + 1
    n_slc_s = max(-(-t_ctx // SLC_BLOCK), SLC_TOPK)
    blocks_per_step = PAGES_PER_STEP * page // SLC_BLOCK
    nsp_s = -(-(n_slc_s + 1) // LANES) * LANES
    nsp_s = -(-nsp_s // blocks_per_step) * blocks_per_step
    o_cmp_s, sel_s = _cmp_select(q_nsa_s, 0, bs, 1, tn, kc_s, past, n_cmp_s, n_slc_s, nsp_s, jnp.float32)
    sel5 = sel_s.reshape(KV_HEADS, bs, tn, nsp_s // blocks_per_step, blocks_per_step).transpose(1, 3, 0, 2, 4)
    o_slc_s = _paged_attn("slc", page_table, pool(cache_nsa_slc_kv), new_rows(col_slc), q_nsa_s, tn, sel5=sel5)
    win_buf = cache_nsa_win_kv[l].reshape(bs, -1, HEAD_DIM)
    o_win_s = _win_sample(win_buf, new_rows(col_win), q_nsa_s, tn, past)
    logf_s = slab_s[:, COL_LOGF:COL_LOGF + FOX_HEADS].reshape(bs, tn, FOX_HEADS)
    ck_s = _cumsum_sample(page_table, cache_fox_logf[l].transpose(0, 2, 1),
                          _pad_rows(logf_s, page).transpose(0, 2, 1))
    cq_s = ck_s[:, :, past:past + tn].transpose(0, 2, 1)
    o_fox_s = _paged_attn("fox", page_table, pool(cache_fox_kv), new_rows(COL_KV_FOX), q_fox_s, tn, cq=cq_s, ck=ck_s)
    a_s = _merge(o_cmp_s, o_slc_s, o_win_s, o_fox_s, slab_s, g_out_nsa[l], g_out_fox[l])

    h1 = _outproj(jnp.concatenate([a_p, a_s], axis=0), w_out[l].astype(jnp.bfloat16), x_all)
    y_p, y_s = _moe(h1, n_p, g_ffn_norm[l], w_router_grp[l], b_router_grp[l], w_router_exp[l], b_router_exp[l],
                    w_exp_gate[l], w_exp_up[l], w_exp_down[l])

    def kv_out(rows, col, bsz, tt):
        return rows[:, col:col + KV_COLS].reshape(1, bsz, tt, 2, KV_HEADS, HEAD_DIM)

    hf_p = hf[:n_p]
    win_p = kv_out(hf_p, col_win, bp, t)[:, :, t - min(WINDOW, t):]
    win_new = kv_out(hf_s, col_win, bs, tn)
    win_s = jnp.concatenate([cache_nsa_win_kv[l:l + 1][:, :, tn:], win_new], axis=2)
    logf_p = slab[:n_p, COL_LOGF:COL_LOGF + FOX_HEADS].reshape(1, bp, t, FOX_HEADS)
    return (y_p.reshape(bp, t, d), y_s.reshape(bs, tn, d),
            kv_out(hf_p, col_cmp, bp, t), kv_out(hf_s, col_cmp, bs, tn),
            kv_out(hf_p, col_slc, bp, t), kv_out(hf_s, col_slc, bs, tn),
            win_p, win_s,
            kv_out(hf_p, COL_KV_FOX, bp, t), kv_out(hf_s, COL_KV_FOX, bs, tn),
            logf_p, logf_s.reshape(1, bs, tn, FOX_HEADS))
```

```python
import functools
import math

import jax
import jax.numpy as jnp
from jax import lax
from jax.experimental import pallas as pl
from jax.experimental.pallas import tpu as pltpu
from jax.experimental.pallas import tpu_sc as plsc

HEAD_DIM = 128
NSA_HEADS = 16
FOX_HEADS = 16
KV_HEADS = 4
REP = NSA_HEADS // KV_HEADS
N_BRANCH = 3
CMP_BLOCK = 32
CMP_STRIDE = 16
CMP_RATIO = CMP_BLOCK // CMP_STRIDE
SLC_BLOCK = 64
SLC_TOPK = 16
WINDOW = 512
ROPE_THETA = 10000.0
RMS_EPS = 1e-6
N_GROUPS = 4
EXPERTS_PER_GROUP = 8
N_EXPERTS = N_GROUPS * EXPERTS_PER_GROUP
EXPERT_TOPK = 2
ATTN_SCALE = HEAD_DIM ** -0.5

LANES = 128
GROUP_COLS = REP * HEAD_DIM
KV_COLS = 2 * KV_HEADS * HEAD_DIM
ROW_SLABS = 2 * KV_HEADS
NEG = -1e30
VMEM_LIMIT = 48 * 1024 * 1024
PAGES_PER_STEP = 8

COL_Q_NSA = 0
COL_KV_NSA = NSA_HEADS * HEAD_DIM
COL_Q_FOX = COL_KV_NSA + N_BRANCH * KV_COLS
COL_KV_FOX = COL_Q_FOX + FOX_HEADS * HEAD_DIM
MAIN_COLS = COL_KV_FOX + KV_COLS
N_GATE = N_BRANCH * NSA_HEADS
COL_LOGF = N_GATE


def _pick(n, cands):
    for c in cands:
        if n % c == 0:
            return c
    raise ValueError(f"no tile in {cands} divides {n}")


def _cparams(sem, vmem=VMEM_LIMIT):
    return pltpu.CompilerParams(dimension_semantics=sem, vmem_limit_bytes=vmem)


def _split3(x):
    hi = x.astype(jnp.bfloat16)
    r1 = x - hi.astype(jnp.float32)
    mid = r1.astype(jnp.bfloat16)
    lo = (r1 - mid.astype(jnp.float32)).astype(jnp.bfloat16)
    return hi, mid, lo


def _dot(a, b):
    return jnp.dot(a, b, preferred_element_type=jnp.float32)


def _dot_nt(a, b):
    return lax.dot_general(a, b, (((1,), (1,)), ((), ())), preferred_element_type=jnp.float32)


def _rms_kernel(x_ref, g_ref, o_ref):
    x = x_ref[...]
    ms = jnp.mean(x * x, axis=-1, keepdims=True)
    o_ref[...] = (x * lax.rsqrt(ms + RMS_EPS) * g_ref[...]).astype(o_ref.dtype)


def _rmsnorm(x, g, out_dtype):
    n, d = x.shape
    tm = _pick(n, (256, 128, 64, 32, 16, 8))
    return pl.pallas_call(
        _rms_kernel,
        grid=(n // tm,),
        in_specs=[pl.BlockSpec((tm, d), lambda i: (i, 0)), pl.BlockSpec((1, d), lambda i: (0, 0))],
        out_specs=pl.BlockSpec((tm, d), lambda i: (i, 0)),
        out_shape=jax.ShapeDtypeStruct((n, d), out_dtype),
        compiler_params=_cparams(("parallel",)),
        name="rmsnorm",
    )(x, g.reshape(1, d))


N_COLBLK = MAIN_COLS // GROUP_COLS


def _colblock_kinds():
    kinds = []
    kinds += ["rope"] * (NSA_HEADS // REP)
    for _ in range(N_BRANCH):
        kinds += ["rope", "id"]
    kinds += ["norm"] * (FOX_HEADS // REP)
    kinds += ["norm", "id"]
    assert len(kinds) == N_COLBLK
    return kinds


def _any_eq(j, vals):
    return functools.reduce(jnp.logical_or, [j == v for v in vals])


def _inproj_kernel(x_ref, w_ref, gain_ref, cos_ref, sin_ref, of_ref, ob_ref):
    j = pl.program_id(1)
    kinds = _colblock_kinds()
    id_blocks = [i for i, k in enumerate(kinds) if k == "id"]
    rope_blocks = [i for i, k in enumerate(kinds) if k == "rope"]
    norm_blocks = [i for i, k in enumerate(kinds) if k == "norm"]
    acc = _dot(x_ref[...], w_ref[...])

    def write(y):
        of_ref[...] = y
        ob_ref[...] = y.astype(ob_ref.dtype)

    def normed(rope):
        outs = []
        for s in range(GROUP_COLS // HEAD_DIM):
            h = acc[:, s * HEAD_DIM:(s + 1) * HEAD_DIM]
            g = gain_ref[:, s * HEAD_DIM:(s + 1) * HEAD_DIM]
            y = h * lax.rsqrt(jnp.mean(h * h, axis=-1, keepdims=True) + RMS_EPS) * g
            if rope:
                y = y * cos_ref[...] + pltpu.roll(y, HEAD_DIM // 2, 1) * sin_ref[...]
            outs.append(y)
        return jnp.concatenate(outs, axis=1)

    @pl.when(_any_eq(j, id_blocks))
    def _():
        write(acc)

    @pl.when(_any_eq(j, norm_blocks))
    def _():
        write(normed(False))

    @pl.when(_any_eq(j, rope_blocks))
    def _():
        write(normed(True))


def _inproj(xn, w_main, gain_cols, cos_f, sin_s):
    n, d = xn.shape
    tm = _pick(n, (768, 512, 384, 256, 128, 64, 32, 16, 8))
    tn = GROUP_COLS
    return pl.pallas_call(
        _inproj_kernel,
        grid=(n // tm, MAIN_COLS // tn),
        in_specs=[
            pl.BlockSpec((tm, d), lambda i, j: (i, 0)),
            pl.BlockSpec((d, tn), lambda i, j: (0, j)),
            pl.BlockSpec((1, tn), lambda i, j: (0, j)),
            pl.BlockSpec((tm, HEAD_DIM), lambda i, j: (i, 0)),
            pl.BlockSpec((tm, HEAD_DIM), lambda i, j: (i, 0)),
        ],
        out_specs=[pl.BlockSpec((tm, tn), lambda i, j: (i, j)), pl.BlockSpec((tm, tn), lambda i, j: (i, j))],
        out_shape=[jax.ShapeDtypeStruct((n, MAIN_COLS), jnp.float32),
                   jax.ShapeDtypeStruct((n, MAIN_COLS), jnp.bfloat16)],
        compiler_params=_cparams(("parallel", "arbitrary")),
        name="inproj",
    )(xn, w_main, gain_cols, cos_f, sin_s)


def _small_kernel(x_ref, w_ref, b_ref, o_ref):
    z = _dot(x_ref[...], w_ref[...]) + b_ref[...]
    lane = lax.broadcasted_iota(jnp.int32, z.shape, 1)
    sig = 1.0 / (1.0 + jnp.exp(-z))
    logsig = jnp.minimum(z, 0.0) - jnp.log1p(jnp.exp(-jnp.abs(z)))
    o_ref[...] = jnp.where(lane < N_GATE, sig, logsig)


def _small_proj(xn, w_small, b_small):
    n, d = xn.shape
    tm = _pick(n, (768, 512, 384, 256, 128, 64, 32, 16, 8))
    return pl.pallas_call(
        _small_kernel,
        grid=(n // tm,),
        in_specs=[pl.BlockSpec((tm, d), lambda i: (i, 0)), pl.BlockSpec((d, LANES), lambda i: (0, 0)),
                  pl.BlockSpec((1, LANES), lambda i: (0, 0))],
        out_specs=pl.BlockSpec((tm, LANES), lambda i: (i, 0)),
        out_shape=jax.ShapeDtypeStruct((n, LANES), jnp.float32),
        compiler_params=_cparams(("parallel",)),
        name="small_proj",
    )(xn, w_small, b_small)


def _tri_lower(n):
    r = lax.broadcasted_iota(jnp.int32, (n, n), 0)
    c = lax.broadcasted_iota(jnp.int32, (n, n), 1)
    return (c <= r).astype(jnp.bfloat16)


def _block_cumsum(x, tri):
    hi, mid, lo = _split3(x)
    return _dot(tri, hi) + _dot(tri, mid) + _dot(tri, lo)


def _cumsum_prompt_kernel(x_ref, o_ref, carry_ref):
    @pl.when(pl.program_id(1) == 0)
    def _():
        carry_ref[...] = jnp.zeros_like(carry_ref)

    blk = x_ref.shape[0]
    c = _block_cumsum(x_ref[...], _tri_lower(blk)) + carry_ref[...]
    o_ref[...] = c
    carry_ref[...] = c[blk - 1:blk, :]


def _cumsum_prompt(slab, b, t):
    blk = _pick(t, (512, 256, 128))
    nb = t // blk
    return pl.pallas_call(
        _cumsum_prompt_kernel,
        grid=(b, nb),
        in_specs=[pl.BlockSpec((blk, LANES), lambda i, j: (i * nb + j, 0))],
        out_specs=pl.BlockSpec((blk, LANES), lambda i, j: (i * nb + j, 0)),
        out_shape=jax.ShapeDtypeStruct((b * t, LANES), jnp.float32),
        scratch_shapes=[pltpu.VMEM((1, LANES), jnp.float32)],
        compiler_params=_cparams(("parallel", "arbitrary")),
        name="cumsum_prompt",
    )(slab)


CUMSUM_PAGES_PER_STEP = 16


def _cumsum_sample_kernel(pt_ref, *refs, n_steps, pps):
    pages = refs[:pps]
    new_ref, o_ref, carry_ref = refs[pps:]
    s = pl.program_id(1)
    _, h, page = pages[0].shape

    @pl.when(s == 0)
    def _():
        carry_ref[...] = jnp.zeros_like(carry_ref)

    def local(x):
        r = lax.broadcasted_iota(jnp.int32, (page, page), 0)
        c = lax.broadcasted_iota(jnp.int32, (page, page), 1)
        tri = (r <= c).astype(jnp.bfloat16)
        hi, mid, lo = _split3(x)
        return _dot(hi, tri) + _dot(mid, tri) + _dot(lo, tri)

    @pl.when(s < n_steps)
    def _():
        loc = local(jnp.concatenate([p[0] for p in pages], axis=0))
        carry = carry_ref[...]
        for p in range(pps):
            blk = loc[p * h:(p + 1) * h] + carry
            o_ref[0, :, p * page:(p + 1) * page] = blk
            carry = blk[:, page - 1:page]
        carry_ref[...] = carry

    @pl.when(s == n_steps)
    def _():
        o_ref[0, :, 0:page] = local(new_ref[0]) + carry_ref[...]
        o_ref[0, :, page:] = jnp.zeros((h, (pps - 1) * page), jnp.float32)


def _cumsum_sample(page_table, pool_t, new_t):
    bsz, n_pages = page_table.shape
    _, h, page = pool_t.shape
    pps = CUMSUM_PAGES_PER_STEP if n_pages % CUMSUM_PAGES_PER_STEP == 0 else PAGES_PER_STEP
    n_steps = n_pages // pps
    width = pps * page

    def page_spec(k):
        return pl.BlockSpec(
            (1, h, page), lambda b, s, pt: (pt[b, jnp.minimum(s, n_steps - 1) * pps + k], 0, 0))

    kern = functools.partial(_cumsum_sample_kernel, n_steps=n_steps, pps=pps)
    return pl.pallas_call(
        kern,
        grid_spec=pltpu.PrefetchScalarGridSpec(
            num_scalar_prefetch=1,
            grid=(bsz, n_steps + 1),
            in_specs=[page_spec(k) for k in range(pps)] + [
                pl.BlockSpec((1, h, page), lambda b, s, pt: (b, 0, 0))],
            out_specs=pl.BlockSpec((1, h, width), lambda b, s, pt: (b, 0, s)),
            scratch_shapes=[pltpu.VMEM((h, 1), jnp.float32)],
        ),
        out_shape=jax.ShapeDtypeStruct((bsz, h, (n_steps + 1) * width), jnp.float32),
        compiler_params=_cparams(("parallel", "arbitrary")),
        name="cumsum_sample",
    )(page_table, *([pool_t] * pps), new_t)


def _pe_term(pe_ref, w_ref):
    acc = jnp.zeros((16, HEAD_DIM), jnp.float32)
    for j in range(CMP_STRIDE):
        lo = jnp.broadcast_to(pe_ref[j:j + 1, :], (16, HEAD_DIM)).astype(jnp.bfloat16)
        hi = jnp.broadcast_to(pe_ref[CMP_STRIDE + j:CMP_STRIDE + j + 1, :], (16, HEAD_DIM)).astype(jnp.bfloat16)
        w = w_ref[j]
        acc = acc + _dot(lo, w[:, :HEAD_DIM]) + _dot(hi, w[:, HEAD_DIM:])
    return acc[0:1]


def _compress_prompt_kernel(x_ref, w_ref, pe_ref, o_ref, *, n_chunks):
    acc = jnp.zeros((n_chunks, 2 * HEAD_DIM), jnp.float32)
    for j in range(CMP_STRIDE):
        xj = x_ref[pl.ds(j, n_chunks, stride=CMP_STRIDE), :].astype(jnp.bfloat16)
        acc = acc + _dot(xj, w_ref[0, j])
    nxt = pltpu.roll(acc[:, HEAD_DIM:], n_chunks - 1, 0)
    o_ref[0, 0, 0] = (acc[:, :HEAD_DIM] + nxt + _pe_term(pe_ref.at[0], w_ref.at[0])).astype(o_ref.dtype)


def _compress_prompt(hf, b, t, col0, w_cat, pe):
    n_chunks = t // CMP_STRIDE
    rows_blk = col0 // HEAD_DIM
    kern = functools.partial(_compress_prompt_kernel, n_chunks=n_chunks)
    return pl.pallas_call(
        kern,
        grid=(b, 2, KV_HEADS),
        in_specs=[
            pl.BlockSpec((t, HEAD_DIM), lambda i, kv, g: (i, rows_blk + kv * KV_HEADS + g)),
            pl.BlockSpec((1, CMP_STRIDE, HEAD_DIM, 2 * HEAD_DIM), lambda i, kv, g: (kv, 0, 0, 0)),
            pl.BlockSpec((1, CMP_BLOCK, HEAD_DIM), lambda i, kv, g: (kv, 0, 0)),
        ],
        out_specs=pl.BlockSpec((1, 1, 1, n_chunks, HEAD_DIM), lambda i, kv, g: (i, kv, g, 0, 0)),
        out_shape=jax.ShapeDtypeStruct((b, 2, KV_HEADS, n_chunks, HEAD_DIM), jnp.bfloat16),
        compiler_params=_cparams(("parallel", "parallel", "parallel")),
        name="compress_prompt",
    )(hf, w_cat, pe)


def _compress_sample_kernel(pt_ref, *refs, n_steps):
    pages = refs[:PAGES_PER_STEP]
    next_ref, new_ref, w_ref, pe_ref, o_ref = refs[PAGES_PER_STEP:]
    s = pl.program_id(1)
    chunk_rows = CMP_STRIDE * ROW_SLABS
    cpp = pages[0].shape[1] // chunk_rows
    n_out = PAGES_PER_STEP * cpp * ROW_SLABS
    is_last = s == n_steps - 1

    def chunk_lhs(ref, c):
        base = c * chunk_rows
        return jnp.concatenate([ref[0, base + j * ROW_SLABS:base + (j + 1) * ROW_SLABS, :]
                                for j in range(CMP_STRIDE)], axis=1)

    lhs = [chunk_lhs(p, c) for p in pages for c in range(cpp)]
    lhs.append(jnp.where(is_last, chunk_lhs(new_ref, 0), chunk_lhs(next_ref, 0)))
    out = _dot(jnp.concatenate(lhs, axis=0).astype(jnp.bfloat16), w_ref[...])
    pe = _dot(pe_ref[...].astype(jnp.bfloat16), w_ref[...])
    own, nxt = out[:n_out], out[ROW_SLABS:n_out + ROW_SLABS]
    d = HEAD_DIM
    is_k = (lax.broadcasted_iota(jnp.int32, (n_out, d), 0) & (ROW_SLABS - 1)) < KV_HEADS
    pe_k = pe[0:1, 0:d] + pe[1:2, d:2 * d]
    pe_v = pe[2:3, 2 * d:3 * d] + pe[3:4, 3 * d:4 * d]
    o_ref[0] = jnp.where(is_k, own[:, 0:d] + nxt[:, d:2 * d] + pe_k, own[:, 2 * d:3 * d] + nxt[:, 3 * d:4 * d] + pe_v)


def _compress_sample(page_table, pool, new_pad, w_cat, pe):
    bsz, n_pages = page_table.shape
    _, page, cols = pool.shape
    assert n_pages % PAGES_PER_STEP == 0
    n_steps = n_pages // PAGES_PER_STEP
    chunk_rows = CMP_STRIDE * ROW_SLABS
    cps = PAGES_PER_STEP * page // chunk_rows
    w_flat = jnp.concatenate([w_cat[0].reshape(CMP_STRIDE * HEAD_DIM, 2 * HEAD_DIM),
                              w_cat[1].reshape(CMP_STRIDE * HEAD_DIM, 2 * HEAD_DIM)], axis=1)
    pe_lhs = jnp.pad(pe.reshape(2 * CMP_RATIO, CMP_STRIDE * HEAD_DIM), ((0, ROW_SLABS - 2 * CMP_RATIO), (0, 0)))

    def page_spec(k):
        return pl.BlockSpec((1, page, cols), lambda b, s, pt: (pt[b, s * PAGES_PER_STEP + k], 0, 0))

    next_spec = pl.BlockSpec(
        (1, chunk_rows, cols), lambda b, s, pt: (pt[b, jnp.minimum((s + 1) * PAGES_PER_STEP, n_pages - 1)], 0, 0))
    kern = functools.partial(_compress_sample_kernel, n_steps=n_steps)
    out = pl.pallas_call(
        kern,
        grid_spec=pltpu.PrefetchScalarGridSpec(
            num_scalar_prefetch=1,
            grid=(bsz, n_steps),
            in_specs=[page_spec(k) for k in range(PAGES_PER_STEP)] + [
                next_spec,
                pl.BlockSpec((1, chunk_rows, cols), lambda b, s, pt: (b, 0, 0)),
                pl.BlockSpec(w_flat.shape, lambda b, s, pt: (0, 0)),
                pl.BlockSpec(pe_lhs.shape, lambda b, s, pt: (0, 0)),
            ],
            out_specs=pl.BlockSpec((1, cps * ROW_SLABS, HEAD_DIM), lambda b, s, pt: (b, s, 0)),
        ),
        out_shape=jax.ShapeDtypeStruct((bsz, n_steps * cps * ROW_SLABS, HEAD_DIM), jnp.float32),
        compiler_params=_cparams(("parallel", "arbitrary")),
        name="compress_sample",
    )(page_table, *([pool] * (PAGES_PER_STEP + 1)), new_pad, w_flat, pe_lhs)
    out = out.reshape(bsz, n_steps * cps, 2, KV_HEADS, HEAD_DIM).transpose(0, 2, 3, 1, 4)
    return out.astype(jnp.bfloat16)


def _stack_heads(q):
    return jnp.concatenate([q[:, r * HEAD_DIM:(r + 1) * HEAD_DIM] for r in range(REP)], axis=0).astype(jnp.bfloat16)


def _div_pow2(x, c):
    assert c & (c - 1) == 0
    return lax.shift_right_arithmetic(x, jnp.int32(c.bit_length() - 1))


def _unstack_heads(o, tq):
    return jnp.concatenate([o[r * tq:(r + 1) * tq] for r in range(REP)], axis=1)


RANK_GROUP = 8


def _cmp_select_kernel(q_ref, kc_ref, vc_ref, ov_ref, o_ref, sel_ref, rank_ref, *, tq, pos_base, n_cmp_valid, n_slc):
    qi = pl.program_id(2)
    nc = kc_ref.shape[3]
    nsp = ov_ref.shape[1]
    q = _stack_heads(q_ref[...])
    s = _dot_nt(q, kc_ref[0, 0, 0]) * ATTN_SCALE
    assert tq & (tq - 1) == 0
    t1 = pos_base + qi * tq + (lax.broadcasted_iota(jnp.int32, (REP * tq, nc), 0) & (tq - 1))
    n1 = lax.broadcasted_iota(jnp.int32, (REP * tq, nc), 1)
    vis = (n1 * CMP_STRIDE + CMP_BLOCK - 1 <= t1) & (n1 < n_cmp_valid)
    s = jnp.where(vis, s, -jnp.inf)
    m = jnp.max(s, axis=-1, keepdims=True)
    m = jnp.where(m == -jnp.inf, 0.0, m)
    p = jnp.exp(s - m)
    p = p / jnp.maximum(jnp.sum(p, axis=-1, keepdims=True), jnp.finfo(jnp.float32).tiny)
    o = _dot(p.astype(jnp.bfloat16), vc_ref[0, 0, 0])
    o_ref[...] = _unstack_heads(o, tq)

    psum = p[0:tq]
    for r in range(1, REP):
        psum = psum + p[r * tq:(r + 1) * tq]
    hi, mid, lo = _split3(psum)
    ov = ov_ref[...]
    imp = _dot(hi, ov) + _dot(mid, ov) + _dot(lo, ov)
    t = pos_base + qi * tq + lax.broadcasted_iota(jnp.int32, (tq, nsp), 0)
    jj = lax.broadcasted_iota(jnp.int32, (tq, nsp), 1)
    cur = _div_pow2(t, SLC_BLOCK)
    forced = (jj == 0) | (jj == cur) | (jj == cur - 1)
    causal = (jj * SLC_BLOCK <= t) & (jj < n_slc)
    imp = jnp.where(forced, jnp.inf, imp)
    imp = jnp.where(causal, imp, -jnp.inf)
    rank_ref[...] = jnp.zeros_like(rank_ref)
    t_last = pos_base + qi * tq + tq - 1
    for i0 in range(0, n_slc, RANK_GROUP):
        @pl.when(i0 * SLC_BLOCK <= t_last)
        def _():
            part = jnp.zeros((tq, nsp), jnp.float32)
            for i in range(i0, min(i0 + RANK_GROUP, n_slc)):
                col = imp[:, i:i + 1]
                beats = (col > imp) | ((col == imp) & (jj > i))
                part = part + beats.astype(jnp.float32)
            rank_ref[...] += part
    sel = (rank_ref[...] < SLC_TOPK) & (imp > -jnp.inf)
    sel_ref[0] = sel.astype(sel_ref.dtype)


def _overlap_matrix(nc, nsp):
    cs = jnp.arange(nc) * CMP_STRIDE
    ss = jnp.arange(nsp) * SLC_BLOCK
    ov = (cs[:, None] < ss[None, :] + SLC_BLOCK) & (cs[:, None] + CMP_BLOCK > ss[None, :])
    return ov.astype(jnp.bfloat16)


def _cmp_select(q_arr, row_blk0, bsz, nq, tq, kc, pos_base, n_cmp_valid, n_slc, nsp, sel_dtype):
    nc = kc.shape[3]
    rows = bsz * nq * tq
    kern = functools.partial(_cmp_select_kernel, tq=tq, pos_base=pos_base, n_cmp_valid=n_cmp_valid, n_slc=n_slc)
    return pl.pallas_call(
        kern,
        grid=(bsz, KV_HEADS, nq),
        in_specs=[
            pl.BlockSpec((tq, GROUP_COLS), lambda b, g, i: (row_blk0 + b * nq + i, g)),
            pl.BlockSpec((1, 1, 1, nc, HEAD_DIM), lambda b, g, i: (b, 0, g, 0, 0)),
            pl.BlockSpec((1, 1, 1, nc, HEAD_DIM), lambda b, g, i: (b, 1, g, 0, 0)),
            pl.BlockSpec((nc, nsp), lambda b, g, i: (0, 0)),
        ],
        out_specs=[
            pl.BlockSpec((tq, GROUP_COLS), lambda b, g, i: (b * nq + i, g)),
            pl.BlockSpec((1, tq, nsp), lambda b, g, i: (g, b * nq + i, 0)),
        ],
        out_shape=[jax.ShapeDtypeStruct((rows, NSA_HEADS * HEAD_DIM), jnp.float32),
                   jax.ShapeDtypeStruct((KV_HEADS, rows, nsp), sel_dtype)],
        scratch_shapes=[pltpu.VMEM((tq, nsp), jnp.float32)],
        compiler_params=_cparams(("parallel", "parallel", "parallel")),
        name="cmp_select",
    )(q_arr, kc, kc, _overlap_matrix(nc, nsp))


def _online_update(s, v, m_ref, l_ref, acc_ref, rows=slice(None)):
    m_prev = m_ref[rows]
    m_new = jnp.maximum(m_prev, jnp.max(s, axis=-1, keepdims=True))
    alpha = jnp.exp(m_prev - m_new)
    p = jnp.exp(s - m_new)
    l_ref[rows] = alpha * l_ref[rows] + jnp.sum(p, axis=-1, keepdims=True)
    acc_ref[rows] = alpha * acc_ref[rows] + _dot(p.astype(jnp.bfloat16), v)
    m_ref[rows] = m_new


def _flash_kernel(qi_ref, kk_ref, *refs, mode, tq, tk):
    if mode == "slc":
        q_ref, k_ref, v_ref, sel_ref, o_ref, m_ref, l_ref, acc_ref = refs
    else:
        q_ref, k_ref, v_ref, cq_ref, ck_ref, o_ref, m_ref, l_ref, acc_ref = refs
    step = pl.program_id(2)
    qi = qi_ref[step]
    kk = kk_ref[step]

    @pl.when(kk == 0)
    def _():
        m_ref[...] = jnp.full_like(m_ref, NEG)
        l_ref[...] = jnp.zeros_like(l_ref)
        acc_ref[...] = jnp.zeros_like(acc_ref)

    q = _stack_heads(q_ref[...])
    s = _dot_nt(q, k_ref[...]) * ATTN_SCALE
    t = qi * tq + lax.broadcasted_iota(jnp.int32, (tq, tk), 0)
    kp = kk * tk + lax.broadcasted_iota(jnp.int32, (tq, tk), 1)
    mask = kp <= t
    if mode == "slc":
        nsb = sel_ref.shape[2]
        jb = lax.broadcasted_iota(jnp.int32, (nsb, tk), 0)
        kb = kk * (tk // SLC_BLOCK) + _div_pow2(lax.broadcasted_iota(jnp.int32, (nsb, tk), 1), SLC_BLOCK)
        expand = (jb == kb).astype(jnp.bfloat16)
        mask = mask & (_dot(sel_ref[0], expand) > 0.5)
    add = jnp.where(mask, 0.0, NEG)
    if mode == "fox":
        add = jnp.concatenate([(cq_ref[0][:, r:r + 1] - ck_ref[0][r:r + 1, :]) + add for r in range(REP)], axis=0)
    else:
        add = jnp.concatenate([add] * REP, axis=0)
    _online_update(s + add, v_ref[...], m_ref, l_ref, acc_ref)

    @pl.when(kk == ((qi + 1) * tq - 1) // tk)
    def _():
        o_ref[...] = _unstack_heads(acc_ref[...] / l_ref[...], tq)


def _flash_prompt(mode, hb, b, t, q_col, k_col, v_col, sel=None, cq=None, ck=None):
    tq = _pick(t, (256, 128))
    tk = _pick(t, (512, 256, 128))
    nq = t // tq
    nkt = t // tk
    qb, kb, vb = q_col // GROUP_COLS, k_col // HEAD_DIM, v_col // HEAD_DIM
    pairs = [(i, k) for i in range(nq) for k in range(((i + 1) * tq - 1) // tk + 1)]
    qi_of = jnp.asarray([p[0] for p in pairs], jnp.int32)
    kk_of = jnp.asarray([p[1] for p in pairs], jnp.int32)

    in_specs = [
        pl.BlockSpec((tq, GROUP_COLS), lambda bi, g, s, qi, kk: (bi * nq + qi[s], qb + g)),
        pl.BlockSpec((tk, HEAD_DIM), lambda bi, g, s, qi, kk: (bi * nkt + kk[s], kb + g)),
        pl.BlockSpec((tk, HEAD_DIM), lambda bi, g, s, qi, kk: (bi * nkt + kk[s], vb + g)),
    ]
    args = [hb, hb, hb]
    if mode == "slc":
        in_specs.append(pl.BlockSpec((1, tq, sel.shape[2]), lambda bi, g, s, qi, kk: (g, bi * nq + qi[s], 0)))
        args.append(sel)
    if mode == "fox":
        in_specs.append(pl.BlockSpec((1, tq, REP), lambda bi, g, s, qi, kk: (g, bi * nq + qi[s], 0)))
        in_specs.append(pl.BlockSpec((1, REP, tk), lambda bi, g, s, qi, kk: (g, 0, bi * nkt + kk[s])))
        args += [cq, ck]
    kern = functools.partial(_flash_kernel, mode=mode, tq=tq, tk=tk)
    return pl.pallas_call(
        kern,
        grid_spec=pltpu.PrefetchScalarGridSpec(
            num_scalar_prefetch=2,
            grid=(b, KV_HEADS, len(pairs)),
            in_specs=in_specs,
            out_specs=pl.BlockSpec((tq, GROUP_COLS), lambda bi, g, s, qi, kk: (bi * nq + qi[s], g)),
            scratch_shapes=[pltpu.VMEM((REP * tq, 1), jnp.float32), pltpu.VMEM((REP * tq, 1), jnp.float32),
                            pltpu.VMEM((REP * tq, HEAD_DIM), jnp.float32)],
        ),
        out_shape=jax.ShapeDtypeStruct((b * t, KV_HEADS * GROUP_COLS), jnp.float32),
        compiler_params=_cparams(("parallel", "parallel", "arbitrary")),
        name="flash_" + mode,
    )(qi_of, kk_of, *args)


def _win_prompt_kernel(q_ref, *refs, tq, n_kb):
    k_refs, v_refs, o_ref = refs[:n_kb], refs[n_kb:2 * n_kb], refs[2 * n_kb]
    qi = pl.program_id(2)
    span = n_kb * tq
    k = jnp.concatenate([r[...] for r in k_refs], axis=0)
    v = jnp.concatenate([r[...] for r in v_refs], axis=0)
    s = _dot_nt(_stack_heads(q_ref[...]), k) * ATTN_SCALE
    t = qi * tq + lax.broadcasted_iota(jnp.int32, (tq, span), 0)
    kp = (qi - (n_kb - 1)) * tq + lax.broadcasted_iota(jnp.int32, (tq, span), 1)
    mask = (kp >= 0) & (kp <= t) & (t - kp < WINDOW)
    s = s + jnp.concatenate([jnp.where(mask, 0.0, NEG)] * REP, axis=0)
    p = jnp.exp(s - jnp.max(s, axis=-1, keepdims=True))
    o = _dot(p.astype(jnp.bfloat16), v) / jnp.sum(p, axis=-1, keepdims=True)
    o_ref[...] = _unstack_heads(o, tq)


def _win_prompt(hb, b, t, q_col, k_col, v_col):
    tq = _pick(t, (256, 128))
    assert WINDOW % tq == 0
    n_kb = WINDOW // tq + 1
    nq = t // tq
    qb, kb, vb = q_col // GROUP_COLS, k_col // HEAD_DIM, v_col // HEAD_DIM

    def kv_spec(col_blk, j):
        return pl.BlockSpec((tq, HEAD_DIM),
                            lambda bi, g, i: (bi * nq + jnp.maximum(i - (n_kb - 1) + j, 0), col_blk + g))

    in_specs = ([pl.BlockSpec((tq, GROUP_COLS), lambda bi, g, i: (bi * nq + i, qb + g))]
                + [kv_spec(kb, j) for j in range(n_kb)] + [kv_spec(vb, j) for j in range(n_kb)])
    kern = functools.partial(_win_prompt_kernel, tq=tq, n_kb=n_kb)
    return pl.pallas_call(
        kern,
        grid=(b, KV_HEADS, nq),
        in_specs=in_specs,
        out_specs=pl.BlockSpec((tq, GROUP_COLS), lambda bi, g, i: (bi * nq + i, g)),
        out_shape=jax.ShapeDtypeStruct((b * t, KV_HEADS * GROUP_COLS), jnp.float32),
        compiler_params=_cparams(("parallel", "parallel", "parallel")),
        name="win_prompt",
    )(*([hb] * (1 + 2 * n_kb)))


def _rows_of(g, tn):
    return slice(g * REP * tn, (g + 1) * REP * tn)


def _sample_init(m_ref, l_ref, acc_ref):
    m_ref[...] = jnp.full_like(m_ref, NEG)
    l_ref[...] = jnp.zeros_like(l_ref)
    acc_ref[...] = jnp.zeros_like(acc_ref)


def _sample_finish(o_ref, l_ref, acc_ref, tn):
    res = acc_ref[...] / l_ref[...]
    for g in range(KV_HEADS):
        o_ref[:, g * GROUP_COLS:(g + 1) * GROUP_COLS] = _unstack_heads(res[_rows_of(g, tn)], tn)


def _online_update_groups(scores, adds, values, m_ref, l_ref, acc_ref, tn):
    s = jnp.concatenate(scores, axis=0) * ATTN_SCALE + jnp.concatenate(adds, axis=0)
    m_prev = m_ref[...]
    m_new = jnp.maximum(m_prev, jnp.max(s, axis=-1, keepdims=True))
    alpha = jnp.exp(m_prev - m_new)
    p = jnp.exp(s - m_new)
    l_ref[...] = alpha * l_ref[...] + jnp.sum(p, axis=-1, keepdims=True)
    pb = p.astype(jnp.bfloat16)
    pv = jnp.concatenate([_dot(pb[_rows_of(g, tn)], values[g]) for g in range(KV_HEADS)], axis=0)
    acc_ref[...] = alpha * acc_ref[...] + pv
    m_ref[...] = m_new


def _kv_of(page_refs, g):
    def pick(ref, slab):
        return ref[0, pl.ds(slab, ref.shape[1] // ROW_SLABS, stride=ROW_SLABS), :]
    k = jnp.concatenate([pick(p, g) for p in page_refs], axis=0)
    v = jnp.concatenate([pick(p, KV_HEADS + g) for p in page_refs], axis=0)
    return k.astype(jnp.bfloat16), v.astype(jnp.bfloat16)


def _new_rows_mask(tn, page):
    tok = lax.broadcasted_iota(jnp.int32, (tn, page), 0)
    r = lax.broadcasted_iota(jnp.int32, (tn, page), 1)
    return r <= tok


def _paged_attn_kernel(pt_ref, *refs, mode, n_steps, tn):
    pages = refs[:PAGES_PER_STEP]
    if mode == "slc":
        q_ref, new_ref, sel_ref, selnew_ref, o_ref, m_ref, l_ref, acc_ref = refs[PAGES_PER_STEP:]
    else:
        q_ref, new_ref, cq_ref, ck_ref, cknew_ref, o_ref, m_ref, l_ref, acc_ref = refs[PAGES_PER_STEP:]
    s_id = pl.program_id(1)
    page = pages[0].shape[1] // ROW_SLABS
    nkeys = PAGES_PER_STEP * page

    @pl.when(s_id == 0)
    def _():
        _sample_init(m_ref, l_ref, acc_ref)

    def bias_rows(g, ck, add):
        return jnp.concatenate(
            [(cq_ref[0][:, g * REP + r:g * REP + r + 1] - ck[g * REP + r:g * REP + r + 1, :]) + add
             for r in range(REP)], axis=0)

    if mode == "slc":
        key_blk = _div_pow2(lax.broadcasted_iota(jnp.int32, (tn, nkeys), 1), SLC_BLOCK)
    scores, values, adds = [], [], []
    for g in range(KV_HEADS):
        q = _stack_heads(q_ref[:, g * GROUP_COLS:(g + 1) * GROUP_COLS])
        k, v = _kv_of(pages, g)
        scores.append(_dot_nt(q, k))
        values.append(v)
        if mode == "slc":
            flags = sel_ref[0, 0, g]
            picked = jnp.zeros((tn, nkeys), jnp.float32)
            for jb in range(flags.shape[1]):
                picked = jnp.where(key_blk == jb, flags[:, jb:jb + 1], picked)
            adds += [jnp.where(picked > 0.5, 0.0, NEG)] * REP
        else:
            adds.append(bias_rows(g, ck_ref[0], 0.0))
    _online_update_groups(scores, adds, values, m_ref, l_ref, acc_ref, tn)

    @pl.when(s_id == n_steps - 1)
    def _():
        new_keys = new_ref.shape[1] // ROW_SLABS
        causal = _new_rows_mask(tn, new_keys)
        scores, values, adds = [], [], []
        for g in range(KV_HEADS):
            q = _stack_heads(q_ref[:, g * GROUP_COLS:(g + 1) * GROUP_COLS])
            k, v = _kv_of([new_ref], g)
            scores.append(_dot_nt(q, k))
            values.append(v)
            if mode == "slc":
                adds += [jnp.where(causal & (selnew_ref[0, 0, g][:, 0:1] > 0.5), 0.0, NEG)] * REP
            else:
                adds.append(bias_rows(g, cknew_ref[0][:, :new_keys], jnp.where(causal, 0.0, NEG)))
        _online_update_groups(scores, adds, values, m_ref, l_ref, acc_ref, tn)
        _sample_finish(o_ref, l_ref, acc_ref, tn)


def _paged_attn(mode, page_table, pool, new_pad, q_s, tn, sel5=None, cq=None, ck=None):
    bsz, n_pages = page_table.shape
    _, slab_rows, cols = pool.shape
    page = slab_rows // ROW_SLABS
    n_steps = n_pages // PAGES_PER_STEP
    nkeys = PAGES_PER_STEP * page

    def page_spec(k):
        return pl.BlockSpec((1, slab_rows, cols), lambda b, s, pt: (pt[b, s * PAGES_PER_STEP + k], 0, 0))

    in_specs = [page_spec(k) for k in range(PAGES_PER_STEP)] + [
        pl.BlockSpec((tn, NSA_HEADS * HEAD_DIM), lambda b, s, pt: (b, 0)),
        pl.BlockSpec((1, new_pad.shape[1], cols), lambda b, s, pt: (b, 0, 0)),
    ]
    args = [pool] * PAGES_PER_STEP + [q_s, new_pad]
    if mode == "slc":
        nb = sel5.shape[4]
        in_specs += [pl.BlockSpec((1, 1, KV_HEADS, tn, nb), lambda b, s, pt: (b, s, 0, 0, 0)),
                     pl.BlockSpec((1, 1, KV_HEADS, tn, nb), lambda b, s, pt: (b, n_steps, 0, 0, 0))]
        args += [sel5, sel5]
    else:
        nh = cq.shape[2]
        in_specs += [pl.BlockSpec((1, tn, nh), lambda b, s, pt: (b, 0, 0)),
                     pl.BlockSpec((1, nh, nkeys), lambda b, s, pt: (b, 0, s)),
                     pl.BlockSpec((1, nh, page), lambda b, s, pt: (b, 0, n_pages))]
        args += [cq, ck, ck]
    rows = KV_HEADS * REP * tn
    kern = functools.partial(_paged_attn_kernel, mode=mode, n_steps=n_steps, tn=tn)
    return pl.pallas_call(
        kern,
        grid_spec=pltpu.PrefetchScalarGridSpec(
            num_scalar_prefetch=1,
            grid=(bsz, n_steps),
            in_specs=in_specs,
            out_specs=pl.BlockSpec((tn, NSA_HEADS * HEAD_DIM), lambda b, s, pt: (b, 0)),
            scratch_shapes=[pltpu.VMEM((rows, 1), jnp.float32), pltpu.VMEM((rows, 1), jnp.float32),
                            pltpu.VMEM((rows, HEAD_DIM), jnp.float32)],
        ),
        out_shape=jax.ShapeDtypeStruct((bsz * tn, NSA_HEADS * HEAD_DIM), jnp.float32),
        compiler_params=_cparams(("parallel", "arbitrary")),
        name="paged_" + mode,
    )(page_table, *args)


def _win_sample_kernel(q_ref, win_ref, new_ref, o_ref, m_ref, l_ref, acc_ref, *, tn, past):
    _sample_init(m_ref, l_ref, acc_ref)
    win_len = win_ref.shape[1] // ROW_SLABS
    page = new_ref.shape[1] // ROW_SLABS
    pos = past + lax.broadcasted_iota(jnp.int32, (tn, win_len), 0)
    kpos = past - win_len + lax.broadcasted_iota(jnp.int32, (tn, win_len), 1)
    wmask = (kpos <= pos) & (pos - kpos < WINDOW)
    add_w = [jnp.where(wmask, 0.0, NEG)] * (REP * KV_HEADS)
    add_n = [jnp.where(_new_rows_mask(tn, page), 0.0, NEG)] * (REP * KV_HEADS)
    for src, adds in ((new_ref, add_n), (win_ref, add_w)):
        scores, values = [], []
        for g in range(KV_HEADS):
            q = _stack_heads(q_ref[:, g * GROUP_COLS:(g + 1) * GROUP_COLS])
            k, v = _kv_of([src], g)
            scores.append(_dot_nt(q, k))
            values.append(v)
        _online_update_groups(scores, adds, values, m_ref, l_ref, acc_ref, tn)
    _sample_finish(o_ref, l_ref, acc_ref, tn)


def _win_sample(win_buf, new_pad, q_s, tn, past):
    bsz, win_rows, cols = win_buf.shape
    new_rows = new_pad.shape[1]
    rows = KV_HEADS * REP * tn
    kern = functools.partial(_win_sample_kernel, tn=tn, past=past)
    return pl.pallas_call(
        kern,
        grid=(bsz,),
        in_specs=[pl.BlockSpec((tn, NSA_HEADS * HEAD_DIM), lambda b: (b, 0)),
                  pl.BlockSpec((1, win_rows, cols), lambda b: (b, 0, 0)),
                  pl.BlockSpec((1, new_rows, cols), lambda b: (b, 0, 0))],
        out_specs=pl.BlockSpec((tn, NSA_HEADS * HEAD_DIM), lambda b: (b, 0)),
        out_shape=jax.ShapeDtypeStruct((bsz * tn, NSA_HEADS * HEAD_DIM), jnp.float32),
        scratch_shapes=[pltpu.VMEM((rows, 1), jnp.float32), pltpu.VMEM((rows, 1), jnp.float32),
                        pltpu.VMEM((rows, HEAD_DIM), jnp.float32)],
        compiler_params=_cparams(("parallel",)),
        name="win_sample",
    )(q_s, win_buf, new_pad)


def _merge_kernel(cmp_ref, slc_ref, win_ref, fox_ref, gate_ref, gn_ref, gf_ref, o_ref):
    gates = gate_ref[...]
    parts = []
    for h in range(NSA_HEADS):
        cols = slice(h * HEAD_DIM, (h + 1) * HEAD_DIM)
        c = N_BRANCH * h
        parts.append(gates[:, c:c + 1] * cmp_ref[:, cols] + gates[:, c + 1:c + 2] * slc_ref[:, cols]
                     + gates[:, c + 2:c + 3] * win_ref[:, cols])
    nsa = jnp.concatenate(parts, axis=1)
    nsa = nsa * lax.rsqrt(jnp.mean(nsa * nsa, axis=-1, keepdims=True) + RMS_EPS) * gn_ref[...]
    fox = fox_ref[...]
    fox = fox * lax.rsqrt(jnp.mean(fox * fox, axis=-1, keepdims=True) + RMS_EPS) * gf_ref[...]
    o_ref[...] = jnp.concatenate([nsa, fox], axis=1).astype(o_ref.dtype)


def _merge(o_cmp, o_slc, o_win, o_fox, slab, g_nsa, g_fox):
    n, w = o_cmp.shape
    tm = _pick(n, (256, 128, 64, 32, 16, 8))
    row = lambda i: (i, 0)
    fix = lambda i: (0, 0)
    return pl.pallas_call(
        _merge_kernel,
        grid=(n // tm,),
        in_specs=[pl.BlockSpec((tm, w), row)] * 4 + [pl.BlockSpec((tm, LANES), row), pl.BlockSpec((1, w), fix),
                                                      pl.BlockSpec((1, w), fix)],
        out_specs=pl.BlockSpec((tm, 2 * w), row),
        out_shape=jax.ShapeDtypeStruct((n, 2 * w), jnp.bfloat16),
        compiler_params=_cparams(("parallel",)),
        name="merge_heads",
    )(o_cmp, o_slc, o_win, o_fox, slab, g_nsa.reshape(1, w), g_fox.reshape(1, w))


def _outproj_kernel(a_ref, w_ref, x_ref, o_ref):
    o_ref[...] = x_ref[...] + _dot(a_ref[...], w_ref[...])


def _outproj(a, w, x):
    n, k = a.shape
    d = w.shape[1]
    tm = _pick(n, (768, 512, 384, 256, 128, 64, 32, 16, 8))
    tn = 512
    return pl.pallas_call(
        _outproj_kernel,
        grid=(n // tm, d // tn),
        in_specs=[pl.BlockSpec((tm, k), lambda i, j: (i, 0)), pl.BlockSpec((k, tn), lambda i, j: (0, j)),
                  pl.BlockSpec((tm, tn), lambda i, j: (i, j))],
        out_specs=pl.BlockSpec((tm, tn), lambda i, j: (i, j)),
        out_shape=jax.ShapeDtypeStruct((n, d), jnp.float32),
        compiler_params=_cparams(("parallel", "arbitrary")),
        name="outproj",
    )(a, w, x)


def _pack_bf16_pair(lo, hi):
    def bits(x):
        return lax.bitcast_convert_type(x.astype(jnp.bfloat16).astype(jnp.float32), jnp.uint32)
    return (bits(hi) & jnp.uint32(0xFFFF0000)) | lax.shift_right_logical(bits(lo), jnp.uint32(16))


def _unpack_bf16_pair(word):
    lo = lax.bitcast_convert_type(lax.shift_left(word, jnp.uint32(16)), jnp.float32)
    hi = lax.bitcast_convert_type(word & jnp.uint32(0xFFFF0000), jnp.float32)
    return lo, hi


def _router_kernel(h_ref, g_ref, w_ref, b_ref, xn_ref, eid_ref, gate_ref):
    x = h_ref[...]
    xn = x * lax.rsqrt(jnp.mean(x * x, axis=-1, keepdims=True) + RMS_EPS) * g_ref[...]
    half = xn.shape[1] // 2
    xn_ref[...] = _pack_bf16_pair(xn[:, :half], xn[:, half:])
    lg = jnp.dot(xn, w_ref[...], precision=lax.Precision.HIGHEST, preferred_element_type=jnp.float32) + b_ref[...]
    lane = lax.broadcasted_iota(jnp.int32, lg.shape, 1)
    lane_f = lane.astype(jnp.float32)
    ninf = -jnp.inf
    is_grp = lane < N_GROUPS
    gl = jnp.where(is_grp, lg, ninf)
    gmax = jnp.max(gl, axis=-1, keepdims=True)
    gsel = jnp.min(jnp.where(gl == gmax, lane_f, float(LANES)), axis=-1, keepdims=True)
    p_sel = 1.0 / jnp.sum(jnp.where(is_grp, jnp.exp(lg - gmax), 0.0), axis=-1, keepdims=True)
    e_lane = lane - N_GROUPS
    e_grp = _div_pow2(e_lane, EXPERTS_PER_GROUP).astype(jnp.float32)
    in_grp = (e_lane >= 0) & (e_lane < N_EXPERTS) & (e_grp == gsel)
    el = jnp.where(in_grp, lg, ninf)
    v1 = jnp.max(el, axis=-1, keepdims=True)
    i1 = jnp.min(jnp.where(el == v1, lane_f, float(LANES)), axis=-1, keepdims=True)
    el2 = jnp.where(lane_f == i1, ninf, el)
    v2 = jnp.max(el2, axis=-1, keepdims=True)
    i2 = jnp.min(jnp.where(el2 == v2, lane_f, float(LANES)), axis=-1, keepdims=True)
    e2 = jnp.exp(v2 - v1)
    den = 1.0 + e2
    g1 = p_sel * (1.0 / den)
    g2 = p_sel * (e2 / den)
    eid = jnp.where(lane == 0, i1 - N_GROUPS, jnp.where(lane == 1, i2 - N_GROUPS, 0.0))
    eid_ref[...] = eid.astype(jnp.int32)
    gate_ref[...] = jnp.where(lane == 0, g1, jnp.where(lane == 1, g2, 0.0))


def _router(h1, g_ffn, w_r, b_r):
    n, d = h1.shape
    tm = _pick(n, (256, 128, 64, 32, 16, 8))
    row = lambda i: (i, 0)
    fix = lambda i: (0, 0)
    return pl.pallas_call(
        _router_kernel,
        grid=(n // tm,),
        in_specs=[pl.BlockSpec((tm, d), row), pl.BlockSpec((1, d), fix), pl.BlockSpec((d, LANES), fix),
                  pl.BlockSpec((1, LANES), fix)],
        out_specs=[pl.BlockSpec((tm, d // 2), row), pl.BlockSpec((tm, LANES), row), pl.BlockSpec((tm, LANES), row)],
        out_shape=[jax.ShapeDtypeStruct((n, d // 2), jnp.uint32), jax.ShapeDtypeStruct((n, LANES), jnp.int32),
                   jax.ShapeDtypeStruct((n, LANES), jnp.float32)],
        compiler_params=_cparams(("parallel",)),
        name="router",
    )(h1, g_ffn.reshape(1, d), w_r, b_r)


SC_CORES = 2
SC_SUBCORES = 16
SC_WORKERS = SC_CORES * SC_SUBCORES
SC_ROW_BUFFER_BYTES = 448 * 1024
SC_ROW_ALIGN = 8


def _gather_rows(idx, table):
    n = idx.shape[0]
    _, d = table.shape
    assert n % (SC_WORKERS * SC_ROW_ALIGN) == 0, n
    per_worker = n // SC_WORKERS
    max_rows = SC_ROW_BUFFER_BYTES // (d * table.dtype.itemsize)
    rows = max(r for r in range(SC_ROW_ALIGN, max_rows + 1, SC_ROW_ALIGN) if per_worker % r == 0)
    n_chunks = per_worker // rows
    mesh = plsc.VectorSubcoreMesh(core_axis_name="c", subcore_axis_name="s")

    @functools.partial(
        pl.kernel, mesh=mesh,
        out_type=jax.ShapeDtypeStruct((n, d), table.dtype),
        scratch_types=[pltpu.VMEM((rows,), jnp.int32),
                       pltpu.VMEM((rows, d), table.dtype),
                       pltpu.SemaphoreType.DMA],
    )
    def gather(table_hbm, idx_hbm, out_hbm, idx_v, rows_v, sem):
        wid = lax.axis_index("s") * SC_CORES + lax.axis_index("c")
        base = wid * per_worker

        @pl.loop(0, n_chunks)
        def _(i):
            off = pl.multiple_of(base + i * rows, SC_ROW_ALIGN)
            pltpu.sync_copy(idx_hbm.at[pl.ds(off, rows)], idx_v)
            pltpu.async_copy(table_hbm.at[idx_v], rows_v, sem).wait()
            pltpu.sync_copy(rows_v, out_hbm.at[pl.ds(off, rows)])

    return gather(table, idx)


MOE_TM = 256
MOE_FT = 512
MOE_NT = 1024


def _gateup_kernel(tb_ref, e_ref, f_ref, first_ref, valid_ref, x_ref, wg_ref, wu_ref, h_ref, wgb_ref, wub_ref):
    s = pl.program_id(0)

    @pl.when(first_ref[s] == 1)
    def _():
        wgb_ref[...] = wg_ref[0].astype(jnp.bfloat16)
        wub_ref[...] = wu_ref[0].astype(jnp.bfloat16)

    @pl.when(valid_ref[s] == 1)
    def _():
        x = jnp.concatenate(_unpack_bf16_pair(x_ref[...]), axis=1).astype(jnp.bfloat16)
        a = _dot(x, wgb_ref[...])
        u = _dot(x, wub_ref[...])
        h_ref[...] = (a * (1.0 / (1.0 + jnp.exp(-a))) * u).astype(h_ref.dtype)


def _down_kernel(tb_ref, e_ref, f_ref, first_ref, valid_ref, h_ref, wd_ref, y_ref, wdb_ref):
    s = pl.program_id(0)

    @pl.when(first_ref[s] == 1)
    def _():
        wdb_ref[...] = wd_ref[0].astype(jnp.bfloat16)

    @pl.when(valid_ref[s] == 1)
    def _():
        y = _dot(h_ref[...], wdb_ref[...])
        half = y.shape[1] // 2
        y_ref[...] = _pack_bf16_pair(y[:, :half], y[:, half:])


def _work_list(block_e, start_blk, n_blk, total_blocks, n_tiles, n_tb_max):
    n_steps = n_tb_max * n_tiles
    s = jnp.minimum(jnp.arange(n_steps, dtype=jnp.int32), total_blocks * n_tiles - 1)
    valid = (jnp.arange(n_steps, dtype=jnp.int32) < total_blocks * n_tiles).astype(jnp.int32)
    e = block_e[s // n_tiles]
    local = s - n_tiles * start_blk[e]
    nb = jnp.maximum(n_blk[e], 1)
    tile = local // nb
    within = local % nb
    tb = start_blk[e] + within
    first = ((within == 0) & (valid == 1)).astype(jnp.int32)
    return tb.astype(jnp.int32), e.astype(jnp.int32), tile.astype(jnp.int32), first, valid


def _experts(xs, w_gate, w_up, w_down, block_e, start_blk, n_blk, total_blocks):
    a_pad, d = xs.shape[0], 2 * xs.shape[1]
    n_tb_max = a_pad // MOE_TM
    d_exp = w_gate.shape[2]
    wl1 = _work_list(block_e, start_blk, n_blk, total_blocks, d_exp // MOE_FT, n_tb_max)
    hidden = pl.pallas_call(
        _gateup_kernel,
        grid_spec=pltpu.PrefetchScalarGridSpec(
            num_scalar_prefetch=5,
            grid=(wl1[0].shape[0],),
            in_specs=[
                pl.BlockSpec((MOE_TM, d // 2), lambda s, tb, e, f, fi, va: (tb[s], 0)),
                pl.BlockSpec((1, d, MOE_FT), lambda s, tb, e, f, fi, va: (e[s], 0, f[s])),
                pl.BlockSpec((1, d, MOE_FT), lambda s, tb, e, f, fi, va: (e[s], 0, f[s])),
            ],
            out_specs=pl.BlockSpec((MOE_TM, MOE_FT), lambda s, tb, e, f, fi, va: (tb[s], f[s])),
            scratch_shapes=[pltpu.VMEM((d, MOE_FT), jnp.bfloat16), pltpu.VMEM((d, MOE_FT), jnp.bfloat16)],
        ),
        out_shape=jax.ShapeDtypeStruct((a_pad, d_exp), jnp.bfloat16),
        compiler_params=_cparams(("arbitrary",)),
        name="expert_gate_up",
    )(*wl1, xs, w_gate, w_up)
    wl2 = _work_list(block_e, start_blk, n_blk, total_blocks, d // MOE_NT, n_tb_max)
    return pl.pallas_call(
        _down_kernel,
        grid_spec=pltpu.PrefetchScalarGridSpec(
            num_scalar_prefetch=5,
            grid=(wl2[0].shape[0],),
            in_specs=[
                pl.BlockSpec((MOE_TM, d_exp), lambda s, tb, e, f, fi, va: (tb[s], 0)),
                pl.BlockSpec((1, d_exp, MOE_NT), lambda s, tb, e, f, fi, va: (e[s], 0, f[s])),
            ],
            out_specs=pl.BlockSpec((MOE_TM, MOE_NT // 2), lambda s, tb, e, f, fi, va: (tb[s], f[s])),
            scratch_shapes=[pltpu.VMEM((d_exp, MOE_NT), jnp.bfloat16)],
        ),
        out_shape=jax.ShapeDtypeStruct((a_pad, d // 2), jnp.uint32),
        compiler_params=_cparams(("arbitrary",)),
        name="expert_down",
    )(*wl2, hidden, w_down)


def _combine_kernel(h_ref, y0_ref, y1_ref, gate_ref, op_ref, os_ref, *, prompt_tiles):
    def write(o_ref):
        g = gate_ref[...]
        half = MOE_NT // 2
        for t in range(o_ref.shape[1] // MOE_NT):
            w = slice(t * half, (t + 1) * half)
            lo0, hi0 = _unpack_bf16_pair(y0_ref[0, :, w])
            lo1, hi1 = _unpack_bf16_pair(y1_ref[0, :, w])
            c = t * MOE_NT
            o_ref[:, c:c + half] = h_ref[:, c:c + half] + (g[:, 0:1] * lo0 + g[:, 1:2] * lo1)
            o_ref[:, c + half:c + MOE_NT] = h_ref[:, c + half:c + MOE_NT] + (g[:, 0:1] * hi0 + g[:, 1:2] * hi1)

    @pl.when(pl.program_id(0) < prompt_tiles)
    def _():
        write(op_ref)

    @pl.when(pl.program_id(0) >= prompt_tiles)
    def _():
        write(os_ref)


def _combine(h1, y2, gate_slab, n_p):
    n, d = h1.shape
    tm = _pick(math.gcd(n_p, n - n_p), (256, 128, 64, 32, 16, 8))
    pt = n_p // tm
    kern = functools.partial(_combine_kernel, prompt_tiles=pt)
    return pl.pallas_call(
        kern,
        grid=(n // tm,),
        in_specs=[pl.BlockSpec((tm, d), lambda i: (i, 0)), pl.BlockSpec((1, tm, d // 2), lambda i: (0, i, 0)),
                  pl.BlockSpec((1, tm, d // 2), lambda i: (1, i, 0)), pl.BlockSpec((tm, LANES), lambda i: (i, 0))],
        out_specs=[pl.BlockSpec((tm, d), lambda i: (jnp.minimum(i, pt - 1), 0)),
                   pl.BlockSpec((tm, d), lambda i: (jnp.maximum(i - pt, 0), 0))],
        out_shape=[jax.ShapeDtypeStruct((n_p, d), jnp.float32), jax.ShapeDtypeStruct((n - n_p, d), jnp.float32)],
        compiler_params=_cparams(("arbitrary",)),
        name="moe_combine",
    )(h1, y2, y2, gate_slab)


def _moe(h1, n_p, g_ffn, w_rg, b_rg, w_re, b_re, w_gate, w_up, w_down):
    n, d = h1.shape
    w_r = jnp.zeros((d, LANES), jnp.float32).at[:, :N_GROUPS].set(w_rg).at[:, N_GROUPS:N_GROUPS + N_EXPERTS].set(w_re)
    b_r = jnp.zeros((1, LANES), jnp.float32).at[0, :N_GROUPS].set(b_rg).at[0, N_GROUPS:N_GROUPS + N_EXPERTS].set(b_re)
    xn, eid_slab, gate_slab = _router(h1, g_ffn, w_r, b_r)
    flat = eid_slab[:, :EXPERT_TOPK].reshape(n * EXPERT_TOPK)
    onehot = (flat[:, None] == jnp.arange(N_EXPERTS, dtype=jnp.int32)[None, :]).astype(jnp.int32)
    rank = jnp.take_along_axis(jnp.cumsum(onehot, axis=0) - onehot, flat[:, None], axis=1)[:, 0]
    counts = jnp.sum(onehot, axis=0)
    n_blk = (counts + MOE_TM - 1) // MOE_TM
    start_blk = jnp.cumsum(n_blk) - n_blk
    total_blocks = jnp.sum(n_blk)
    slot = start_blk[flat] * MOE_TM + rank
    n_tb_max = -(-(n * EXPERT_TOPK) // MOE_TM) + N_EXPERTS
    a_pad = n_tb_max * MOE_TM
    src_tok = jnp.zeros((a_pad,), jnp.int32).at[slot].set(jnp.arange(n * EXPERT_TOPK, dtype=jnp.int32) // EXPERT_TOPK)
    end_blk = jnp.cumsum(n_blk)
    owner = jnp.sum((end_blk[None, :] <= jnp.arange(n_tb_max, dtype=jnp.int32)[:, None]).astype(jnp.int32), axis=1)
    block_e = jnp.minimum(owner, N_EXPERTS - 1).astype(jnp.int32)
    xs = _gather_rows(src_tok, xn)
    yb = _experts(xs, w_gate, w_up, w_down, block_e, start_blk.astype(jnp.int32), n_blk.astype(jnp.int32),
                  total_blocks.astype(jnp.int32))
    back = slot.reshape(n, EXPERT_TOPK).T.reshape(n * EXPERT_TOPK).astype(jnp.int32)
    y2 = _gather_rows(back, yb).reshape(EXPERT_TOPK, n, d // 2)
    return _combine(h1, y2, gate_slab, n_p)


def _rope_tables(pos):
    half = HEAD_DIM // 2
    inv = jnp.exp(-math.log(ROPE_THETA) * jnp.arange(half, dtype=jnp.float32) * (2.0 / HEAD_DIM))
    ang = pos.astype(jnp.float32)[:, None] * inv[None, :]
    cos, sin = jnp.cos(ang), jnp.sin(ang)
    return jnp.concatenate([cos, cos], axis=1), jnp.concatenate([-sin, sin], axis=1)


def _repack_weights(w_in, b_gate, b_forget, g_nsa_q, g_nsa_k, g_fox_q, g_fox_k):
    o1 = NSA_HEADS * HEAD_DIM
    o2 = o1 + N_BRANCH * KV_COLS
    o3 = o2 + N_GATE
    o4 = o3 + FOX_HEADS * HEAD_DIM
    o5 = o4 + KV_COLS
    d = w_in.shape[0]
    w_main = jnp.concatenate([w_in[:, :o2], w_in[:, o3:o5]], axis=1).astype(jnp.bfloat16)
    n_f = w_in.shape[1] - o5
    w_small = jnp.zeros((d, LANES), jnp.float32).at[:, :N_GATE].set(w_in[:, o2:o3])
    w_small = w_small.at[:, COL_LOGF:COL_LOGF + n_f].set(w_in[:, o5:]).astype(jnp.bfloat16)
    b_small = jnp.zeros((1, LANES), jnp.float32).at[0, :N_GATE].set(b_gate.reshape(N_GATE))
    b_small = b_small.at[0, COL_LOGF:COL_LOGF + n_f].set(b_forget)
    ones = jnp.ones((KV_HEADS * HEAD_DIM,), jnp.float32)
    gains = [jnp.tile(g_nsa_q, NSA_HEADS)]
    for br in range(N_BRANCH):
        gains += [jnp.tile(g_nsa_k[br], KV_HEADS), ones]
    gains += [jnp.tile(g_fox_q, FOX_HEADS), jnp.tile(g_fox_k, KV_HEADS), ones]
    return w_main, w_small, b_small, jnp.concatenate(gains).reshape(1, MAIN_COLS)


def _cmp_weights(w_k, w_v, pe_k, pe_v):
    def cat(w):
        return jnp.concatenate([w[:CMP_STRIDE], w[CMP_STRIDE:]], axis=2)
    return jnp.stack([cat(w_k), cat(w_v)]).astype(jnp.bfloat16), jnp.stack([pe_k, pe_v])


def _pad_rows(x, rows):
    return jnp.pad(x, ((0, 0), (0, rows - x.shape[1]), (0, 0)))


def kernel(x_prompt, x_sample, cache_nsa_cmp_kv, cache_nsa_slc_kv, cache_nsa_win_kv, cache_fox_kv, cache_fox_logf, page_table, g_attn_norm, w_in, b_nsa_gate, b_fox_forget, g_nsa_q, g_nsa_k, g_fox_q, g_fox_k, w_cmp_k, w_cmp_v, pe_cmp_k, pe_cmp_v, g_out_nsa, g_out_fox, w_out, g_ffn_norm, w_router_grp, b_router_grp, w_router_exp, b_router_exp, w_exp_gate, w_exp_up, w_exp_down):
    depth = w_in.shape[0]
    assert depth == 1, "single-layer step"
    bp, t, d = x_prompt.shape
    bs, tn, _ = x_sample.shape
    n_pages = page_table.shape[1]
    page = cache_fox_kv.shape[2]
    past = n_pages * page
    n_p, n_s = bp * t, bs * tn
    l = 0

    x_all = jnp.concatenate([x_prompt.reshape(n_p, d), x_sample.reshape(n_s, d)], axis=0)
    pos_all = jnp.concatenate([jnp.tile(jnp.arange(t, dtype=jnp.int32), bp),
                               jnp.tile(past + jnp.arange(tn, dtype=jnp.int32), bs)])
    cos_f, sin_s = _rope_tables(pos_all)
    w_main, w_small, b_small, gain_cols = _repack_weights(
        w_in[l], b_nsa_gate[l], b_fox_forget[l], g_nsa_q[l], g_nsa_k[l], g_fox_q[l], g_fox_k[l])

    xn = _rmsnorm(x_all, g_attn_norm[l], jnp.bfloat16)
    hf, hb = _inproj(xn, w_main, gain_cols, cos_f, sin_s)
    slab = _small_proj(xn, w_small, b_small)

    col_cmp, col_slc, col_win = (COL_KV_NSA + br * KV_COLS for br in range(N_BRANCH))
    w_cat, pe_cat = _cmp_weights(w_cmp_k[l], w_cmp_v[l], pe_cmp_k[l], pe_cmp_v[l])
    v_off = KV_HEADS * HEAD_DIM

    tq = _pick(t, (256, 128))
    kc_p = _compress_prompt(hf, bp, t, col_cmp, w_cat, pe_cat)
    n_slc_p = max(-(-t // SLC_BLOCK), SLC_TOPK)
    nsp_p = -(-n_slc_p // LANES) * LANES if n_slc_p > 64 else 64
    o_cmp_p, sel_p = _cmp_select(hb, 0, bp, t // tq, tq, kc_p, 0, t // CMP_STRIDE - CMP_RATIO + 1, n_slc_p, nsp_p,
                                 jnp.bfloat16)
    o_slc_p = _flash_prompt("slc", hb, bp, t, COL_Q_NSA, col_slc, col_slc + v_off, sel=sel_p)
    o_win_p = _win_prompt(hb, bp, t, COL_Q_NSA, col_win, col_win + v_off)
    c_p = _cumsum_prompt(slab[:n_p], bp, t)[:, COL_LOGF:COL_LOGF + FOX_HEADS]
    cq_p = c_p.reshape(n_p, KV_HEADS, REP).transpose(1, 0, 2)
    ck_p = c_p.T.reshape(KV_HEADS, REP, n_p)
    o_fox_p = _flash_prompt("fox", hb, bp, t, COL_Q_FOX, COL_KV_FOX, COL_KV_FOX + v_off, cq=cq_p, ck=ck_p)
    a_p = _merge(o_cmp_p, o_slc_p, o_win_p, o_fox_p, slab[:n_p], g_out_nsa[l], g_out_fox[l])

    hf_s, slab_s = hf[n_p:], slab[n_p:]
    q_nsa_s = hf_s[:, COL_Q_NSA:COL_Q_NSA + NSA_HEADS * HEAD_DIM]
    q_fox_s = hf_s[:, COL_Q_FOX:COL_Q_FOX + FOX_HEADS * HEAD_DIM]

    new_keys = -(-tn // CMP_STRIDE) * CMP_STRIDE

    def new_rows(col):
        rows = _pad_rows(hf_s[:, col:col + KV_COLS].reshape(bs, tn, KV_COLS), new_keys)
        return rows.reshape(bs, new_keys * ROW_SLABS, HEAD_DIM)

    pool = lambda c: c[l].reshape(c.shape[1], page * ROW_SLABS, HEAD_DIM)
    kc_s = _compress_sample(page_table, pool(cache_nsa_cmp_kv), new_rows(col_cmp), w_cat, pe_cat)
    t_ctx = past + tn
    n_cmp_s = -(-t_ctx // CMP_STRIDE) - CMP_RATIO + 1
    n_slc_s = max(-(-t_ctx // SLC_BLOCK), SLC_TOPK)
    blocks_per_step = PAGES_PER_STEP * page // SLC_BLOCK
    nsp_s = -(-(n_slc_s + 1) // LANES) * LANES
    nsp_s = -(-nsp_s // blocks_per_step) * blocks_per_step
    o_cmp_s, sel_s = _cmp_select(q_nsa_s, 0, bs, 1, tn, kc_s, past, n_cmp_s, n_slc_s, nsp_s, jnp.float32)
    sel5 = sel_s.reshape(KV_HEADS, bs, tn, nsp_s // blocks_per_step, blocks_per_step).transpose(1, 3, 0, 2, 4)
    o_slc_s = _paged_attn("slc", page_table, pool(cache_nsa_slc_kv), new_rows(col_slc), q_nsa_s, tn, sel5=sel5)
    win_buf = cache_nsa_win_kv[l].reshape(bs, -1, HEAD_DIM)
    o_win_s = _win_sample(win_buf, new_rows(col_win), q_nsa_s, tn, past)
    logf_s = slab_s[:, COL_LOGF:COL_LOGF + FOX_HEADS].reshape(bs, tn, FOX_HEADS)
    ck_s = _cumsum_sample(page_table, cache_fox_logf[l].transpose(0, 2, 1),
                          _pad_rows(logf_s, page).transpose(0, 2, 1))
    cq_s = ck_s[:, :, past:past + tn].transpose(0, 2, 1)
    o_fox_s = _paged_attn("fox", page_table, pool(cache_fox_kv), new_rows(COL_KV_FOX), q_fox_s, tn, cq=cq_s, ck=ck_s)
    a_s = _merge(o_cmp_s, o_slc_s, o_win_s, o_fox_s, slab_s, g_out_nsa[l], g_out_fox[l])

    h1 = _outproj(jnp.concatenate([a_p, a_s], axis=0), w_out[l].astype(jnp.bfloat16), x_all)
    y_p, y_s = _moe(h1, n_p, g_ffn_norm[l], w_router_grp[l], b_router_grp[l], w_router_exp[l], b_router_exp[l],
                    w_exp_gate[l], w_exp_up[l], w_exp_down[l])

    def kv_out(rows, col, bsz, tt):
        return rows[:, col:col + KV_COLS].reshape(1, bsz, tt, 2, KV_HEADS, HEAD_DIM)

    hf_p = hf[:n_p]
    win_p = kv_out(hf_p, col_win, bp, t)[:, :, t - min(WINDOW, t):]
    win_new = kv_out(hf_s, col_win, bs, tn)
    win_s = jnp.concatenate([cache_nsa_win_kv[l:l + 1][:, :, tn:], win_new], axis=2)
    logf_p = slab[:n_p, COL_LOGF:COL_LOGF + FOX_HEADS].reshape(1, bp, t, FOX_HEADS)
    return (y_p.reshape(bp, t, d), y_s.reshape(bs, tn, d),
            kv_out(hf_p, col_cmp, bp, t), kv_out(hf_s, col_cmp, bs, tn),
            kv_out(hf_p, col_slc, bp, t), kv_out(hf_s, col_slc, bs, tn),
            win_p, win_s,
            kv_out(hf_p, COL_KV_FOX, bp, t), kv_out(hf_s, COL_KV_FOX, bs, tn),
            logf_p, logf_s.reshape(1, bs, tn, FOX_HEADS))
```
